```python
import math
import jax, jax.numpy as jnp
from jax import lax
import numpy as np

D_MODEL = 1024
BATCH = 32
SEQ = 256
DEPTH = 2
DEC_BATCH = 8
DEC_SEQ = 2048
PAST_LEN = 256

GRID_W = 64
N_HEADS = 8
KV_HEADS = 2
HEAD_DIM = 64
Q_PER_KV = N_HEADS // KV_HEADS
ATTN_W = N_HEADS * HEAD_DIM
KV_W = KV_HEADS * HEAD_DIM
CONV_W = D_MODEL // 4
HYENA_W = D_MODEL // 4
MIX_W = ATTN_W + CONV_W + HYENA_W
IN_W = ATTN_W + 2 * KV_W + 2 * CONV_W + 3 * HYENA_W
WINDOW = 128
BLOCK = 128
CONV_K = 31
SHORT_K = 3
HYENA_ORDER = 2
HYENA_EMB = 33
HYENA_BANDS = (HYENA_EMB - 1) // 2
HYENA_HID = 64
D_FF = 2816
N_MOD = 9
ROPE_BASE = 10000.0
EPS = 1e-6
NEG_INF = -1e30

kernel_name = 'hybrid_prefix_diffusion_step'

F32 = jnp.float32


def rmsnorm(x, g):
    xf = x.astype(F32)
    y = xf * lax.rsqrt(jnp.mean(xf * xf, axis=-1, keepdims=True) + EPS)
    return (y * g.astype(F32)).astype(x.dtype)


def layernorm(x, g, b):
    xf = x.astype(F32)
    mu = jnp.mean(xf, axis=-1, keepdims=True)
    var = jnp.mean(jnp.square(xf - mu), axis=-1, keepdims=True)
    y = (xf - mu) * lax.rsqrt(var + EPS)
    return (y * g.astype(F32) + b.astype(F32)).astype(x.dtype)


def swiglu(h, wg, wu, wd):
    return (jax.nn.silu(h @ wg) * (h @ wu)) @ wd


def depthwise_conv(x, w, b):
    k = w.shape[0]
    y = lax.conv_general_dilated(x, w[:, None, :].astype(x.dtype), (1,), ((k // 2, k // 2),),
                                 dimension_numbers=('NWC', 'WIO', 'NWC'),
                                 feature_group_count=x.shape[-1])
    return y + b


def rope_2d(x):
    L = x.shape[1]
    rows = L // GRID_W
    row = jnp.repeat(jnp.arange(rows), GRID_W)
    col = jnp.arange(rows * GRID_W) % GRID_W
    nf = HEAD_DIM // 4
    inv = ROPE_BASE ** (-jnp.arange(nf, dtype=F32) / nf)
    xf = x.astype(F32)

    def rot(xp, pos):
        ang = pos.astype(F32)[:, None] * inv[None, :]
        cos = jnp.cos(ang)[None, :, None, :]
        sin = jnp.sin(ang)[None, :, None, :]
        a, b = xp[..., :nf], xp[..., nf:]
        return jnp.concatenate([a * cos - b * sin, b * cos + a * sin], axis=-1)

    half = HEAD_DIM // 2
    out = jnp.concatenate([rot(xf[..., :half], row), rot(xf[..., half:], col)], axis=-1)
    return out.astype(x.dtype)


def context_attention(q, k, v, sink):
    B, L = q.shape[0], q.shape[1]
    nb = L // BLOCK
    qb = q.reshape(B, nb, BLOCK, KV_HEADS, Q_PER_KV, HEAD_DIM).swapaxes(0, 1)
    sk = jnp.broadcast_to(sink.astype(F32).reshape(1, KV_HEADS, Q_PER_KV, 1, 1),
                          (B, KV_HEADS, Q_PER_KV, BLOCK, 1))
    scale = HEAD_DIM ** -0.5

    def one(qblk):
        s = jnp.einsum('bqgrd,bkgd->bgrqk', qblk, k).astype(F32) * scale
        p = jax.nn.softmax(jnp.concatenate([s, sk], axis=-1), axis=-1)[..., :-1]
        return jnp.einsum('bgrqk,bkgd->bqgrd', p.astype(v.dtype), v)

    o = lax.map(one, qb)
    return o.swapaxes(0, 1).reshape(B, L, ATTN_W)


def latent_attention(q, k, v, kc, vc, sink):
    B, L = q.shape[0], q.shape[1]
    nb = L // BLOCK
    nloc = 3 * BLOCK
    qb = q.reshape(B, nb, BLOCK, KV_HEADS, Q_PER_KV, HEAD_DIM).swapaxes(0, 1)
    pad = ((0, 0), (BLOCK, BLOCK), (0, 0), (0, 0))
    kp = jnp.pad(k, pad)
    vp = jnp.pad(v, pad)
    qi = jnp.arange(BLOCK)
    kj = jnp.arange(nloc)
    band = jnp.abs(kj[None, :] - BLOCK - qi[:, None]) <= WINDOW
    sk = jnp.broadcast_to(sink.astype(F32).reshape(1, KV_HEADS, Q_PER_KV, 1, 1),
                          (B, KV_HEADS, Q_PER_KV, BLOCK, 1))
    scale = HEAD_DIM ** -0.5

    def one(args):
        qblk, b = args
        kw = lax.dynamic_slice_in_dim(kp, b * BLOCK, nloc, axis=1)
        vw = lax.dynamic_slice_in_dim(vp, b * BLOCK, nloc, axis=1)
        j = (b - 1) * BLOCK + kj
        valid = band & ((j >= 0) & (j < L))[None, :]
        s_loc = jnp.einsum('bqgrd,bkgd->bgrqk', qblk, kw).astype(F32) * scale
        s_loc = jnp.where(valid, s_loc, NEG_INF)
        s_ctx = jnp.einsum('bqgrd,bcgd->bgrqc', qblk, kc).astype(F32) * scale
        p = jax.nn.softmax(jnp.concatenate([s_loc, s_ctx, sk], axis=-1), axis=-1)
        o = jnp.einsum('bgrqk,bkgd->bqgrd', p[..., :nloc].astype(v.dtype), vw)
        o = o + jnp.einsum('bgrqc,bcgd->bqgrd', p[..., nloc:-1].astype(vc.dtype), vc)
        return o

    o = lax.map(one, (qb, jnp.arange(nb)))
    return o.swapaxes(0, 1).reshape(B, L, ATTN_W)


def conformer_conv(u, dw, dwb, lng, lnb, pw):
    a, g = jnp.split(u, 2, axis=-1)
    y = a * jax.nn.sigmoid(g)
    y = depthwise_conv(y, dw, dwb)
    y = jax.nn.silu(layernorm(y, lng, lnb))
    return y @ pw


def hyena_filters(L, w1, b1, f1, w2, b2, f2, w3, log_decay):
    t = jnp.arange(L, dtype=F32)
    tn = t / (L - 1)
    bands = jnp.linspace(1e-4, HYENA_BANDS - 1, HYENA_BANDS, dtype=F32)
    ang = 2.0 * math.pi * t[:, None] * bands[None, :] / L
    z = jnp.concatenate([tn[:, None], jnp.cos(ang), -jnp.sin(ang)], axis=-1)
    h = jnp.sin(f1.astype(F32) * (z @ w1.astype(F32) + b1.astype(F32)))
    h = jnp.sin(f2.astype(F32) * (h @ w2.astype(F32) + b2.astype(F32)))
    h = (h @ w3.astype(F32)).reshape(L, HYENA_ORDER, 2, HYENA_W)
    decay = jnp.exp(log_decay.astype(F32)).reshape(HYENA_ORDER, 2, HYENA_W)
    h = h * jnp.exp(-tn[:, None, None, None] * decay[None])
    h = h * lax.rsqrt(jnp.sum(h * h, axis=(0, 2), keepdims=True) + EPS)
    return h


def bidir_longconv(u, hf, hb):
    L, C = u.shape[1], u.shape[2]
    hc = jnp.concatenate([hf, jnp.zeros((1, C), F32), hb[1:][::-1]], axis=0)
    U = jnp.fft.rfft(u.astype(F32), n=2 * L, axis=1)
    H = jnp.fft.rfft(hc, axis=0)
    y = jnp.fft.irfft(U * H[None], n=2 * L, axis=1)[:, :L]
    return y.astype(u.dtype)


def hyena_mixer(u, sw, sb, w1, b1, f1, w2, b2, f2, w3, log_decay, bias):
    L = u.shape[1]
    u = depthwise_conv(u, sw, sb)
    vv, x1, x2 = jnp.split(u, 3, axis=-1)
    h = hyena_filters(L, w1, b1, f1, w2, b2, f2, w3, log_decay)
    y = x1 * (bidir_longconv(vv, h[:, 0, 0], h[:, 0, 1]) + bias[0] * vv)
    y = x2 * (bidir_longconv(y, h[:, 1, 0], h[:, 1, 1]) + bias[1] * y)
    return y


def token_mix(h, l, P, cache):
    B, L = h.shape[0], h.shape[1]
    z = h @ P['w_in'][l]
    o1 = ATTN_W
    o2 = o1 + KV_W
    o3 = o2 + KV_W
    o4 = o3 + 2 * CONV_W
    q = z[..., :o1].reshape(B, L, N_HEADS, HEAD_DIM)
    k = z[..., o1:o2].reshape(B, L, KV_HEADS, HEAD_DIM)
    v = z[..., o2:o3].reshape(B, L, KV_HEADS, HEAD_DIM)
    if cache is None:
        att = context_attention(q, k, v, P['attn_sink'][l])
        kv_out = (k, v)
    else:
        att = latent_attention(rope_2d(q), rope_2d(k), v, cache[0], cache[1], P['attn_sink'][l])
        kv_out = None
    conv = conformer_conv(z[..., o3:o4], P['conv_dw'][l], P['conv_dw_b'][l],
                          P['conv_ln_g'][l], P['conv_ln_b'][l], P['conv_pw'][l])
    hy = hyena_mixer(z[..., o4:], P['hy_short_w'][l], P['hy_short_b'][l],
                     P['hy_w1'][l], P['hy_b1'][l], P['hy_f1'][l],
                     P['hy_w2'][l], P['hy_b2'][l], P['hy_f2'][l],
                     P['hy_w3'][l], P['hy_log_decay'][l], P['hy_bias'][l])
    out = jnp.concatenate([att, conv, hy], axis=-1) @ P['w_out'][l]
    return out, kv_out


def trunk_layer(x, cond, l, P, cache):
    mod = (jax.nn.silu(cond) @ P['w_mod'][l] + P['b_mod'][l])[:, None, :]
    sh1, sc1, g1, sh2, sc2, g2, sh3, sc3, g3 = jnp.split(mod, N_MOD, axis=-1)
    h = rmsnorm(x, P['g_ffn1'][l]) * (1 + sc1) + sh1
    x = x + 0.5 * g1 * swiglu(h, P['w1_gate'][l], P['w1_up'][l], P['w1_down'][l])
    h = rmsnorm(x, P['g_mix'][l]) * (1 + sc2) + sh2
    m, kv = token_mix(h, l, P, cache)
    x = x + g2 * m
    h = rmsnorm(x, P['g_ffn2'][l]) * (1 + sc3) + sh3
    x = x + 0.5 * g3 * swiglu(h, P['w2_gate'][l], P['w2_up'][l], P['w2_down'][l])
    return x, kv


def setup_inputs(seed: int = 0) -> dict:
    key = jax.random.key(seed)
    keys = iter(jax.random.split(key, 48))

    def nrm(shape, std):
        return std * jax.random.normal(next(keys), shape, F32)

    def gain(shape):
        return 1.0 + nrm(shape, 0.05)

    D = D_MODEL
    cache_shape = (DEC_BATCH, DEPTH, PAST_LEN, KV_HEADS, HEAD_DIM)
    return {
        'x_prompt': nrm((BATCH, SEQ, D), 1.0),
        'x_sample': nrm((DEC_BATCH, DEC_SEQ, D), 1.0),
        'c': nrm((DEC_BATCH, D), 1.0),
        'cache_k': nrm(cache_shape, 1.0),
        'cache_v': nrm(cache_shape, 1.0),
        'c_ctx': nrm((D,), 1.0),
        'w_mod': nrm((DEPTH, D, N_MOD * D), 0.5 * D ** -0.5),
        'b_mod': nrm((DEPTH, N_MOD * D), 0.02),
        'g_ffn1': gain((DEPTH, D)),
        'g_mix': gain((DEPTH, D)),
        'g_ffn2': gain((DEPTH, D)),
        'g_final': gain((D,)),
        'w1_gate': nrm((DEPTH, D, D_FF), D ** -0.5),
        'w1_up': nrm((DEPTH, D, D_FF), D ** -0.5),
        'w1_down': nrm((DEPTH, D_FF, D), D_FF ** -0.5),
        'w2_gate': nrm((DEPTH, D, D_FF), D ** -0.5),
        'w2_up': nrm((DEPTH, D, D_FF), D ** -0.5),
        'w2_down': nrm((DEPTH, D_FF, D), D_FF ** -0.5),
        'w_in': nrm((DEPTH, D, IN_W), D ** -0.5),
        'w_out': nrm((DEPTH, MIX_W, D), MIX_W ** -0.5),
        'attn_sink': nrm((DEPTH, N_HEADS), 0.5),
        'conv_dw': nrm((DEPTH, CONV_K, CONV_W), CONV_K ** -0.5),
        'conv_dw_b': nrm((DEPTH, CONV_W), 0.02),
        'conv_ln_g': gain((DEPTH, CONV_W)),
        'conv_ln_b': nrm((DEPTH, CONV_W), 0.02),
        'conv_pw': nrm((DEPTH, CONV_W, CONV_W), CONV_W ** -0.5),
        'hy_short_w': nrm((DEPTH, SHORT_K, 3 * HYENA_W), SHORT_K ** -0.5),
        'hy_short_b': nrm((DEPTH, 3 * HYENA_W), 0.02),
        'hy_w1': nrm((DEPTH, HYENA_EMB, HYENA_HID), HYENA_EMB ** -0.5),
        'hy_b1': nrm((DEPTH, HYENA_HID), 0.1),
        'hy_f1': gain((DEPTH, HYENA_HID)),
        'hy_w2': nrm((DEPTH, HYENA_HID, HYENA_HID), HYENA_HID ** -0.5),
        'hy_b2': nrm((DEPTH, HYENA_HID), 0.1),
        'hy_f2': gain((DEPTH, HYENA_HID)),
        'hy_w3': nrm((DEPTH, HYENA_HID, HYENA_ORDER * 2 * HYENA_W), HYENA_HID ** -0.5),
        'hy_log_decay': jax.random.uniform(next(keys), (DEPTH, HYENA_ORDER * 2 * HYENA_W), F32,
                                           math.log(3.0), math.log(15.0)),
        'hy_bias': nrm((DEPTH, HYENA_ORDER, HYENA_W), 0.5),
    }


def reference(x_prompt, x_sample, c, cache_k, cache_v, c_ctx, w_mod, b_mod, g_ffn1, g_mix, g_ffn2,
              g_final, w1_gate, w1_up, w1_down, w2_gate, w2_up, w2_down, w_in, w_out, attn_sink,
              conv_dw, conv_dw_b, conv_ln_g, conv_ln_b, conv_pw, hy_short_w, hy_short_b,
              hy_w1, hy_b1, hy_f1, hy_w2, hy_b2, hy_f2, hy_w3, hy_log_decay, hy_bias):
    P = dict(w_mod=w_mod, b_mod=b_mod, g_ffn1=g_ffn1, g_mix=g_mix, g_ffn2=g_ffn2,
             w1_gate=w1_gate, w1_up=w1_up, w1_down=w1_down,
             w2_gate=w2_gate, w2_up=w2_up, w2_down=w2_down,
             w_in=w_in, w_out=w_out, attn_sink=attn_sink,
             conv_dw=conv_dw, conv_dw_b=conv_dw_b, conv_ln_g=conv_ln_g, conv_ln_b=conv_ln_b,
             conv_pw=conv_pw, hy_short_w=hy_short_w, hy_short_b=hy_short_b,
             hy_w1=hy_w1, hy_b1=hy_b1, hy_f1=hy_f1, hy_w2=hy_w2, hy_b2=hy_b2, hy_f2=hy_f2,
             hy_w3=hy_w3, hy_log_decay=hy_log_decay, hy_bias=hy_bias)

    xp = x_prompt
    cond_ctx = c_ctx[None, :]
    ks = []
    vs = []
    for l in range(DEPTH):
        xp, kv = trunk_layer(xp, cond_ctx, l, P, None)
        ks.append(kv[0])
        vs.append(kv[1])

    xs = x_sample
    for l in range(DEPTH):
        xs, _ = trunk_layer(xs, c, l, P, (cache_k[:, l], cache_v[:, l]))

    y_prompt = rmsnorm(xp, g_final)
    y_sample = rmsnorm(xs, g_final)
    new_k = jnp.stack(ks, axis=1)
    new_v = jnp.stack(vs, axis=1)
    return (y_prompt, y_sample, new_k, new_v)
```

```python
import functools
import math

import numpy as np
import jax
import jax.numpy as jnp
from jax import lax
from jax.experimental import pallas as pl
from jax.experimental.pallas import tpu as pltpu

F32 = jnp.float32
BF16 = jnp.bfloat16

D_MODEL = 1024
DEPTH = 2
GRID_W = 64
N_HEADS = 8
KV_HEADS = 2
HEAD_DIM = 64
Q_PER_KV = N_HEADS // KV_HEADS
ATTN_W = N_HEADS * HEAD_DIM
KV_W = KV_HEADS * HEAD_DIM
CONV_W = D_MODEL // 4
HYENA_W = D_MODEL // 4
WINDOW = 128
BLOCK = 128
CONV_K = 31
HYENA_EMB = 33
HYENA_BANDS = (HYENA_EMB - 1) // 2
HYENA_HID = 64
D_FF = 2816
N_MOD = 9
ROPE_BASE = 10000.0
EPS = 1e-6
NEG_INF = -1e30

LANE = 128
MXU_W = 256
COND_ROWS = 16
TOKEN_TILE = 512
CONV_CHUNK = 128
CONV_PAD = 16
VMEM_LIMIT = 56 * 1024 * 1024


def _params(sem, vmem=None):
    return pltpu.CompilerParams(dimension_semantics=sem, vmem_limit_bytes=vmem)


def _silu(x):
    return x * jax.nn.sigmoid(x)


def _dot(a, b):
    return jnp.dot(a, b, preferred_element_type=F32)


def _dot_nt(a, b):
    return lax.dot_general(a, b, (((1,), (1,)), ((), ())), preferred_element_type=F32)


def _dot_hi(a, b):
    return jnp.dot(a, b, preferred_element_type=F32, precision=lax.Precision.HIGHEST)


def _mod_norm(x, gain, scale, shift):
    y = x * lax.rsqrt(jnp.mean(x * x, axis=-1, keepdims=True) + EPS)
    return (y * gain) * (1.0 + scale) + shift


def _mod_kernel(c_ref, w_ref, b_ref, o_ref):
    s = _silu(c_ref[...]).astype(BF16)
    o_ref[...] = _dot(s, w_ref[...].astype(BF16)) + b_ref[...]


def _modulation(conds, w_mod, b_mod):
    d = D_MODEL
    return pl.pallas_call(
        _mod_kernel,
        grid=(DEPTH, N_MOD),
        in_specs=[
            pl.BlockSpec((COND_ROWS, d), lambda l, j: (0, 0)),
            pl.BlockSpec((None, d, d), lambda l, j: (l, 0, j)),
            pl.BlockSpec((None, 1, d), lambda l, j: (l, 0, j)),
        ],
        out_specs=pl.BlockSpec((None, COND_ROWS, d), lambda l, j: (l, 0, j)),
        out_shape=jax.ShapeDtypeStruct((DEPTH, COND_ROWS, N_MOD * d), F32),
        compiler_params=_params(("arbitrary", "arbitrary")),
        name="modulation",
    )(conds, w_mod, b_mod.reshape(DEPTH, 1, N_MOD * d))


def _swiglu_into(h16, wg_ref, wu_ref, wd_ref, a_ref):
    for c in range(D_FF // MXU_W):
        sl = slice(c * MXU_W, (c + 1) * MXU_W)
        g = _dot(h16, wg_ref[:, sl])
        u = _dot(h16, wu_ref[:, sl])
        a_ref[:, sl] = (_silu(g) * u).astype(BF16)
    return _dot(a_ref[...], wd_ref[...])


def _ffn_kernel(x_ref, mod_ref, g_ref, wg_ref, wu_ref, wd_ref, o_ref, a_ref, *, mi):
    x = x_ref[...]
    m = mod_ref[...]
    h = _mod_norm(x, g_ref[...], m[mi + 1:mi + 2], m[mi:mi + 1])
    y = _swiglu_into(h.astype(BF16), wg_ref, wu_ref, wd_ref, a_ref)
    o_ref[...] = x + (0.5 * m[mi + 2:mi + 3]) * y


def _resident(shape, index_map):
    return pl.BlockSpec(shape, index_map, pipeline_mode=pl.Buffered(1))


def _mod_spec(l, row0, rows_per_cond, tm):
    return pl.BlockSpec((None, None, N_MOD, D_MODEL),
                        lambda i: (l, row0 + (i * tm) // rows_per_cond, 0, 0))


def _ffn(x, mod4, gain, wg, wu, wd, *, l, mi, row0, rows_per_cond):
    t, d = x.shape
    tm = TOKEN_TILE
    wspec_in = _resident((None, d, D_FF), lambda i: (l, 0, 0))
    return pl.pallas_call(
        functools.partial(_ffn_kernel, mi=mi),
        grid=(t // tm,),
        in_specs=[
            pl.BlockSpec((tm, d), lambda i: (i, 0)),
            _mod_spec(l, row0, rows_per_cond, tm),
            pl.BlockSpec((None, 1, d), lambda i: (l, 0, 0)),
            wspec_in, wspec_in,
            _resident((None, D_FF, d), lambda i: (l, 0, 0)),
        ],
        out_specs=pl.BlockSpec((tm, d), lambda i: (i, 0)),
        out_shape=jax.ShapeDtypeStruct((t, d), F32),
        scratch_shapes=[pltpu.VMEM((tm, D_FF), BF16)],
        compiler_params=_params(("arbitrary",), VMEM_LIMIT),
        name="ffn",
    )(x, mod4, gain.reshape(DEPTH, 1, d), wg, wu, wd)


_IN_SPLITS = (ATTN_W, KV_W, KV_W, 2 * CONV_W, 3 * HYENA_W)


def _inproj_kernel(x_ref, mod_ref, g_ref, w_ref, q_ref, k_ref, v_ref, cv_ref, hy_ref):
    m = mod_ref[...]
    h = _mod_norm(x_ref[...], g_ref[...], m[4:5], m[3:4]).astype(BF16)
    off = 0
    for ref, width in zip((q_ref, k_ref, v_ref, cv_ref, hy_ref), _IN_SPLITS):
        ref[...] = _dot(h, w_ref[:, off:off + width])
        off += width


def _inproj(x, mod4, gain, w_in, *, l, row0, rows_per_cond):
    t, d = x.shape
    tm = TOKEN_TILE
    return pl.pallas_call(
        _inproj_kernel,
        grid=(t // tm,),
        in_specs=[
            pl.BlockSpec((tm, d), lambda i: (i, 0)),
            _mod_spec(l, row0, rows_per_cond, tm),
            pl.BlockSpec((None, 1, d), lambda i: (l, 0, 0)),
            _resident((None, d, sum(_IN_SPLITS)), lambda i: (l, 0, 0)),
        ],
        out_specs=[pl.BlockSpec((tm, w), lambda i: (i, 0)) for w in _IN_SPLITS],
        out_shape=[jax.ShapeDtypeStruct((t, w), F32) for w in _IN_SPLITS],
        compiler_params=_params(("arbitrary",), VMEM_LIMIT),
        name="inproj",
    )(x, mod4, gain.reshape(DEPTH, 1, d), w_in)


def _ctx_attn_kernel(sink_ref, q_ref, k_ref, v_ref, o_ref, *, l):
    q = q_ref[...] * (HEAD_DIM ** -0.5)
    k = k_ref[...].astype(BF16)
    v = v_ref[...].astype(BF16)
    for h in range(N_HEADS):
        g = h // Q_PER_KV
        gs = slice(g * HEAD_DIM, (g + 1) * HEAD_DIM)
        hs = slice(h * HEAD_DIM, (h + 1) * HEAD_DIM)
        s = _dot_nt(q[:, hs].astype(BF16), k[:, gs])
        sk = sink_ref[l, h]
        mx = jnp.maximum(jnp.max(s, axis=-1, keepdims=True), sk)
        p = jnp.exp(s - mx)
        den = jnp.sum(p, axis=-1, keepdims=True) + jnp.exp(sk - mx)
        o_ref[:, hs] = _dot(p.astype(BF16), v[:, gs]) / den


def _ctx_attention(q, k, v, sink, *, l, batch, seq):
    return pl.pallas_call(
        functools.partial(_ctx_attn_kernel, l=l),
        grid=(batch,),
        in_specs=[
            pl.BlockSpec(memory_space=pltpu.SMEM),
            pl.BlockSpec((seq, ATTN_W), lambda b: (b, 0)),
            pl.BlockSpec((seq, KV_W), lambda b: (b, 0)),
            pl.BlockSpec((seq, KV_W), lambda b: (b, 0)),
        ],
        out_specs=pl.BlockSpec((seq, ATTN_W), lambda b: (b, 0)),
        out_shape=jax.ShapeDtypeStruct((batch * seq, ATTN_W), F32),
        compiler_params=_params(("arbitrary",)),
        name="ctx_attention",
    )(sink, q, k, v)


def _rope(x, cos, sin_signed):
    w = x.shape[-1]
    lane = lax.broadcasted_iota(jnp.int32, x.shape, 1)
    quarter = HEAD_DIM // 4
    partner = jnp.where(lane % (2 * quarter) < quarter,
                        pltpu.roll(x, w - quarter, 1), pltpu.roll(x, quarter, 1))
    return x * cos + partner * sin_signed


def _lat_attn_kernel(sink_ref, q_ref, k_ref, v_ref, kc_ref, vc_ref, cos_ref, sin_ref, o_ref, *, l, seq):
    i = pl.program_id(1)
    nloc = 3 * BLOCK
    start = pl.multiple_of(jnp.clip((i - 1) * BLOCK, 0, seq - nloc), BLOCK)
    q0 = pl.multiple_of(i * BLOCK, BLOCK)

    kw = _rope(k_ref[pl.ds(start, nloc), :], cos_ref[pl.ds(start, nloc), :],
               sin_ref[pl.ds(start, nloc), :]).astype(BF16)
    vw = v_ref[pl.ds(start, nloc), :].astype(BF16)
    cq = cos_ref[pl.ds(q0, BLOCK), :]
    sq = sin_ref[pl.ds(q0, BLOCK), :]
    reps = ATTN_W // KV_W
    q = _rope(q_ref[...], jnp.concatenate([cq] * reps, axis=1), jnp.concatenate([sq] * reps, axis=1))
    q = q * (HEAD_DIM ** -0.5)
    kc = kc_ref[...].astype(BF16)
    vc = vc_ref[...].astype(BF16)

    jpos = start + lax.broadcasted_iota(jnp.int32, (BLOCK, nloc), 1)
    qpos = q0 + lax.broadcasted_iota(jnp.int32, (BLOCK, nloc), 0)
    valid = jnp.abs(jpos - qpos) <= WINDOW

    for h in range(N_HEADS):
        g = h // Q_PER_KV
        gs = slice(g * HEAD_DIM, (g + 1) * HEAD_DIM)
        hs = slice(h * HEAD_DIM, (h + 1) * HEAD_DIM)
        qh = q[:, hs].astype(BF16)
        s_loc = jnp.where(valid, _dot_nt(qh, kw[:, gs]), NEG_INF)
        s_ctx = _dot_nt(qh, kc[:, gs])
        sk = sink_ref[l, h]
        mx = jnp.maximum(jnp.maximum(jnp.max(s_loc, axis=-1, keepdims=True),
                                     jnp.max(s_ctx, axis=-1, keepdims=True)), sk)
        p_loc = jnp.exp(s_loc - mx)
        p_ctx = jnp.exp(s_ctx - mx)
        den = (jnp.sum(p_loc, axis=-1, keepdims=True) + jnp.sum(p_ctx, axis=-1, keepdims=True)
               + jnp.exp(sk - mx))
        o = _dot(p_loc.astype(BF16), vw[:, gs]) + _dot(p_ctx.astype(BF16), vc[:, gs])
        o_ref[:, hs] = o / den


def _lat_attention(q, k, v, cache_k, cache_v, cos_t, sin_t, sink, *, l, batch, seq):
    nb = seq // BLOCK
    past = cache_k.shape[2]
    kv_spec = pl.BlockSpec((seq, KV_W), lambda b, i: (b, 0))
    cache_spec = pl.BlockSpec((None, None, past, KV_W), lambda b, i: (b, l, 0, 0))
    table_spec = pl.BlockSpec((seq, KV_W), lambda b, i: (0, 0))
    return pl.pallas_call(
        functools.partial(_lat_attn_kernel, l=l, seq=seq),
        grid=(batch, nb),
        in_specs=[
            pl.BlockSpec(memory_space=pltpu.SMEM),
            pl.BlockSpec((BLOCK, ATTN_W), lambda b, i: (b * nb + i, 0)),
            kv_spec, kv_spec, cache_spec, cache_spec, table_spec, table_spec,
        ],
        out_specs=pl.BlockSpec((BLOCK, ATTN_W), lambda b, i: (b * nb + i, 0)),
        out_shape=jax.ShapeDtypeStruct((batch * seq, ATTN_W), F32),
        compiler_params=_params(("arbitrary", "arbitrary")),
        name="lat_attention",
    )(sink, q, k, v, cache_k, cache_v, cos_t, sin_t)


def _rope_tables(seq):
    rows = seq // GRID_W
    row = jnp.repeat(jnp.arange(rows), GRID_W)
    col = jnp.arange(rows * GRID_W) % GRID_W
    nf = HEAD_DIM // 4
    inv = ROPE_BASE ** (-jnp.arange(nf, dtype=F32) / nf)
    ang_r = row.astype(F32)[:, None] * inv[None, :]
    ang_c = col.astype(F32)[:, None] * inv[None, :]
    cos_h = jnp.concatenate([jnp.cos(ang_r)] * 2 + [jnp.cos(ang_c)] * 2, axis=1)
    sin_h = jnp.concatenate([-jnp.sin(ang_r), jnp.sin(ang_r), -jnp.sin(ang_c), jnp.sin(ang_c)], axis=1)
    return jnp.tile(cos_h, (1, KV_HEADS)), jnp.tile(sin_h, (1, KV_HEADS))


def _conv_kernel(cv_ref, dw_ref, dwb_ref, lng_ref, lnb_ref, pw_ref, o_ref, pad_ref, *, seq):
    x = cv_ref[...]
    zeros = jnp.zeros((CONV_PAD, CONV_W), F32)
    pad_ref[0:CONV_PAD, :] = zeros
    pad_ref[CONV_PAD:CONV_PAD + seq, :] = x[:, :CONV_W] * jax.nn.sigmoid(x[:, CONV_W:])
    pad_ref[CONV_PAD + seq:2 * CONV_PAD + seq, :] = zeros
    dw = dw_ref[...]
    pw = pw_ref[...].astype(BF16)
    first = CONV_PAD - CONV_K // 2

    def chunk(c, carry):
        base = pl.multiple_of(c * CONV_CHUNK, CONV_CHUNK)
        win = pad_ref[pl.ds(base, CONV_CHUNK + 2 * CONV_PAD), :]
        acc = jnp.zeros((CONV_CHUNK, CONV_W), F32)
        for k in range(CONV_K):
            acc = acc + win[first + k:first + k + CONV_CHUNK] * dw[k:k + 1]
        acc = acc + dwb_ref[...]
        mu = jnp.mean(acc, axis=-1, keepdims=True)
        cen = acc - mu
        var = jnp.mean(cen * cen, axis=-1, keepdims=True)
        y = cen * lax.rsqrt(var + EPS) * lng_ref[...] + lnb_ref[...]
        o_ref[pl.ds(base, CONV_CHUNK), :] = _dot(_silu(y).astype(BF16), pw)
        return carry

    lax.fori_loop(0, seq // CONV_CHUNK, chunk, 0)


def _conformer_conv(cv, dw, dwb, lng, lnb, pw, *, l, batch, seq):
    kpad = 2 * CONV_PAD
    dw_p = jnp.pad(dw, ((0, 0), (0, kpad - CONV_K), (0, 0)))
    vec = lambda a: a.reshape(DEPTH, 1, CONV_W)
    vspec = pl.BlockSpec((None, 1, CONV_W), lambda b: (l, 0, 0))
    return pl.pallas_call(
        functools.partial(_conv_kernel, seq=seq),
        grid=(batch,),
        in_specs=[
            pl.BlockSpec((seq, 2 * CONV_W), lambda b: (b, 0)),
            pl.BlockSpec((None, kpad, CONV_W), lambda b: (l, 0, 0)),
            vspec, vspec, vspec,
            pl.BlockSpec((None, CONV_W, CONV_W), lambda b: (l, 0, 0)),
        ],
        out_specs=pl.BlockSpec((seq, CONV_W), lambda b: (b, 0)),
        out_shape=jax.ShapeDtypeStruct((batch * seq, CONV_W), F32),
        scratch_shapes=[pltpu.VMEM((seq + 2 * CONV_PAD, CONV_W), F32)],
        compiler_params=_params(("arbitrary",)),
        name="conformer_conv",
    )(cv, dw_p, vec(dwb), vec(lng), vec(lnb), pw)


def _short_conv_kernel(x_ref, w_ref, b_ref, o32_ref, o16_ref, pad_ref, *, seq):
    c = x_ref.shape[-1]
    zeros = jnp.zeros((8, c), F32)
    pad_ref[0:8, :] = zeros
    pad_ref[8:8 + seq, :] = x_ref[...]
    pad_ref[8 + seq:16 + seq, :] = zeros
    w = w_ref[...]
    y = (pad_ref[7:7 + seq, :] * w[0:1] + pad_ref[8:8 + seq, :] * w[1:2]
         + pad_ref[9:9 + seq, :] * w[2:3] + b_ref[...])
    o32_ref[...] = y
    o16_ref[...] = y.astype(BF16)


def _short_conv(hy, sw, sb, *, l, batch, seq):
    cw = HYENA_W
    nch = hy.shape[1] // cw
    sw_p = jnp.pad(sw, ((0, 0), (0, 8 - sw.shape[1]), (0, 0)))
    io_spec = pl.BlockSpec((seq, cw), lambda b, j: (b, j))
    return pl.pallas_call(
        functools.partial(_short_conv_kernel, seq=seq),
        grid=(batch, nch),
        in_specs=[
            io_spec,
            pl.BlockSpec((None, 8, cw), lambda b, j: (l, 0, j)),
            pl.BlockSpec((None, 1, cw), lambda b, j: (l, 0, j)),
        ],
        out_specs=[io_spec, io_spec],
        out_shape=[jax.ShapeDtypeStruct(hy.shape, F32), jax.ShapeDtypeStruct(hy.shape, BF16)],
        scratch_shapes=[pltpu.VMEM((seq + 16, cw), F32)],
        compiler_params=_params(("arbitrary", "arbitrary")),
        name="hyena_short_conv",
    )(hy, sw_p, sb.reshape(DEPTH, 1, nch * cw))


def _filter_kernel(z_ref, w1_ref, b1_ref, f1_ref, w2_ref, b2_ref, f2_ref, w3_ref, ld_ref, o_ref):
    z = z_ref[...]
    h = jnp.sin(f1_ref[...] * (_dot_hi(z, w1_ref[...]) + b1_ref[...]))
    h = jnp.sin(f2_ref[...] * (_dot_hi(h, w2_ref[...]) + b2_ref[...]))
    h = _dot_hi(h, w3_ref[...])
    tn = z[:, 0:1]
    h = h * jnp.exp(-tn * jnp.exp(ld_ref[...]))
    ss = jnp.sum(h * h, axis=0, keepdims=True)
    c = HYENA_W
    scale = []
    for o in range(2):
        tot = ss[:, 2 * o * c:(2 * o + 1) * c] + ss[:, (2 * o + 1) * c:(2 * o + 2) * c]
        r = lax.rsqrt(tot + EPS)
        scale += [r, r]
    h = h * jnp.concatenate(scale, axis=1)
    row = lax.broadcasted_iota(jnp.int32, h.shape, 0)
    col = lax.broadcasted_iota(jnp.int32, h.shape, 1)
    o_ref[...] = jnp.where((row == 0) & ((col // c) % 2 == 1), 0.0, h)


def _hyena_features(seq):
    t = jnp.arange(seq, dtype=F32)
    tn = t / (seq - 1)
    bands = jnp.linspace(1e-4, HYENA_BANDS - 1, HYENA_BANDS, dtype=F32)
    ang = 2.0 * math.pi * t[:, None] * bands[None, :] / seq
    z = jnp.concatenate([tn[:, None], jnp.cos(ang), -jnp.sin(ang)], axis=-1)
    return jnp.pad(z, ((0, 0), (0, LANE - HYENA_EMB)))


def _hyena_filters(z, w1, b1, f1, w2, b2, f2, w3, log_decay):
    seq = z.shape[0]
    n = w3.shape[1]
    return pl.pallas_call(
        _filter_kernel,
        out_shape=jax.ShapeDtypeStruct((seq, n), F32),
        compiler_params=_params(None, VMEM_LIMIT),
        name="hyena_filters",
    )(z, w1, b1, f1, w2, b2, f2, w3, log_decay)


def _spectrum_kernel(m_ref, h_ref, a_ref, b_ref, d_ref, *, n_fft):
    j = pl.program_id(0)
    fc = m_ref.shape[1]
    c = HYENA_W
    h16 = h_ref[...].astype(BF16)
    gc = _dot(m_ref[0], h16)
    gs = _dot(m_ref[1], h16)
    row = j * fc + lax.broadcasted_iota(jnp.int32, (fc, c), 0)
    is0 = row == 0
    wgt = jnp.where(is0, 1.0 / n_fft, 2.0 / n_fft)
    for o in range(2):
        f = slice(2 * o * c, (2 * o + 1) * c)
        b = slice((2 * o + 1) * c, (2 * o + 2) * c)
        re = gc[:, f] + gc[:, b]
        a_ref[o] = re * wgt
        b_ref[o] = jnp.where(is0, 0.0, gs[:, f] - gs[:, b]) * wgt
        d_ref[o] = jnp.where(is0, gs[:, f] + gs[:, b], re) * wgt


def _filter_spectra(dft, hcat, *, fc):
    _, nf, seq = dft.shape
    c = HYENA_W
    out_spec = pl.BlockSpec((2, fc, c), lambda j: (0, j, 0))
    shape = jax.ShapeDtypeStruct((2, nf, c), F32)
    return pl.pallas_call(
        functools.partial(_spectrum_kernel, n_fft=2 * seq),
        grid=(nf // fc,),
        in_specs=[
            pl.BlockSpec((2, fc, seq), lambda j: (0, j, 0)),
            pl.BlockSpec(hcat.shape, lambda j: (0, 0)),
        ],
        out_specs=[out_spec, out_spec, out_spec],
        out_shape=[shape, shape, shape],
        compiler_params=_params(("arbitrary",), VMEM_LIMIT),
        name="hyena_spectra",
    )(dft, hcat)


def _longconv_kernel(u_ref, m_ref, mt_ref, a_ref, b_ref, d_ref, o_ref):
    j = pl.program_id(1)

    @pl.when(j == 0)
    def _():
        o_ref[...] = jnp.zeros(o_ref.shape, F32)

    a = a_ref[...]
    b = b_ref[...]
    d = d_ref[...]
    for bi in range(u_ref.shape[0]):
        u = u_ref[bi]
        ur = _dot(m_ref[0], u)
        ui = _dot(m_ref[1], u)
        yr = (ur * a - ui * b).astype(BF16)
        yi = (ur * b + ui * d).astype(BF16)
        o_ref[bi] += _dot(mt_ref[0], yr) + _dot(mt_ref[1], yi)


def _longconv(u16, dft, dft_t, spectra, *, order, batch, seq, bg, fc):
    c = HYENA_W
    nf = dft.shape[1]
    coef_spec = pl.BlockSpec((None, fc, c), lambda g, j: (order, j, 0))
    return pl.pallas_call(
        _longconv_kernel,
        grid=(batch // bg, nf // fc),
        in_specs=[
            pl.BlockSpec((bg, seq, c), lambda g, j: (g, 0, 0)),
            pl.BlockSpec((2, fc, seq), lambda g, j: (0, j, 0)),
            pl.BlockSpec((2, seq, fc), lambda g, j: (0, 0, j)),
            coef_spec, coef_spec, coef_spec,
        ],
        out_specs=pl.BlockSpec((bg, seq, c), lambda g, j: (g, 0, 0)),
        out_shape=jax.ShapeDtypeStruct((batch, seq, c), F32),
        compiler_params=_params(("arbitrary", "arbitrary"), VMEM_LIMIT),
        name="hyena_longconv",
    )(u16, dft, dft_t, *spectra)


def _gate_kernel(c_ref, v_ref, x_ref, bias_ref, o32_ref, o16_ref):
    v = v_ref[...]
    y = x_ref[...] * (c_ref[...] + bias_ref[0:1] * v)
    o32_ref[...] = y
    o16_ref[...] = y.astype(BF16)


def _hyena_gate(conv1, u32, bias, *, l):
    t, c = conv1.shape
    tm = TOKEN_TILE
    spec = pl.BlockSpec((tm, c), lambda i: (i, 0))
    return pl.pallas_call(
        _gate_kernel,
        grid=(t // tm,),
        in_specs=[spec, spec, pl.BlockSpec((tm, c), lambda i: (i, 1)),
                  pl.BlockSpec((None, 2, c), lambda i: (l, 0, 0))],
        out_specs=[spec, spec],
        out_shape=[jax.ShapeDtypeStruct((t, c), F32), jax.ShapeDtypeStruct((t, c), BF16)],
        compiler_params=_params(("arbitrary",)),
        name="hyena_gate",
    )(conv1, u32, u32, bias)


def _dft_matrix(seq):
    n = 2 * seq
    t = np.arange(seq, dtype=np.int64)
    nyq = np.where(t % 2 == 0, 1.0, -1.0).astype(np.float32)
    if seq <= 256:
        k = (t[:, None] * t[None, :]) % n
        ang = 2.0 * np.pi * k.astype(np.float64) / n
        m = np.stack([np.cos(ang), -np.sin(ang)]).astype(np.float32)
        m[1, 0] = nyq
        return jnp.asarray(m)
    f0n = 64
    f1n = seq // f0n
    ang1 = 2.0 * np.pi * ((np.arange(f1n)[:, None] * f0n * t[None, :]) % n).astype(np.float64) / n
    ang0 = 2.0 * np.pi * ((np.arange(f0n)[:, None] * t[None, :]) % n).astype(np.float64) / n
    c1 = jnp.asarray(np.cos(ang1), F32)[:, None, :]
    s1 = jnp.asarray(np.sin(ang1), F32)[:, None, :]
    c0 = jnp.asarray(np.cos(ang0), F32)[None, :, :]
    s0 = jnp.asarray(np.sin(ang0), F32)[None, :, :]
    mc = (c1 * c0 - s1 * s0).reshape(seq, seq)
    ms = (-(s1 * c0 + c1 * s0)).reshape(seq, seq)
    row = lax.broadcasted_iota(jnp.int32, (seq, seq), 0)
    ms = jnp.where(row == 0, jnp.asarray(nyq)[None, :], ms)
    return jnp.stack([mc, ms])


def _mixffn_kernel(x_ref, att_ref, cvo_ref, c2_ref, y_ref, x2_ref, hb_ref, mod_ref, wo_ref, g_ref,
                   wg_ref, wu_ref, wd_ref, gf_ref, o_ref, a_ref, *, final):
    m = mod_ref[...]
    y1 = y_ref[...]
    hyo = x2_ref[...] * (c2_ref[...] + hb_ref[1:2] * y1)
    o1 = ATTN_W
    o2 = ATTN_W + CONV_W
    mix = (_dot(att_ref[...].astype(BF16), wo_ref[0:o1, :])
           + _dot(cvo_ref[...].astype(BF16), wo_ref[o1:o2, :])
           + _dot(hyo.astype(BF16), wo_ref[o2:, :]))
    x = x_ref[...] + m[5:6] * mix
    h = _mod_norm(x, g_ref[...], m[7:8], m[6:7])
    y = _swiglu_into(h.astype(BF16), wg_ref, wu_ref, wd_ref, a_ref)
    x = x + (0.5 * m[8:9]) * y
    if final:
        x = x * lax.rsqrt(jnp.mean(x * x, axis=-1, keepdims=True) + EPS) * gf_ref[...]
    o_ref[...] = x


def _mixffn(x, att, cvo, c2, y1, u32, hy_bias, mod4, w_out, gain, wg, wu, wd, g_final, *,
            l, row0, rows_per_cond, final):
    t, d = x.shape
    tm = TOKEN_TILE
    c = HYENA_W
    tok = lambda w: pl.BlockSpec((tm, w), lambda i: (i, 0))
    wspec_in = _resident((None, d, D_FF), lambda i: (l, 0, 0))
    return pl.pallas_call(
        functools.partial(_mixffn_kernel, final=final),
        grid=(t // tm,),
        in_specs=[
            tok(d), tok(ATTN_W), tok(CONV_W), tok(c), tok(c),
            pl.BlockSpec((tm, c), lambda i: (i, 2)),
            pl.BlockSpec((None, 2, c), lambda i: (l, 0, 0)),
            _mod_spec(l, row0, rows_per_cond, tm),
            _resident((None, d, d), lambda i: (l, 0, 0)),
            pl.BlockSpec((None, 1, d), lambda i: (l, 0, 0)),
            wspec_in, wspec_in,
            _resident((None, D_FF, d), lambda i: (l, 0, 0)),
            pl.BlockSpec((1, d), lambda i: (0, 0)),
        ],
        out_specs=tok(d),
        out_shape=jax.ShapeDtypeStruct((t, d), F32),
        scratch_shapes=[pltpu.VMEM((tm, D_FF), BF16)],
        compiler_params=_params(("arbitrary",), VMEM_LIMIT),
        name="mixffn",
    )(x, att, cvo, c2, y1, u32, hy_bias, mod4, w_out, gain.reshape(DEPTH, 1, d), wg, wu, wd,
      g_final.reshape(1, d))


def _pad_cols(a, width):
    return jnp.pad(a, [(0, 0)] * (a.ndim - 1) + [(0, width - a.shape[-1])])


def _trunk(x, cache, mod4, P, *, batch, seq, row0, rows_per_cond):
    f0n = min(seq, 512)
    bg = 8 if seq <= 256 else 4
    dft = _dft_matrix(seq).astype(BF16)
    dft_t = jnp.swapaxes(dft, 1, 2)
    z = _hyena_features(seq)
    if cache is not None:
        cos_t, sin_t = _rope_tables(seq)
    cond = dict(row0=row0, rows_per_cond=rows_per_cond)
    ks, vs = [], []
    for l in range(DEPTH):
        x = _ffn(x, mod4, P['g_ffn1'], P['w1_gate'], P['w1_up'], P['w1_down'], l=l, mi=0, **cond)
        q, k, v, cv, hy = _inproj(x, mod4, P['g_mix'], P['w_in'], l=l, **cond)
        if cache is None:
            att = _ctx_attention(q, k, v, P['attn_sink'], l=l, batch=batch, seq=seq)
            ks.append(k)
            vs.append(v)
        else:
            att = _lat_attention(q, k, v, cache[0], cache[1], cos_t, sin_t, P['attn_sink'],
                                 l=l, batch=batch, seq=seq)
        cvo = _conformer_conv(cv, P['conv_dw'], P['conv_dw_b'], P['conv_ln_g'], P['conv_ln_b'],
                              P['conv_pw'], l=l, batch=batch, seq=seq)
        u32, u16 = _short_conv(hy, P['hy_short_w'], P['hy_short_b'], l=l, batch=batch, seq=seq)
        hcat = _hyena_filters(z, P['hy_w1'][l], P['hy_b1'][l], P['hy_f1'][l], P['hy_w2'][l],
                              P['hy_b2'][l], P['hy_f2'][l], P['hy_w3'][l], P['hy_log_decay'][l])
        spectra = _filter_spectra(dft, hcat, fc=f0n)
        lc = functools.partial(_longconv, dft=dft, dft_t=dft_t, spectra=spectra,
                               batch=batch, seq=seq, bg=bg, fc=f0n)
        c1 = lc(u16.reshape(batch, seq, -1), order=0)
        y32, y16 = _hyena_gate(c1.reshape(batch * seq, -1), u32, P['hy_bias'], l=l)
        c2 = lc(y16.reshape(batch, seq, -1), order=1).reshape(batch * seq, -1)
        x = _mixffn(x, att, cvo, c2, y32, u32, P['hy_bias'], mod4, P['w_out'], P['g_ffn2'],
                    P['w2_gate'], P['w2_up'], P['w2_down'], P['g_final'], l=l, final=(l == DEPTH - 1),
                    **cond)
    return x, ks, vs


def kernel(x_prompt, x_sample, c, cache_k, cache_v, c_ctx, w_mod, b_mod, g_ffn1, g_mix, g_ffn2,
           g_final, w1_gate, w1_up, w1_down, w2_gate, w2_up, w2_down, w_in, w_out, attn_sink,
           conv_dw, conv_dw_b, conv_ln_g, conv_ln_b, conv_pw, hy_short_w, hy_short_b,
           hy_w1, hy_b1, hy_f1, hy_w2, hy_b2, hy_f2, hy_w3, hy_log_decay, hy_bias):
    batch, seq, d = x_prompt.shape
    dec_batch, dec_seq, _ = x_sample.shape
    assert 1 + dec_batch <= COND_ROWS

    hid = LANE
    P = dict(
        g_ffn1=g_ffn1, g_mix=g_mix, g_ffn2=g_ffn2, g_final=g_final, attn_sink=attn_sink,
        w1_gate=w1_gate.astype(BF16), w1_up=w1_up.astype(BF16), w1_down=w1_down.astype(BF16),
        w2_gate=w2_gate.astype(BF16), w2_up=w2_up.astype(BF16), w2_down=w2_down.astype(BF16),
        w_in=w_in.astype(BF16), w_out=w_out.astype(BF16),
        conv_dw=conv_dw, conv_dw_b=conv_dw_b, conv_ln_g=conv_ln_g, conv_ln_b=conv_ln_b, conv_pw=conv_pw,
        hy_short_w=hy_short_w, hy_short_b=hy_short_b,
        hy_w1=jnp.pad(hy_w1, ((0, 0), (0, hid - HYENA_EMB), (0, hid - HYENA_HID))),
        hy_b1=_pad_cols(hy_b1, hid)[:, None, :], hy_f1=_pad_cols(hy_f1, hid)[:, None, :],
        hy_w2=jnp.pad(hy_w2, ((0, 0), (0, hid - HYENA_HID), (0, hid - HYENA_HID))),
        hy_b2=_pad_cols(hy_b2, hid)[:, None, :], hy_f2=_pad_cols(hy_f2, hid)[:, None, :],
        hy_w3=jnp.pad(hy_w3, ((0, 0), (0, hid - HYENA_HID), (0, 0))),
        hy_log_decay=hy_log_decay[:, None, :], hy_bias=hy_bias,
    )

    conds = jnp.concatenate([c_ctx[None, :], c, jnp.zeros((COND_ROWS - 1 - dec_batch, d), F32)], axis=0)
    mod = _modulation(conds, w_mod, b_mod)
    mod4 = mod.reshape(DEPTH, COND_ROWS, N_MOD, d)

    xp, ks, vs = _trunk(x_prompt.reshape(batch * seq, d), None, mod4, P,
                        batch=batch, seq=seq, row0=0, rows_per_cond=batch * seq)
    cache = (cache_k.reshape(*cache_k.shape[:3], KV_W), cache_v.reshape(*cache_v.shape[:3], KV_W))
    xs, _, _ = _trunk(x_sample.reshape(dec_batch * dec_seq, d), cache, mod4, P,
                      batch=dec_batch, seq=dec_seq, row0=1, rows_per_cond=dec_seq)

    kv_shape = (batch, DEPTH, seq, KV_HEADS, HEAD_DIM)
    new_k = jnp.stack([k.reshape(batch, seq, KV_W) for k in ks], axis=1).reshape(kv_shape)
    new_v = jnp.stack([v.reshape(batch, seq, KV_W) for v in vs], axis=1).reshape(kv_shape)
    return (xp.reshape(batch, seq, d), xs.reshape(dec_batch, dec_seq, d), new_k, new_v)
```

```python
import functools
import math

import numpy as np
import jax
import jax.numpy as jnp
from jax import lax
from jax.experimental import pallas as pl
from jax.experimental.pallas import tpu as pltpu

F32 = jnp.float32
BF16 = jnp.bfloat16

D_MODEL = 1024
DEPTH = 2
GRID_W = 64
N_HEADS = 8
KV_HEADS = 2
HEAD_DIM = 64
Q_PER_KV = N_HEADS // KV_HEADS
ATTN_W = N_HEADS * HEAD_DIM
KV_W = KV_HEADS * HEAD_DIM
CONV_W = D_MODEL // 4
HYENA_W = D_MODEL // 4
WINDOW = 128
BLOCK = 128
CONV_K = 31
HYENA_EMB = 33
HYENA_BANDS = (HYENA_EMB - 1) // 2
HYENA_HID = 64
D_FF = 2816
N_MOD = 9
ROPE_BASE = 10000.0
EPS = 1e-6
NEG_INF = -1e30

LANE = 128
SUBLANE = 8
MXU_W = 256
COND_ROWS = 16
TOKEN_TILE = 512
CONV_CHUNK = 128
CONV_PAD = 16
VMEM_LIMIT = 56 * 1024 * 1024


def _params(sem, vmem=None):
    return pltpu.CompilerParams(dimension_semantics=sem, vmem_limit_bytes=vmem)


def _silu(x):
    return x * jax.nn.sigmoid(x)


def _dot(a, b):
    return jnp.dot(a, b, preferred_element_type=F32)


def _dot_nt(a, b):
    return lax.dot_general(a, b, (((1,), (1,)), ((), ())), preferred_element_type=F32)


def _dot_hi(a, b):
    return jnp.dot(a, b, preferred_element_type=F32, precision=lax.Precision.HIGHEST)


def _mod_norm(x, gain, scale, shift):
    y = x * lax.rsqrt(jnp.mean(x * x, axis=-1, keepdims=True) + EPS)
    return (y * gain) * (1.0 + scale) + shift


def _mod_kernel(c_ref, w_ref, b_ref, o_ref):
    s = _silu(c_ref[...]).astype(BF16)
    o_ref[...] = _dot(s, w_ref[...].astype(BF16)) + b_ref[...]


def _modulation(conds, w_mod, b_mod):
    d = D_MODEL
    return pl.pallas_call(
        _mod_kernel,
        grid=(DEPTH, N_MOD),
        in_specs=[
            pl.BlockSpec((COND_ROWS, d), lambda l, j: (0, 0)),
            pl.BlockSpec((None, d, d), lambda l, j: (l, 0, j)),
            pl.BlockSpec((None, 1, d), lambda l, j: (l, 0, j)),
        ],
        out_specs=pl.BlockSpec((None, COND_ROWS, d), lambda l, j: (l, 0, j)),
        out_shape=jax.ShapeDtypeStruct((DEPTH, COND_ROWS, N_MOD * d), F32),
        compiler_params=_params(("arbitrary", "arbitrary")),
        name="modulation",
    )(conds, w_mod, b_mod.reshape(DEPTH, 1, N_MOD * d))


def _swiglu_into(h16, wg_ref, wu_ref, wd_ref, a_ref):
    for c in range(D_FF // MXU_W):
        sl = slice(c * MXU_W, (c + 1) * MXU_W)
        g = _dot(h16, wg_ref[:, sl])
        u = _dot(h16, wu_ref[:, sl])
        a_ref[:, sl] = (_silu(g) * u).astype(BF16)
    return _dot(a_ref[...], wd_ref[...])


def _ffn_kernel(x_ref, mod_ref, g_ref, wg_ref, wu_ref, wd_ref, o_ref, a_ref, *, mi):
    x = x_ref[...]
    m = mod_ref[...]
    h = _mod_norm(x, g_ref[...], m[mi + 1:mi + 2], m[mi:mi + 1])
    y = _swiglu_into(h.astype(BF16), wg_ref, wu_ref, wd_ref, a_ref)
    o_ref[...] = x + (0.5 * m[mi + 2:mi + 3]) * y


def _resident(shape, index_map):
    return pl.BlockSpec(shape, index_map, pipeline_mode=pl.Buffered(1))


def _mod_spec(l, row0, rows_per_cond, tm):
    return pl.BlockSpec((None, None, N_MOD, D_MODEL),
                        lambda i: (l, row0 + (i * tm) // rows_per_cond, 0, 0))


def _ffn(x, mod4, gain, wg, wu, wd, *, l, mi, row0, rows_per_cond):
    t, d = x.shape
    tm = TOKEN_TILE
    wspec_in = _resident((None, d, D_FF), lambda i: (l, 0, 0))
    return pl.pallas_call(
        functools.partial(_ffn_kernel, mi=mi),
        grid=(t // tm,),
        in_specs=[
            pl.BlockSpec((tm, d), lambda i: (i, 0)),
            _mod_spec(l, row0, rows_per_cond, tm),
            pl.BlockSpec((None, 1, d), lambda i: (l, 0, 0)),
            wspec_in, wspec_in,
            _resident((None, D_FF, d), lambda i: (l, 0, 0)),
        ],
        out_specs=pl.BlockSpec((tm, d), lambda i: (i, 0)),
        out_shape=jax.ShapeDtypeStruct((t, d), F32),
        scratch_shapes=[pltpu.VMEM((tm, D_FF), BF16)],
        compiler_params=_params(("arbitrary",), VMEM_LIMIT),
        name="ffn",
    )(x, mod4, gain.reshape(DEPTH, 1, d), wg, wu, wd)


_IN_SPLITS = (ATTN_W, KV_W, KV_W, 2 * CONV_W, 3 * HYENA_W)
_IN_SPLITS_LAT = (ATTN_W, 2 * KV_W, 2 * KV_W, 2 * CONV_W, 3 * HYENA_W)


def _rope(x, cos, sin_signed):
    w = x.shape[-1]
    lane = lax.broadcasted_iota(jnp.int32, x.shape, 1)
    quarter = HEAD_DIM // 4
    partner = jnp.where(lane % (2 * quarter) < quarter,
                        pltpu.roll(x, w - quarter, 1), pltpu.roll(x, quarter, 1))
    return x * cos + partner * sin_signed


def _inproj_kernel(x_ref, mod_ref, g_ref, w_ref, q_ref, k_ref, v_ref, cv_ref, hy_ref):
    m = mod_ref[...]
    h = _mod_norm(x_ref[...], g_ref[...], m[4:5], m[3:4]).astype(BF16)
    off = 0
    for ref, width in zip((q_ref, k_ref, v_ref, cv_ref, hy_ref), _IN_SPLITS):
        ref[...] = _dot(h, w_ref[:, off:off + width])
        off += width


def _inproj_lat_kernel(x_ref, mod_ref, g_ref, w_ref, cos_ref, sin_ref, q_ref, k_ref, v_ref, cv_ref, hy_ref):
    m = mod_ref[...]
    h = _mod_norm(x_ref[...], g_ref[...], m[4:5], m[3:4]).astype(BF16)
    cos = cos_ref[...]
    sin = sin_ref[...]
    wq, wk, wv, wc, wh = _IN_SPLITS_LAT
    tile = lambda a, width: jnp.concatenate([a] * (width // a.shape[1]), axis=1)
    q = _rope(_dot(h, w_ref[:, 0:wq]), tile(cos, wq), tile(sin, wq))
    q_ref[...] = (q * (HEAD_DIM ** -0.5)).astype(BF16)
    off = wq
    k_ref[...] = _rope(_dot(h, w_ref[:, off:off + wk]), tile(cos, wk), tile(sin, wk)).astype(BF16)
    off += wk
    v_ref[...] = _dot(h, w_ref[:, off:off + wv]).astype(BF16)
    off += wv
    cv_ref[...] = _dot(h, w_ref[:, off:off + wc])
    off += wc
    hy_ref[...] = _dot(h, w_ref[:, off:off + wh])


def _inproj(x, mod4, gain, w_in, rope_tables, *, l, row0, rows_per_cond, seq):
    t, d = x.shape
    tm = TOKEN_TILE
    lat = rope_tables is not None
    splits = _IN_SPLITS_LAT if lat else _IN_SPLITS
    dtypes = (BF16, BF16, BF16, F32, F32) if lat else (F32,) * 5
    in_specs = [
        pl.BlockSpec((tm, d), lambda i: (i, 0)),
        _mod_spec(l, row0, rows_per_cond, tm),
        pl.BlockSpec((None, 1, d), lambda i: (l, 0, 0)),
        _resident((None, d, sum(splits)), lambda i: (l, 0, 0)),
    ]
    args = [x, mod4, gain.reshape(DEPTH, 1, d), w_in]
    if lat:
        table_spec = pl.BlockSpec((tm, LANE), lambda i: (i % (seq // tm), 0))
        in_specs += [table_spec, table_spec]
        args += list(rope_tables)
    return pl.pallas_call(
        _inproj_lat_kernel if lat else _inproj_kernel,
        grid=(t // tm,),
        in_specs=in_specs,
        out_specs=[pl.BlockSpec((tm, w), lambda i: (i, 0)) for w in splits],
        out_shape=[jax.ShapeDtypeStruct((t, w), dt) for w, dt in zip(splits, dtypes)],
        compiler_params=_params(("arbitrary",), VMEM_LIMIT),
        name="inproj",
    )(*args)


def _ctx_attn_kernel(sink_ref, q_ref, k_ref, v_ref, o_ref, *, l):
    q = q_ref[...] * (HEAD_DIM ** -0.5)
    k = k_ref[...].astype(BF16)
    v = v_ref[...].astype(BF16)
    for h in range(N_HEADS):
        g = h // Q_PER_KV
        gs = slice(g * HEAD_DIM, (g + 1) * HEAD_DIM)
        hs = slice(h * HEAD_DIM, (h + 1) * HEAD_DIM)
        s = _dot_nt(q[:, hs].astype(BF16), k[:, gs])
        sk = sink_ref[l, h]
        mx = jnp.maximum(jnp.max(s, axis=-1, keepdims=True), sk)
        p = jnp.exp(s - mx)
        den = jnp.sum(p, axis=-1, keepdims=True) + jnp.exp(sk - mx)
        o_ref[:, hs] = (_dot(p.astype(BF16), v[:, gs]) / den).astype(BF16)


def _ctx_attention(q, k, v, sink, *, l, batch, seq):
    return pl.pallas_call(
        functools.partial(_ctx_attn_kernel, l=l),
        grid=(batch,),
        in_specs=[
            pl.BlockSpec(memory_space=pltpu.SMEM),
            pl.BlockSpec((seq, ATTN_W), lambda b: (b, 0)),
            pl.BlockSpec((seq, KV_W), lambda b: (b, 0)),
            pl.BlockSpec((seq, KV_W), lambda b: (b, 0)),
        ],
        out_specs=pl.BlockSpec((seq, ATTN_W), lambda b: (b, 0)),
        out_shape=jax.ShapeDtypeStruct((batch * seq, ATTN_W), BF16),
        compiler_params=_params(("arbitrary",)),
        name="ctx_attention",
    )(sink, q, k, v)


def _dup_kv_heads(x):
    swapped = pltpu.roll(x, HEAD_DIM, 1)
    low = lax.broadcasted_iota(jnp.int32, x.shape, 1) < HEAD_DIM
    return jnp.concatenate([jnp.where(low, x, swapped), jnp.where(low, swapped, x)], axis=1).astype(BF16)


def _lat_attn_kernel(sink_ref, q_ref, k_ref, v_ref, kc_ref, vc_ref, o_ref, kcd_ref, vcd_ref, *, l, seq):
    i = pl.program_id(1)

    @pl.when(i == 0)
    def _():
        kcd_ref[...] = _dup_kv_heads(kc_ref[...])
        vcd_ref[...] = _dup_kv_heads(vc_ref[...])

    nloc = 3 * BLOCK
    start = pl.multiple_of(jnp.clip((i - 1) * BLOCK, 0, seq - nloc), BLOCK)
    q0 = i * BLOCK
    jpos = start + lax.broadcasted_iota(jnp.int32, (2 * BLOCK, nloc), 1)
    row = lax.broadcasted_iota(jnp.int32, (2 * BLOCK, nloc), 0)
    valid = jnp.abs(jpos - (q0 + row % BLOCK)) <= WINDOW
    top = lax.broadcasted_iota(jnp.int32, (2 * BLOCK, 1), 0) < BLOCK
    low = lax.broadcasted_iota(jnp.int32, (2 * BLOCK, LANE), 1) < HEAD_DIM
    q = q_ref[...]
    zero = jnp.zeros((2 * BLOCK, LANE), BF16)

    for g in range(KV_HEADS):
        gl = slice(g * LANE, (g + 1) * LANE)
        kd = jnp.concatenate([k_ref[pl.ds(start, nloc), gl], kcd_ref[:, gl]], axis=0)
        vd = jnp.concatenate([v_ref[pl.ds(start, nloc), gl], vcd_ref[:, gl]], axis=0)
        c0 = 2 * g * LANE
        lhs = jnp.concatenate([q[:, c0:c0 + LANE], q[:, c0 + LANE:c0 + 2 * LANE]], axis=0)
        outs = []
        for half in range(2):
            s = _dot_nt(jnp.where(low, lhs, zero) if half == 0 else jnp.where(low, zero, lhs), kd)
            s_loc = jnp.where(valid, s[:, :nloc], NEG_INF)
            s_ctx = s[:, nloc:]
            sk = jnp.where(top, sink_ref[l, 4 * g + half], sink_ref[l, 4 * g + 2 + half])
            mx = jnp.maximum(jnp.maximum(jnp.max(s_loc, axis=-1, keepdims=True),
                                         jnp.max(s_ctx, axis=-1, keepdims=True)), sk)
            p_loc = jnp.exp(s_loc - mx)
            p_ctx = jnp.exp(s_ctx - mx)
            den = (jnp.sum(p_loc, axis=-1, keepdims=True) + jnp.sum(p_ctx, axis=-1, keepdims=True)
                   + jnp.exp(sk - mx))
            p = jnp.concatenate([p_loc.astype(BF16), p_ctx.astype(BF16)], axis=1)
            outs.append(_dot(p, vd) / den)
        o = jnp.where(low, outs[0], outs[1]).astype(BF16)
        o_ref[:, c0:c0 + LANE] = o[:BLOCK]
        o_ref[:, c0 + LANE:c0 + 2 * LANE] = o[BLOCK:]


def _lat_attention(q, k, v, cache_k, cache_v, sink, *, l, batch, seq):
    nb = seq // BLOCK
    past = cache_k.shape[2]
    kv_spec = pl.BlockSpec((seq, 2 * KV_W), lambda b, i: (b, 0))
    cache_spec = pl.BlockSpec((None, None, past, KV_W), lambda b, i: (b, l, 0, 0))
    return pl.pallas_call(
        functools.partial(_lat_attn_kernel, l=l, seq=seq),
        grid=(batch, nb),
        in_specs=[
            pl.BlockSpec(memory_space=pltpu.SMEM),
            pl.BlockSpec((BLOCK, ATTN_W), lambda b, i: (b * nb + i, 0)),
            kv_spec, kv_spec, cache_spec, cache_spec,
        ],
        out_specs=pl.BlockSpec((BLOCK, ATTN_W), lambda b, i: (b * nb + i, 0)),
        out_shape=jax.ShapeDtypeStruct((batch * seq, ATTN_W), BF16),
        scratch_shapes=[pltpu.VMEM((past, 2 * KV_W), BF16), pltpu.VMEM((past, 2 * KV_W), BF16)],
        compiler_params=_params(("arbitrary", "arbitrary")),
        name="lat_attention",
    )(sink, q, k, v, cache_k, cache_v)


def _rope_tables(seq):
    rows = seq // GRID_W
    row = jnp.repeat(jnp.arange(rows), GRID_W)
    col = jnp.arange(rows * GRID_W) % GRID_W
    nf = HEAD_DIM // 4
    inv = ROPE_BASE ** (-jnp.arange(nf, dtype=F32) / nf)
    ang_r = row.astype(F32)[:, None] * inv[None, :]
    ang_c = col.astype(F32)[:, None] * inv[None, :]
    cos_h = jnp.concatenate([jnp.cos(ang_r)] * 2 + [jnp.cos(ang_c)] * 2, axis=1)
    sin_h = jnp.concatenate([-jnp.sin(ang_r), jnp.sin(ang_r), -jnp.sin(ang_c), jnp.sin(ang_c)], axis=1)
    return jnp.tile(cos_h, (1, KV_HEADS)), jnp.tile(sin_h, (1, KV_HEADS))


def _conv_kernel(cv_ref, dw_ref, dwb_ref, lng_ref, lnb_ref, pw_ref, o_ref, pad_ref, *, seq):
    x = cv_ref[...]
    zeros = jnp.zeros((CONV_PAD, CONV_W), F32)
    pad_ref[0:CONV_PAD, :] = zeros
    pad_ref[CONV_PAD:CONV_PAD + seq, :] = x[:, :CONV_W] * jax.nn.sigmoid(x[:, CONV_W:])
    pad_ref[CONV_PAD + seq:2 * CONV_PAD + seq, :] = zeros
    dw = dw_ref[...]
    pw = pw_ref[...].astype(BF16)
    first = CONV_PAD - CONV_K // 2

    def chunk(c, carry):
        base = pl.multiple_of(c * CONV_CHUNK, CONV_CHUNK)
        win = pad_ref[pl.ds(base, CONV_CHUNK + 2 * CONV_PAD), :]
        acc = jnp.zeros((CONV_CHUNK, CONV_W), F32)
        for r in range(SUBLANE):
            offs = [first + k for k in range(CONV_K) if (first + k) % SUBLANE == r]
            if not offs:
                continue
            shifted = win if r == 0 else pltpu.roll(win, win.shape[0] - r, 0)
            for off in offs:
                acc = acc + shifted[off - r:off - r + CONV_CHUNK] * dw[off - first:off - first + 1]
        acc = acc + dwb_ref[...]
        mu = jnp.mean(acc, axis=-1, keepdims=True)
        cen = acc - mu
        var = jnp.mean(cen * cen, axis=-1, keepdims=True)
        y = cen * lax.rsqrt(var + EPS) * lng_ref[...] + lnb_ref[...]
        o_ref[pl.ds(base, CONV_CHUNK), :] = _dot(_silu(y).astype(BF16), pw).astype(BF16)
        return carry

    lax.fori_loop(0, seq // CONV_CHUNK, chunk, 0)


def _conformer_conv(cv, dw, dwb, lng, lnb, pw, *, l, batch, seq):
    kpad = 2 * CONV_PAD
    dw_p = jnp.pad(dw, ((0, 0), (0, kpad - CONV_K), (0, 0)))
    vec = lambda a: a.reshape(DEPTH, 1, CONV_W)
    vspec = pl.BlockSpec((None, 1, CONV_W), lambda b: (l, 0, 0))
    return pl.pallas_call(
        functools.partial(_conv_kernel, seq=seq),
        grid=(batch,),
        in_specs=[
            pl.BlockSpec((seq, 2 * CONV_W), lambda b: (b, 0)),
            pl.BlockSpec((None, kpad, CONV_W), lambda b: (l, 0, 0)),
            vspec, vspec, vspec,
            pl.BlockSpec((None, CONV_W, CONV_W), lambda b: (l, 0, 0)),
        ],
        out_specs=pl.BlockSpec((seq, CONV_W), lambda b: (b, 0)),
        out_shape=jax.ShapeDtypeStruct((batch * seq, CONV_W), BF16),
        scratch_shapes=[pltpu.VMEM((seq + 2 * CONV_PAD, CONV_W), F32)],
        compiler_params=_params(("arbitrary",)),
        name="conformer_conv",
    )(cv, dw_p, vec(dwb), vec(lng), vec(lnb), pw)


def _short_conv_kernel(x_ref, w_ref, b_ref, o32_ref, o16_ref, pad_ref, *, seq):
    c = x_ref.shape[-1]
    zeros = jnp.zeros((8, c), F32)
    pad_ref[0:8, :] = zeros
    pad_ref[8:8 + seq, :] = x_ref[...]
    pad_ref[8 + seq:16 + seq, :] = zeros
    w = w_ref[...]
    y = (pad_ref[7:7 + seq, :] * w[0:1] + pad_ref[8:8 + seq, :] * w[1:2]
         + pad_ref[9:9 + seq, :] * w[2:3] + b_ref[...])
    o32_ref[...] = y
    o16_ref[...] = y.astype(BF16)


def _short_conv(hy, sw, sb, *, l, batch, seq):
    cw = HYENA_W
    nch = hy.shape[1] // cw
    sw_p = jnp.pad(sw, ((0, 0), (0, 8 - sw.shape[1]), (0, 0)))
    io_spec = pl.BlockSpec((seq, cw), lambda b, j: (b, j))
    return pl.pallas_call(
        functools.partial(_short_conv_kernel, seq=seq),
        grid=(batch, nch),
        in_specs=[
            io_spec,
            pl.BlockSpec((None, 8, cw), lambda b, j: (l, 0, j)),
            pl.BlockSpec((None, 1, cw), lambda b, j: (l, 0, j)),
        ],
        out_specs=[io_spec, io_spec],
        out_shape=[jax.ShapeDtypeStruct(hy.shape, F32), jax.ShapeDtypeStruct(hy.shape, BF16)],
        scratch_shapes=[pltpu.VMEM((seq + 16, cw), F32)],
        compiler_params=_params(("arbitrary", "arbitrary")),
        name="hyena_short_conv",
    )(hy, sw_p, sb.reshape(DEPTH, 1, nch * cw))


def _filter_kernel(z_ref, w1_ref, b1_ref, f1_ref, w2_ref, b2_ref, f2_ref, w3_ref, ld_ref, o_ref):
    z = z_ref[...]
    h = jnp.sin(f1_ref[...] * (_dot_hi(z, w1_ref[...]) + b1_ref[...]))
    h = jnp.sin(f2_ref[...] * (_dot_hi(h, w2_ref[...]) + b2_ref[...]))
    h = _dot_hi(h, w3_ref[...])
    tn = z[:, 0:1]
    h = h * jnp.exp(-tn * jnp.exp(ld_ref[...]))
    ss = jnp.sum(h * h, axis=0, keepdims=True)
    c = HYENA_W
    scale = []
    for o in range(2):
        tot = ss[:, 2 * o * c:(2 * o + 1) * c] + ss[:, (2 * o + 1) * c:(2 * o + 2) * c]
        r = lax.rsqrt(tot + EPS)
        scale += [r, r]
    h = h * jnp.concatenate(scale, axis=1)
    row = lax.broadcasted_iota(jnp.int32, h.shape, 0)
    col = lax.broadcasted_iota(jnp.int32, h.shape, 1)
    o_ref[...] = jnp.where((row == 0) & ((col // c) % 2 == 1), 0.0, h)


def _hyena_features(seq):
    t = jnp.arange(seq, dtype=F32)
    tn = t / (seq - 1)
    bands = jnp.linspace(1e-4, HYENA_BANDS - 1, HYENA_BANDS, dtype=F32)
    ang = 2.0 * math.pi * t[:, None] * bands[None, :] / seq
    z = jnp.concatenate([tn[:, None], jnp.cos(ang), -jnp.sin(ang)], axis=-1)
    return jnp.pad(z, ((0, 0), (0, LANE - HYENA_EMB)))


def _hyena_filters(z, w1, b1, f1, w2, b2, f2, w3, log_decay):
    seq = z.shape[0]
    n = w3.shape[1]
    return pl.pallas_call(
        _filter_kernel,
        out_shape=jax.ShapeDtypeStruct((seq, n), F32),
        compiler_params=_params(None, VMEM_LIMIT),
        name="hyena_filters",
    )(z, w1, b1, f1, w2, b2, f2, w3, log_decay)


def _spectrum_kernel(m_ref, h_ref, a_ref, b_ref, d_ref, *, n_fft):
    j = pl.program_id(0)
    fc = m_ref.shape[1]
    c = HYENA_W
    h16 = h_ref[...].astype(BF16)
    gc = _dot(m_ref[0], h16)
    gs = _dot(m_ref[1], h16)
    row = j * fc + lax.broadcasted_iota(jnp.int32, (fc, c), 0)
    is0 = row == 0
    wgt = jnp.where(is0, 1.0 / n_fft, 2.0 / n_fft)
    for o in range(2):
        f = slice(2 * o * c, (2 * o + 1) * c)
        b = slice((2 * o + 1) * c, (2 * o + 2) * c)
        re = gc[:, f] + gc[:, b]
        a_ref[o] = re * wgt
        b_ref[o] = jnp.where(is0, 0.0, gs[:, f] - gs[:, b]) * wgt
        d_ref[o] = jnp.where(is0, gs[:, f] + gs[:, b], re) * wgt


def _filter_spectra(dft, hcat, *, fc):
    _, nf, seq = dft.shape
    c = HYENA_W
    out_spec = pl.BlockSpec((2, fc, c), lambda j: (0, j, 0))
    shape = jax.ShapeDtypeStruct((2, nf, c), F32)
    return pl.pallas_call(
        functools.partial(_spectrum_kernel, n_fft=2 * seq),
        grid=(nf // fc,),
        in_specs=[
            pl.BlockSpec((2, fc, seq), lambda j: (0, j, 0)),
            pl.BlockSpec(hcat.shape, lambda j: (0, 0)),
        ],
        out_specs=[out_spec, out_spec, out_spec],
        out_shape=[shape, shape, shape],
        compiler_params=_params(("arbitrary",), VMEM_LIMIT),
        name="hyena_spectra",
    )(dft, hcat)


def _longconv_kernel(u_ref, m_ref, mt_ref, a_ref, b_ref, d_ref, o_ref):
    j = pl.program_id(1)

    @pl.when(j == 0)
    def _():
        o_ref[...] = jnp.zeros(o_ref.shape, F32)

    a = a_ref[...]
    b = b_ref[...]
    d = d_ref[...]
    for bi in range(u_ref.shape[0]):
        u = u_ref[bi]
        ur = _dot(m_ref[0], u)
        ui = _dot(m_ref[1], u)
        yr = (ur * a - ui * b).astype(BF16)
        yi = (ur * b + ui * d).astype(BF16)
        o_ref[bi] += _dot(mt_ref[0], yr) + _dot(mt_ref[1], yi)


def _longconv(u16, dft, dft_t, spectra, *, order, batch, seq, bg, fc):
    c = HYENA_W
    nf = dft.shape[1]
    coef_spec = pl.BlockSpec((None, fc, c), lambda g, j: (order, j, 0))
    return pl.pallas_call(
        _longconv_kernel,
        grid=(batch // bg, nf // fc),
        in_specs=[
            pl.BlockSpec((bg, seq, c), lambda g, j: (g, 0, 0)),
            pl.BlockSpec((2, fc, seq), lambda g, j: (0, j, 0)),
            pl.BlockSpec((2, seq, fc), lambda g, j: (0, 0, j)),
            coef_spec, coef_spec, coef_spec,
        ],
        out_specs=pl.BlockSpec((bg, seq, c), lambda g, j: (g, 0, 0)),
        out_shape=jax.ShapeDtypeStruct((batch, seq, c), F32),
        compiler_params=_params(("arbitrary", "arbitrary"), VMEM_LIMIT),
        name="hyena_longconv",
    )(u16, dft, dft_t, *spectra)


def _gate_kernel(c_ref, v_ref, x_ref, bias_ref, o32_ref, o16_ref):
    v = v_ref[...]
    y = x_ref[...] * (c_ref[...] + bias_ref[0:1] * v)
    o32_ref[...] = y
    o16_ref[...] = y.astype(BF16)


def _hyena_gate(conv1, u32, bias, *, l):
    t, c = conv1.shape
    tm = TOKEN_TILE
    spec = pl.BlockSpec((tm, c), lambda i: (i, 0))
    return pl.pallas_call(
        _gate_kernel,
        grid=(t // tm,),
        in_specs=[spec, spec, pl.BlockSpec((tm, c), lambda i: (i, 1)),
                  pl.BlockSpec((None, 2, c), lambda i: (l, 0, 0))],
        out_specs=[spec, spec],
        out_shape=[jax.ShapeDtypeStruct((t, c), F32), jax.ShapeDtypeStruct((t, c), BF16)],
        compiler_params=_params(("arbitrary",)),
        name="hyena_gate",
    )(conv1, u32, u32, bias)


def _dft_matrix(seq):
    n = 2 * seq
    t = np.arange(seq, dtype=np.int64)
    nyq = np.where(t % 2 == 0, 1.0, -1.0).astype(np.float32)
    if seq <= 256:
        k = (t[:, None] * t[None, :]) % n
        ang = 2.0 * np.pi * k.astype(np.float64) / n
        m = np.stack([np.cos(ang), -np.sin(ang)]).astype(np.float32)
        m[1, 0] = nyq
        return jnp.asarray(m)
    f0n = 64
    f1n = seq // f0n
    ang1 = 2.0 * np.pi * ((np.arange(f1n)[:, None] * f0n * t[None, :]) % n).astype(np.float64) / n
    ang0 = 2.0 * np.pi * ((np.arange(f0n)[:, None] * t[None, :]) % n).astype(np.float64) / n
    c1 = jnp.asarray(np.cos(ang1), F32)[:, None, :]
    s1 = jnp.asarray(np.sin(ang1), F32)[:, None, :]
    c0 = jnp.asarray(np.cos(ang0), F32)[None, :, :]
    s0 = jnp.asarray(np.sin(ang0), F32)[None, :, :]
    mc = (c1 * c0 - s1 * s0).reshape(seq, seq)
    ms = (-(s1 * c0 + c1 * s0)).reshape(seq, seq)
    row = lax.broadcasted_iota(jnp.int32, (seq, seq), 0)
    ms = jnp.where(row == 0, jnp.asarray(nyq)[None, :], ms)
    return jnp.stack([mc, ms])


def _mixffn_kernel(x_ref, att_ref, cvo_ref, c2_ref, y_ref, x2_ref, hb_ref, mod_ref, wo_ref, g_ref,
                   wg_ref, wu_ref, wd_ref, gf_ref, o_ref, a_ref, *, final):
    m = mod_ref[...]
    y1 = y_ref[...]
    hyo = x2_ref[...] * (c2_ref[...] + hb_ref[1:2] * y1)
    o1 = ATTN_W
    o2 = ATTN_W + CONV_W
    mix = (_dot(att_ref[...].astype(BF16), wo_ref[0:o1, :])
           + _dot(cvo_ref[...].astype(BF16), wo_ref[o1:o2, :])
           + _dot(hyo.astype(BF16), wo_ref[o2:, :]))
    x = x_ref[...] + m[5:6] * mix
    h = _mod_norm(x, g_ref[...], m[7:8], m[6:7])
    y = _swiglu_into(h.astype(BF16), wg_ref, wu_ref, wd_ref, a_ref)
    x = x + (0.5 * m[8:9]) * y
    if final:
        x = x * lax.rsqrt(jnp.mean(x * x, axis=-1, keepdims=True) + EPS) * gf_ref[...]
    o_ref[...] = x


def _mixffn(x, att, cvo, c2, y1, u32, hy_bias, mod4, w_out, gain, wg, wu, wd, g_final, *,
            l, row0, rows_per_cond, final):
    t, d = x.shape
    tm = TOKEN_TILE
    c = HYENA_W
    tok = lambda w: pl.BlockSpec((tm, w), lambda i: (i, 0))
    wspec_in = _resident((None, d, D_FF), lambda i: (l, 0, 0))
    return pl.pallas_call(
        functools.partial(_mixffn_kernel, final=final),
        grid=(t // tm,),
        in_specs=[
            tok(d), tok(ATTN_W), tok(CONV_W), tok(c), tok(c),
            pl.BlockSpec((tm, c), lambda i: (i, 2)),
            pl.BlockSpec((None, 2, c), lambda i: (l, 0, 0)),
            _mod_spec(l, row0, rows_per_cond, tm),
            _resident((None, d, d), lambda i: (l, 0, 0)),
            pl.BlockSpec((None, 1, d), lambda i: (l, 0, 0)),
            wspec_in, wspec_in,
            _resident((None, D_FF, d), lambda i: (l, 0, 0)),
            pl.BlockSpec((1, d), lambda i: (0, 0)),
        ],
        out_specs=tok(d),
        out_shape=jax.ShapeDtypeStruct((t, d), F32),
        scratch_shapes=[pltpu.VMEM((tm, D_FF), BF16)],
        compiler_params=_params(("arbitrary",), VMEM_LIMIT),
        name="mixffn",
    )(x, att, cvo, c2, y1, u32, hy_bias, mod4, w_out, gain.reshape(DEPTH, 1, d), wg, wu, wd,
      g_final.reshape(1, d))


def _pad_cols(a, width):
    return jnp.pad(a, [(0, 0)] * (a.ndim - 1) + [(0, width - a.shape[-1])])


def _trunk(x, cache, mod4, P, *, batch, seq, row0, rows_per_cond):
    f0n = min(seq, 512)
    bg = 8 if seq <= 256 else 4
    dft = _dft_matrix(seq).astype(BF16)
    dft_t = jnp.swapaxes(dft, 1, 2)
    z = _hyena_features(seq)
    rope_tables = None if cache is None else _rope_tables(seq)
    w_in = P['w_in'] if cache is None else P['w_in_lat']
    cond = dict(row0=row0, rows_per_cond=rows_per_cond)
    ks, vs = [], []
    for l in range(DEPTH):
        x = _ffn(x, mod4, P['g_ffn1'], P['w1_gate'], P['w1_up'], P['w1_down'], l=l, mi=0, **cond)
        q, k, v, cv, hy = _inproj(x, mod4, P['g_mix'], w_in, rope_tables, l=l, seq=seq, **cond)
        if cache is None:
            att = _ctx_attention(q, k, v, P['attn_sink'], l=l, batch=batch, seq=seq)
            ks.append(k)
            vs.append(v)
        else:
            att = _lat_attention(q, k, v, cache[0], cache[1], P['attn_sink'], l=l, batch=batch, seq=seq)
        cvo = _conformer_conv(cv, P['conv_dw'], P['conv_dw_b'], P['conv_ln_g'], P['conv_ln_b'],
                              P['conv_pw'], l=l, batch=batch, seq=seq)
        u32, u16 = _short_conv(hy, P['hy_short_w'], P['hy_short_b'], l=l, batch=batch, seq=seq)
        hcat = _hyena_filters(z, P['hy_w1'][l], P['hy_b1'][l], P['hy_f1'][l], P['hy_w2'][l],
                              P['hy_b2'][l], P['hy_f2'][l], P['hy_w3'][l], P['hy_log_decay'][l])
        spectra = _filter_spectra(dft, hcat, fc=f0n)
        lc = functools.partial(_longconv, dft=dft, dft_t=dft_t, spectra=spectra,
                               batch=batch, seq=seq, bg=bg, fc=f0n)
        c1 = lc(u16.reshape(batch, seq, -1), order=0)
        y32, y16 = _hyena_gate(c1.reshape(batch * seq, -1), u32, P['hy_bias'], l=l)
        c2 = lc(y16.reshape(batch, seq, -1), order=1).reshape(batch * seq, -1)
        x = _mixffn(x, att, cvo, c2, y32, u32, P['hy_bias'], mod4, P['w_out'], P['g_ffn2'],
                    P['w2_gate'], P['w2_up'], P['w2_down'], P['g_final'], l=l, final=(l == DEPTH - 1),
                    **cond)
    return x, ks, vs


def kernel(x_prompt, x_sample, c, cache_k, cache_v, c_ctx, w_mod, b_mod, g_ffn1, g_mix, g_ffn2,
           g_final, w1_gate, w1_up, w1_down, w2_gate, w2_up, w2_down, w_in, w_out, attn_sink,
           conv_dw, conv_dw_b, conv_ln_g, conv_ln_b, conv_pw, hy_short_w, hy_short_b,
           hy_w1, hy_b1, hy_f1, hy_w2, hy_b2, hy_f2, hy_w3, hy_log_decay, hy_bias):
    batch, seq, d = x_prompt.shape
    dec_batch, dec_seq, _ = x_sample.shape
    assert 1 + dec_batch <= COND_ROWS

    hid = LANE
    w_in16 = w_in.astype(BF16)
    o1, o2, o3 = ATTN_W, ATTN_W + KV_W, ATTN_W + 2 * KV_W

    def dup_heads(w):
        g0, g1 = w[..., :HEAD_DIM], w[..., HEAD_DIM:]
        return jnp.concatenate([g0, g0, g1, g1], axis=-1)

    w_in_lat = jnp.concatenate([w_in16[..., :o1], dup_heads(w_in16[..., o1:o2]),
                                dup_heads(w_in16[..., o2:o3]), w_in16[..., o3:]], axis=-1)
    P = dict(
        w_in_lat=w_in_lat,
        g_ffn1=g_ffn1, g_mix=g_mix, g_ffn2=g_ffn2, g_final=g_final, attn_sink=attn_sink,
        w1_gate=w1_gate.astype(BF16), w1_up=w1_up.astype(BF16), w1_down=w1_down.astype(BF16),
        w2_gate=w2_gate.astype(BF16), w2_up=w2_up.astype(BF16), w2_down=w2_down.astype(BF16),
        w_in=w_in16, w_out=w_out.astype(BF16),
        conv_dw=conv_dw, conv_dw_b=conv_dw_b, conv_ln_g=conv_ln_g, conv_ln_b=conv_ln_b, conv_pw=conv_pw,
        hy_short_w=hy_short_w, hy_short_b=hy_short_b,
        hy_w1=jnp.pad(hy_w1, ((0, 0), (0, hid - HYENA_EMB), (0, hid - HYENA_HID))),
        hy_b1=_pad_cols(hy_b1, hid)[:, None, :], hy_f1=_pad_cols(hy_f1, hid)[:, None, :],
        hy_w2=jnp.pad(hy_w2, ((0, 0), (0, hid - HYENA_HID), (0, hid - HYENA_HID))),
        hy_b2=_pad_cols(hy_b2, hid)[:, None, :], hy_f2=_pad_cols(hy_f2, hid)[:, None, :],
        hy_w3=jnp.pad(hy_w3, ((0, 0), (0, hid - HYENA_HID), (0, 0))),
        hy_log_decay=hy_log_decay[:, None, :], hy_bias=hy_bias,
    )

    conds = jnp.concatenate([c_ctx[None, :], c, jnp.zeros((COND_ROWS - 1 - dec_batch, d), F32)], axis=0)
    mod = _modulation(conds, w_mod, b_mod)
    mod4 = mod.reshape(DEPTH, COND_ROWS, N_MOD, d)

    xp, ks, vs = _trunk(x_prompt.reshape(batch * seq, d), None, mod4, P,
                        batch=batch, seq=seq, row0=0, rows_per_cond=batch * seq)
    cache = (cache_k.reshape(*cache_k.shape[:3], KV_W), cache_v.reshape(*cache_v.shape[:3], KV_W))
    xs, _, _ = _trunk(x_sample.reshape(dec_batch * dec_seq, d), cache, mod4, P,
                      batch=dec_batch, seq=dec_seq, row0=1, rows_per_cond=dec_seq)

    kv_shape = (batch, DEPTH, seq, KV_HEADS, HEAD_DIM)
    new_k = jnp.stack([k.reshape(batch, seq, KV_W) for k in ks], axis=1).reshape(kv_shape)
    new_v = jnp.stack([v.reshape(batch, seq, KV_W) for v in vs], axis=1).reshape(kv_shape)
    return (xp.reshape(batch, seq, d), xs.reshape(dec_batch, dec_seq, d), new_k, new_v)
```

```python
import functools
import math

import numpy as np
import jax
import jax.numpy as jnp
from jax import lax
from jax.experimental import pallas as pl
from jax.experimental.pallas import tpu as pltpu

F32 = jnp.float32
BF16 = jnp.bfloat16

D_MODEL = 1024
DEPTH = 2
GRID_W = 64
N_HEADS = 8
KV_HEADS = 2
HEAD_DIM = 64
Q_PER_KV = N_HEADS // KV_HEADS
ATTN_W = N_HEADS * HEAD_DIM
KV_W = KV_HEADS * HEAD_DIM
CONV_W = D_MODEL // 4
HYENA_W = D_MODEL // 4
WINDOW = 128
BLOCK = 128
CONV_K = 31
HYENA_EMB = 33
HYENA_BANDS = (HYENA_EMB - 1) // 2
HYENA_HID = 64
D_FF = 2816
N_MOD = 9
ROPE_BASE = 10000.0
EPS = 1e-6
NEG_INF = -1e30

LANE = 128
SUBLANE = 8
MXU_W = 256
COND_ROWS = 16
TOKEN_TILE = 512
CONV_CHUNK = 128
CONV_PAD = 16
FREQ_CHUNK = 512
LONGCONV_ROWS = 4096
VMEM_LIMIT = 56 * 1024 * 1024


def _params(sem, vmem=None):
    return pltpu.CompilerParams(dimension_semantics=sem, vmem_limit_bytes=vmem)


def _silu(x):
    return x * jax.nn.sigmoid(x)


def _dot(a, b):
    return jnp.dot(a, b, preferred_element_type=F32)


def _dot_nt(a, b):
    return lax.dot_general(a, b, (((1,), (1,)), ((), ())), preferred_element_type=F32)


def _dot_hi(a, b):
    return jnp.dot(a, b, preferred_element_type=F32, precision=lax.Precision.HIGHEST)


def _mod_norm(x, gain, scale, shift):
    y = x * lax.rsqrt(jnp.mean(x * x, axis=-1, keepdims=True) + EPS)
    return (y * gain) * (1.0 + scale) + shift


def _mod_kernel(c_ref, w_ref, b_ref, o_ref):
    s = _silu(c_ref[...]).astype(BF16)
    o_ref[...] = _dot(s, w_ref[...].astype(BF16)) + b_ref[...]


def _modulation(conds, w_mod, b_mod):
    d = D_MODEL
    return pl.pallas_call(
        _mod_kernel,
        grid=(DEPTH, N_MOD),
        in_specs=[
            pl.BlockSpec((COND_ROWS, d), lambda l, j: (0, 0)),
            pl.BlockSpec((None, d, d), lambda l, j: (l, 0, j)),
            pl.BlockSpec((None, 1, d), lambda l, j: (l, 0, j)),
        ],
        out_specs=pl.BlockSpec((None, COND_ROWS, d), lambda l, j: (l, 0, j)),
        out_shape=jax.ShapeDtypeStruct((DEPTH, COND_ROWS, N_MOD * d), F32),
        compiler_params=_params(("arbitrary", "arbitrary")),
        name="modulation",
    )(conds, w_mod, b_mod.reshape(DEPTH, 1, N_MOD * d))


def _swiglu_into(h16, wg_ref, wu_ref, wd_ref, a_ref):
    for c in range(D_FF // MXU_W):
        sl = slice(c * MXU_W, (c + 1) * MXU_W)
        g = _dot(h16, wg_ref[:, sl])
        u = _dot(h16, wu_ref[:, sl])
        a_ref[:, sl] = (_silu(g) * u).astype(BF16)
    return _dot(a_ref[...], wd_ref[...])


def _ffn1_then_norm(x_ref, mod_ref, g1_ref, g2_ref, wg_ref, wu_ref, wd_ref, o_ref, a_ref):
    x = x_ref[...]
    m = mod_ref[...]
    h = _mod_norm(x, g1_ref[...], m[1:2], m[0:1])
    y = _swiglu_into(h.astype(BF16), wg_ref, wu_ref, wd_ref, a_ref)
    x = x + (0.5 * m[2:3]) * y
    o_ref[...] = x
    return _mod_norm(x, g2_ref[...], m[4:5], m[3:4]).astype(BF16)


def _resident(shape, index_map):
    return pl.BlockSpec(shape, index_map, pipeline_mode=pl.Buffered(1))


def _mod_spec(l, row0, rows_per_cond, tm):
    return pl.BlockSpec((None, None, N_MOD, D_MODEL),
                        lambda i: (l, row0 + (i * tm) // rows_per_cond, 0, 0))


_IN_SPLITS = (ATTN_W, KV_W, KV_W, 2 * CONV_W, 3 * HYENA_W)
_IN_SPLITS_LAT = (ATTN_W, 2 * KV_W, 2 * KV_W, 2 * CONV_W, 3 * HYENA_W)


def _rope(x, cos, sin_signed):
    w = x.shape[-1]
    lane = lax.broadcasted_iota(jnp.int32, x.shape, 1)
    quarter = HEAD_DIM // 4
    partner = jnp.where(lane % (2 * quarter) < quarter,
                        pltpu.roll(x, w - quarter, 1), pltpu.roll(x, quarter, 1))
    return x * cos + partner * sin_signed


def _ffn_inproj_kernel(x_ref, mod_ref, g1_ref, g2_ref, wg_ref, wu_ref, wd_ref, w_ref,
                       o_ref, q_ref, k_ref, v_ref, cv_ref, hy_ref, a_ref):
    h = _ffn1_then_norm(x_ref, mod_ref, g1_ref, g2_ref, wg_ref, wu_ref, wd_ref, o_ref, a_ref)
    off = 0
    for ref, width in zip((q_ref, k_ref, v_ref, cv_ref, hy_ref), _IN_SPLITS):
        ref[...] = _dot(h, w_ref[:, off:off + width])
        off += width


def _ffn_inproj_lat_kernel(x_ref, mod_ref, g1_ref, g2_ref, wg_ref, wu_ref, wd_ref, w_ref, cos_ref, sin_ref,
                           o_ref, q_ref, k_ref, v_ref, cv_ref, hy_ref, a_ref):
    h = _ffn1_then_norm(x_ref, mod_ref, g1_ref, g2_ref, wg_ref, wu_ref, wd_ref, o_ref, a_ref)
    cos = cos_ref[...]
    sin = sin_ref[...]
    wq, wk, wv, wc, wh = _IN_SPLITS_LAT
    tile = lambda a, width: jnp.concatenate([a] * (width // a.shape[1]), axis=1)
    q = _rope(_dot(h, w_ref[:, 0:wq]), tile(cos, wq), tile(sin, wq))
    q_ref[...] = (q * (HEAD_DIM ** -0.5)).astype(BF16)
    off = wq
    k_ref[...] = _rope(_dot(h, w_ref[:, off:off + wk]), tile(cos, wk), tile(sin, wk)).astype(BF16)
    off += wk
    v_ref[...] = _dot(h, w_ref[:, off:off + wv]).astype(BF16)
    off += wv
    cv_ref[...] = _dot(h, w_ref[:, off:off + wc])
    off += wc
    hy_ref[...] = _dot(h, w_ref[:, off:off + wh])


def _ffn_inproj(x, mod4, g_ffn, g_mix, wg, wu, wd, w_in, rope_tables, *, l, row0, rows_per_cond, seq):
    t, d = x.shape
    tm = TOKEN_TILE
    lat = rope_tables is not None
    splits = _IN_SPLITS_LAT if lat else _IN_SPLITS
    dtypes = (BF16, BF16, BF16, F32, F32) if lat else (F32,) * 5
    gain_spec = pl.BlockSpec((None, 1, d), lambda i: (l, 0, 0))
    wspec_in = _resident((None, d, D_FF), lambda i: (l, 0, 0))
    in_specs = [
        pl.BlockSpec((tm, d), lambda i: (i, 0)),
        _mod_spec(l, row0, rows_per_cond, tm),
        gain_spec, gain_spec,
        wspec_in, wspec_in,
        _resident((None, D_FF, d), lambda i: (l, 0, 0)),
        _resident((None, d, sum(splits)), lambda i: (l, 0, 0)),
    ]
    args = [x, mod4, g_ffn.reshape(DEPTH, 1, d), g_mix.reshape(DEPTH, 1, d), wg, wu, wd, w_in]
    if lat:
        table_spec = pl.BlockSpec((tm, LANE), lambda i: (i % (seq // tm), 0))
        in_specs += [table_spec, table_spec]
        args += list(rope_tables)
    tok = lambda w: pl.BlockSpec((tm, w), lambda i: (i, 0))
    return pl.pallas_call(
        _ffn_inproj_lat_kernel if lat else _ffn_inproj_kernel,
        grid=(t // tm,),
        in_specs=in_specs,
        out_specs=[tok(d)] + [tok(w) for w in splits],
        out_shape=[jax.ShapeDtypeStruct((t, d), F32)]
        + [jax.ShapeDtypeStruct((t, w), dt) for w, dt in zip(splits, dtypes)],
        scratch_shapes=[pltpu.VMEM((tm, D_FF), BF16)],
        compiler_params=_params(("arbitrary",), VMEM_LIMIT),
        name="ffn_inproj",
    )(*args)


def _dup_kv_heads(x):
    swapped = pltpu.roll(x, HEAD_DIM, 1)
    low = lax.broadcasted_iota(jnp.int32, x.shape, 1) < HEAD_DIM
    return jnp.concatenate([jnp.where(low, x, swapped), jnp.where(low, swapped, x)], axis=1).astype(BF16)


def _group_attention(q, g, kd, vd, sink_ref, l, valid, o_ref):
    n = q.shape[0]
    c0 = 2 * g * LANE
    lhs = jnp.concatenate([q[:, c0:c0 + LANE], q[:, c0 + LANE:c0 + 2 * LANE]], axis=0)
    top = lax.broadcasted_iota(jnp.int32, (2 * n, 1), 0) < n
    low = lax.broadcasted_iota(jnp.int32, (2 * n, LANE), 1) < HEAD_DIM
    zero = jnp.zeros((2 * n, LANE), BF16)
    outs = []
    for half in range(2):
        s = _dot_nt(jnp.where(low, lhs, zero) if half == 0 else jnp.where(low, zero, lhs), kd)
        if valid is not None:
            nloc = valid.shape[1]
            s = jnp.concatenate([jnp.where(valid, s[:, :nloc], NEG_INF), s[:, nloc:]], axis=1)
        sk = jnp.where(top, sink_ref[l, 4 * g + half], sink_ref[l, 4 * g + 2 + half])
        mx = jnp.maximum(jnp.max(s, axis=-1, keepdims=True), sk)
        p = jnp.exp(s - mx)
        den = jnp.sum(p, axis=-1, keepdims=True) + jnp.exp(sk - mx)
        outs.append(_dot(p.astype(BF16), vd) / den)
    o = jnp.where(low, outs[0], outs[1]).astype(BF16)
    o_ref[:, c0:c0 + LANE] = o[:n]
    o_ref[:, c0 + LANE:c0 + 2 * LANE] = o[n:]


def _ctx_attn_kernel(sink_ref, q_ref, k_ref, v_ref, o_ref, *, l):
    q = (q_ref[...] * (HEAD_DIM ** -0.5)).astype(BF16)
    kd = _dup_kv_heads(k_ref[...])
    vd = _dup_kv_heads(v_ref[...])
    for g in range(KV_HEADS):
        gl = slice(g * LANE, (g + 1) * LANE)
        _group_attention(q, g, kd[:, gl], vd[:, gl], sink_ref, l, None, o_ref)


def _ctx_attention(q, k, v, sink, *, l, batch, seq):
    return pl.pallas_call(
        functools.partial(_ctx_attn_kernel, l=l),
        grid=(batch,),
        in_specs=[
            pl.BlockSpec(memory_space=pltpu.SMEM),
            pl.BlockSpec((seq, ATTN_W), lambda b: (b, 0)),
            pl.BlockSpec((seq, KV_W), lambda b: (b, 0)),
            pl.BlockSpec((seq, KV_W), lambda b: (b, 0)),
        ],
        out_specs=pl.BlockSpec((seq, ATTN_W), lambda b: (b, 0)),
        out_shape=jax.ShapeDtypeStruct((batch * seq, ATTN_W), BF16),
        compiler_params=_params(("arbitrary",)),
        name="ctx_attention",
    )(sink, q, k, v)


def _lat_attn_kernel(sink_ref, q_ref, k_ref, v_ref, kc_ref, vc_ref, o_ref, kcd_ref, vcd_ref, *, l, seq):
    i = pl.program_id(1)

    @pl.when(i == 0)
    def _():
        kcd_ref[...] = _dup_kv_heads(kc_ref[...])
        vcd_ref[...] = _dup_kv_heads(vc_ref[...])

    nloc = 3 * BLOCK
    start = pl.multiple_of(jnp.clip((i - 1) * BLOCK, 0, seq - nloc), BLOCK)
    q0 = i * BLOCK
    jpos = start + lax.broadcasted_iota(jnp.int32, (2 * BLOCK, nloc), 1)
    row = lax.broadcasted_iota(jnp.int32, (2 * BLOCK, nloc), 0)
    valid = jnp.abs(jpos - (q0 + row % BLOCK)) <= WINDOW
    q = q_ref[...]
    for g in range(KV_HEADS):
        gl = slice(g * LANE, (g + 1) * LANE)
        kd = jnp.concatenate([k_ref[pl.ds(start, nloc), gl], kcd_ref[:, gl]], axis=0)
        vd = jnp.concatenate([v_ref[pl.ds(start, nloc), gl], vcd_ref[:, gl]], axis=0)
        _group_attention(q, g, kd, vd, sink_ref, l, valid, o_ref)


def _lat_attention(q, k, v, cache_k, cache_v, sink, *, l, batch, seq):
    nb = seq // BLOCK
    past = cache_k.shape[2]
    kv_spec = pl.BlockSpec((seq, 2 * KV_W), lambda b, i: (b, 0))
    cache_spec = pl.BlockSpec((None, None, past, KV_W), lambda b, i: (b, l, 0, 0))
    return pl.pallas_call(
        functools.partial(_lat_attn_kernel, l=l, seq=seq),
        grid=(batch, nb),
        in_specs=[
            pl.BlockSpec(memory_space=pltpu.SMEM),
            pl.BlockSpec((BLOCK, ATTN_W), lambda b, i: (b * nb + i, 0)),
            kv_spec, kv_spec, cache_spec, cache_spec,
        ],
        out_specs=pl.BlockSpec((BLOCK, ATTN_W), lambda b, i: (b * nb + i, 0)),
        out_shape=jax.ShapeDtypeStruct((batch * seq, ATTN_W), BF16),
        scratch_shapes=[pltpu.VMEM((past, 2 * KV_W), BF16), pltpu.VMEM((past, 2 * KV_W), BF16)],
        compiler_params=_params(("arbitrary", "arbitrary")),
        name="lat_attention",
    )(sink, q, k, v, cache_k, cache_v)


def _rope_tables(seq):
    rows = seq // GRID_W
    row = jnp.repeat(jnp.arange(rows), GRID_W)
    col = jnp.arange(rows * GRID_W) % GRID_W
    nf = HEAD_DIM // 4
    inv = ROPE_BASE ** (-jnp.arange(nf, dtype=F32) / nf)
    ang_r = row.astype(F32)[:, None] * inv[None, :]
    ang_c = col.astype(F32)[:, None] * inv[None, :]
    cos_h = jnp.concatenate([jnp.cos(ang_r)] * 2 + [jnp.cos(ang_c)] * 2, axis=1)
    sin_h = jnp.concatenate([-jnp.sin(ang_r), jnp.sin(ang_r), -jnp.sin(ang_c), jnp.sin(ang_c)], axis=1)
    return jnp.tile(cos_h, (1, KV_HEADS)), jnp.tile(sin_h, (1, KV_HEADS))


def _conv_kernel(cv_ref, dw_ref, dwb_ref, lng_ref, lnb_ref, pw_ref, o_ref, pad_ref, *, seq):
    x = cv_ref[...]
    zeros = jnp.zeros((CONV_PAD, CONV_W), F32)
    pad_ref[0:CONV_PAD, :] = zeros
    pad_ref[CONV_PAD:CONV_PAD + seq, :] = x[:, :CONV_W] * jax.nn.sigmoid(x[:, CONV_W:])
    pad_ref[CONV_PAD + seq:2 * CONV_PAD + seq, :] = zeros
    dw = dw_ref[...]
    pw = pw_ref[...].astype(BF16)
    first = CONV_PAD - CONV_K // 2

    def chunk(c, carry):
        base = pl.multiple_of(c * CONV_CHUNK, CONV_CHUNK)
        win = pad_ref[pl.ds(base, CONV_CHUNK + 2 * CONV_PAD), :]
        acc = jnp.zeros((CONV_CHUNK, CONV_W), F32)
        for r in range(SUBLANE):
            offs = [first + k for k in range(CONV_K) if (first + k) % SUBLANE == r]
            if not offs:
                continue
            shifted = win if r == 0 else pltpu.roll(win, win.shape[0] - r, 0)
            for off in offs:
                acc = acc + shifted[off - r:off - r + CONV_CHUNK] * dw[off - first:off - first + 1]
        acc = acc + dwb_ref[...]
        mu = jnp.mean(acc, axis=-1, keepdims=True)
        cen = acc - mu
        var = jnp.mean(cen * cen, axis=-1, keepdims=True)
        y = cen * lax.rsqrt(var + EPS) * lng_ref[...] + lnb_ref[...]
        o_ref[pl.ds(base, CONV_CHUNK), :] = _dot(_silu(y).astype(BF16), pw).astype(BF16)
        return carry

    lax.fori_loop(0, seq // CONV_CHUNK, chunk, 0)


def _conformer_conv(cv, dw, dwb, lng, lnb, pw, *, l, batch, seq):
    kpad = 2 * CONV_PAD
    dw_p = jnp.pad(dw, ((0, 0), (0, kpad - CONV_K), (0, 0)))
    vec = lambda a: a.reshape(DEPTH, 1, CONV_W)
    vspec = pl.BlockSpec((None, 1, CONV_W), lambda b: (l, 0, 0))
    return pl.pallas_call(
        functools.partial(_conv_kernel, seq=seq),
        grid=(batch,),
        in_specs=[
            pl.BlockSpec((seq, 2 * CONV_W), lambda b: (b, 0)),
            pl.BlockSpec((None, kpad, CONV_W), lambda b: (l, 0, 0)),
            vspec, vspec, vspec,
            pl.BlockSpec((None, CONV_W, CONV_W), lambda b: (l, 0, 0)),
        ],
        out_specs=pl.BlockSpec((seq, CONV_W), lambda b: (b, 0)),
        out_shape=jax.ShapeDtypeStruct((batch * seq, CONV_W), BF16),
        scratch_shapes=[pltpu.VMEM((seq + 2 * CONV_PAD, CONV_W), F32)],
        compiler_params=_params(("arbitrary",)),
        name="conformer_conv",
    )(cv, dw_p, vec(dwb), vec(lng), vec(lnb), pw)


def _short_conv3(x, w, b):
    n = x.shape[0]
    row = lax.broadcasted_iota(jnp.int32, x.shape, 0)
    prev = jnp.where(row == 0, 0.0, pltpu.roll(x, 1, 0))
    nxt = jnp.where(row == n - 1, 0.0, pltpu.roll(x, n - 1, 0))
    return prev * w[0:1] + x * w[1:2] + nxt * w[2:3] + b


def _filter_kernel(z_ref, w1_ref, b1_ref, f1_ref, w2_ref, b2_ref, f2_ref, w3_ref, ld_ref, o_ref):
    z = z_ref[...]
    h = jnp.sin(f1_ref[...] * (_dot_hi(z, w1_ref[...]) + b1_ref[...]))
    h = jnp.sin(f2_ref[...] * (_dot_hi(h, w2_ref[...]) + b2_ref[...]))
    h = _dot_hi(h, w3_ref[...])
    tn = z[:, 0:1]
    h = h * jnp.exp(-tn * jnp.exp(ld_ref[...]))
    ss = jnp.sum(h * h, axis=0, keepdims=True)
    c = HYENA_W
    scale = []
    for o in range(2):
        tot = ss[:, 2 * o * c:(2 * o + 1) * c] + ss[:, (2 * o + 1) * c:(2 * o + 2) * c]
        r = lax.rsqrt(tot + EPS)
        scale += [r, r]
    h = h * jnp.concatenate(scale, axis=1)
    row = lax.broadcasted_iota(jnp.int32, h.shape, 0)
    col = lax.broadcasted_iota(jnp.int32, h.shape, 1)
    o_ref[...] = jnp.where((row == 0) & ((col // c) % 2 == 1), 0.0, h)


def _hyena_features(seq):
    t = jnp.arange(seq, dtype=F32)
    tn = t / (seq - 1)
    bands = jnp.linspace(1e-4, HYENA_BANDS - 1, HYENA_BANDS, dtype=F32)
    ang = 2.0 * math.pi * t[:, None] * bands[None, :] / seq
    z = jnp.concatenate([tn[:, None], jnp.cos(ang), -jnp.sin(ang)], axis=-1)
    return jnp.pad(z, ((0, 0), (0, LANE - HYENA_EMB)))


def _hyena_filters(z, w1, b1, f1, w2, b2, f2, w3, log_decay):
    seq = z.shape[0]
    n = w3.shape[1]
    return pl.pallas_call(
        _filter_kernel,
        out_shape=jax.ShapeDtypeStruct((seq, n), F32),
        compiler_params=_params(None, VMEM_LIMIT),
        name="hyena_filters",
    )(z, w1, b1, f1, w2, b2, f2, w3, log_decay)


def _spectrum_kernel(m_ref, h_ref, a_ref, b_ref, d_ref, *, n_fft):
    j = pl.program_id(0)
    fc = m_ref.shape[1]
    c = HYENA_W
    h16 = h_ref[...].astype(BF16)
    gc = _dot(m_ref[0], h16)
    gs = _dot(m_ref[1], h16)
    row = j * fc + lax.broadcasted_iota(jnp.int32, (fc, c), 0)
    is0 = row == 0
    wgt = jnp.where(is0, 1.0 / n_fft, 2.0 / n_fft)
    for o in range(2):
        f = slice(2 * o * c, (2 * o + 1) * c)
        b = slice((2 * o + 1) * c, (2 * o + 2) * c)
        re = gc[:, f] + gc[:, b]
        a_ref[o] = re * wgt
        b_ref[o] = jnp.where(is0, 0.0, gs[:, f] - gs[:, b]) * wgt
        d_ref[o] = jnp.where(is0, gs[:, f] + gs[:, b], re) * wgt


def _filter_spectra(dft, hcat, *, fc):
    _, nf, seq = dft.shape
    c = HYENA_W
    out_spec = pl.BlockSpec((2, fc, c), lambda j: (0, j, 0))
    shape = jax.ShapeDtypeStruct((2, nf, c), F32)
    return pl.pallas_call(
        functools.partial(_spectrum_kernel, n_fft=2 * seq),
        grid=(nf // fc,),
        in_specs=[
            pl.BlockSpec((2, fc, seq), lambda j: (0, j, 0)),
            pl.BlockSpec(hcat.shape, lambda j: (0, 0)),
        ],
        out_specs=[out_spec, out_spec, out_spec],
        out_shape=[shape, shape, shape],
        compiler_params=_params(("arbitrary",), VMEM_LIMIT),
        name="hyena_spectra",
    )(dft, hcat)


def _dot_tn(a, b):
    return lax.dot_general(a, b, (((0,), (0,)), ((), ())), preferred_element_type=F32)


def _longconv_accumulate(u_ref, m_ref, a_ref, b_ref, d_ref, acc_ref, *, seq):
    a = a_ref[...]
    b = b_ref[...]
    d = d_ref[...]
    for bi in range(u_ref.shape[0] // seq):
        rows = slice(bi * seq, (bi + 1) * seq)
        u = u_ref[rows, :]
        ur = _dot(m_ref[0], u)
        ui = _dot(m_ref[1], u)
        yr = (ur * a - ui * b).astype(BF16)
        yi = (ur * b + ui * d).astype(BF16)
        acc_ref[rows, :] += _dot_tn(m_ref[0], yr) + _dot_tn(m_ref[1], yi)


def _hyena_conv0_kernel(hv_ref, hx_ref, sw_ref, sb_ref, bias_ref, m_ref, a_ref, b_ref, d_ref,
                        y32_ref, y16_ref, u16_ref, acc_ref, *, seq):
    c = HYENA_W
    j = pl.program_id(1)
    nseq = hv_ref.shape[0] // seq

    @pl.when(j == 0)
    def _():
        for bi in range(nseq):
            rows = slice(bi * seq, (bi + 1) * seq)
            u16_ref[rows, :] = _short_conv3(hv_ref[rows, :], sw_ref[:, 0:c], sb_ref[:, 0:c]).astype(BF16)
        acc_ref[...] = jnp.zeros(acc_ref.shape, F32)

    _longconv_accumulate(u16_ref, m_ref, a_ref, b_ref, d_ref, acc_ref, seq=seq)

    @pl.when(j == pl.num_programs(1) - 1)
    def _():
        for bi in range(nseq):
            rows = slice(bi * seq, (bi + 1) * seq)
            v = _short_conv3(hv_ref[rows, :], sw_ref[:, 0:c], sb_ref[:, 0:c])
            x1 = _short_conv3(hx_ref[rows, :], sw_ref[:, c:2 * c], sb_ref[:, c:2 * c])
            y = x1 * (acc_ref[rows, :] + bias_ref[0:1] * v)
            y32_ref[rows, :] = y
            y16_ref[rows, :] = y.astype(BF16)


def _hyena_conv1_kernel(y16_ref, y32_ref, hx_ref, sw_ref, sb_ref, bias_ref, m_ref, a_ref, b_ref, d_ref,
                        o_ref, acc_ref, *, seq):
    c = HYENA_W
    j = pl.program_id(1)

    @pl.when(j == 0)
    def _():
        acc_ref[...] = jnp.zeros(acc_ref.shape, F32)

    _longconv_accumulate(y16_ref, m_ref, a_ref, b_ref, d_ref, acc_ref, seq=seq)

    @pl.when(j == pl.num_programs(1) - 1)
    def _():
        for bi in range(y16_ref.shape[0] // seq):
            rows = slice(bi * seq, (bi + 1) * seq)
            x2 = _short_conv3(hx_ref[rows, :], sw_ref[:, 2 * c:3 * c], sb_ref[:, 2 * c:3 * c])
            o_ref[rows, :] = (x2 * (acc_ref[rows, :] + bias_ref[1:2] * y32_ref[rows, :])).astype(BF16)


def _hyena_mixer(hy, sw, sb, bias, dft, spectra, *, l, batch, seq, bg, fc):
    c = HYENA_W
    t = batch * seq
    nf = dft.shape[1]
    grid = (batch // bg, nf // fc)
    rows = bg * seq
    sw_p = jnp.pad(sw, ((0, 0), (0, SUBLANE - sw.shape[1]), (0, 0)))
    sb3 = sb.reshape(DEPTH, 1, 3 * c)
    chan = lambda ch: _resident((rows, c), lambda g, j: (g, ch))
    sw_spec = pl.BlockSpec((None, SUBLANE, 3 * c), lambda g, j: (l, 0, 0))
    sb_spec = pl.BlockSpec((None, 1, 3 * c), lambda g, j: (l, 0, 0))
    bias_spec = pl.BlockSpec((None, 2, c), lambda g, j: (l, 0, 0))
    dft_spec = pl.BlockSpec((2, fc, seq), lambda g, j: (0, j, 0))
    coef = lambda order: [pl.BlockSpec((None, fc, c), lambda g, j: (order, j, 0))] * 3
    out_spec = pl.BlockSpec((rows, c), lambda g, j: (g, 0))
    params = _params(("arbitrary", "arbitrary"), VMEM_LIMIT)

    y32, y16 = pl.pallas_call(
        functools.partial(_hyena_conv0_kernel, seq=seq),
        grid=grid,
        in_specs=[chan(0), chan(1), sw_spec, sb_spec, bias_spec, dft_spec] + coef(0),
        out_specs=[out_spec, out_spec],
        out_shape=[jax.ShapeDtypeStruct((t, c), F32), jax.ShapeDtypeStruct((t, c), BF16)],
        scratch_shapes=[pltpu.VMEM((rows, c), BF16), pltpu.VMEM((rows, c), F32)],
        compiler_params=params,
        name="hyena_conv0",
    )(hy, hy, sw_p, sb3, bias, dft, *spectra)

    return pl.pallas_call(
        functools.partial(_hyena_conv1_kernel, seq=seq),
        grid=grid,
        in_specs=[_resident((rows, c), lambda g, j: (g, 0)), _resident((rows, c), lambda g, j: (g, 0)),
                  chan(2), sw_spec, sb_spec, bias_spec, dft_spec] + coef(1),
        out_specs=out_spec,
        out_shape=jax.ShapeDtypeStruct((t, c), BF16),
        scratch_shapes=[pltpu.VMEM((rows, c), F32)],
        compiler_params=params,
        name="hyena_conv1",
    )(y16, y32, hy, sw_p, sb3, bias, dft, *spectra)


def _dft_matrix(seq):
    n = 2 * seq
    t = np.arange(seq, dtype=np.int64)
    nyq = np.where(t % 2 == 0, 1.0, -1.0).astype(np.float32)
    if seq <= 256:
        k = (t[:, None] * t[None, :]) % n
        ang = 2.0 * np.pi * k.astype(np.float64) / n
        m = np.stack([np.cos(ang), -np.sin(ang)]).astype(np.float32)
        m[1, 0] = nyq
        return jnp.asarray(m)
    f0n = 64
    f1n = seq // f0n
    ang1 = 2.0 * np.pi * ((np.arange(f1n)[:, None] * f0n * t[None, :]) % n).astype(np.float64) / n
    ang0 = 2.0 * np.pi * ((np.arange(f0n)[:, None] * t[None, :]) % n).astype(np.float64) / n
    c1 = jnp.asarray(np.cos(ang1), F32)[:, None, :]
    s1 = jnp.asarray(np.sin(ang1), F32)[:, None, :]
    c0 = jnp.asarray(np.cos(ang0), F32)[None, :, :]
    s0 = jnp.asarray(np.sin(ang0), F32)[None, :, :]
    mc = (c1 * c0 - s1 * s0).reshape(seq, seq)
    ms = (-(s1 * c0 + c1 * s0)).reshape(seq, seq)
    row = lax.broadcasted_iota(jnp.int32, (seq, seq), 0)
    ms = jnp.where(row == 0, jnp.asarray(nyq)[None, :], ms)
    return jnp.stack([mc, ms])


def _mixffn_kernel(x_ref, att_ref, cvo_ref, hyo_ref, mod_ref, wo_ref, g_ref,
                   wg_ref, wu_ref, wd_ref, gf_ref, o_ref, a_ref, *, final):
    m = mod_ref[...]
    o1 = ATTN_W
    o2 = ATTN_W + CONV_W
    mix = (_dot(att_ref[...], wo_ref[0:o1, :]) + _dot(cvo_ref[...], wo_ref[o1:o2, :])
           + _dot(hyo_ref[...], wo_ref[o2:, :]))
    x = x_ref[...] + m[5:6] * mix
    h = _mod_norm(x, g_ref[...], m[7:8], m[6:7])
    y = _swiglu_into(h.astype(BF16), wg_ref, wu_ref, wd_ref, a_ref)
    x = x + (0.5 * m[8:9]) * y
    if final:
        x = x * lax.rsqrt(jnp.mean(x * x, axis=-1, keepdims=True) + EPS) * gf_ref[...]
    o_ref[...] = x


def _mixffn(x, att, cvo, hyo, mod4, w_out, gain, wg, wu, wd, g_final, *, l, row0, rows_per_cond, final):
    t, d = x.shape
    tm = TOKEN_TILE
    tok = lambda w: pl.BlockSpec((tm, w), lambda i: (i, 0))
    wspec_in = _resident((None, d, D_FF), lambda i: (l, 0, 0))
    return pl.pallas_call(
        functools.partial(_mixffn_kernel, final=final),
        grid=(t // tm,),
        in_specs=[
            tok(d), tok(ATTN_W), tok(CONV_W), tok(HYENA_W),
            _mod_spec(l, row0, rows_per_cond, tm),
            _resident((None, d, d), lambda i: (l, 0, 0)),
            pl.BlockSpec((None, 1, d), lambda i: (l, 0, 0)),
            wspec_in, wspec_in,
            _resident((None, D_FF, d), lambda i: (l, 0, 0)),
            pl.BlockSpec((1, d), lambda i: (0, 0)),
        ],
        out_specs=tok(d),
        out_shape=jax.ShapeDtypeStruct((t, d), F32),
        scratch_shapes=[pltpu.VMEM((tm, D_FF), BF16)],
        compiler_params=_params(("arbitrary",), VMEM_LIMIT),
        name="mixffn",
    )(x, att, cvo, hyo, mod4, w_out, gain.reshape(DEPTH, 1, d), wg, wu, wd, g_final.reshape(1, d))


def _pad_cols(a, width):
    return jnp.pad(a, [(0, 0)] * (a.ndim - 1) + [(0, width - a.shape[-1])])


def _trunk(x, cache, mod4, P, *, batch, seq, row0, rows_per_cond):
    fc = min(seq, FREQ_CHUNK)
    bg = max(1, min(batch, LONGCONV_ROWS // seq))
    dft = _dft_matrix(seq).astype(BF16)
    z = _hyena_features(seq)
    rope_tables = None if cache is None else _rope_tables(seq)
    w_in = P['w_in'] if cache is None else P['w_in_lat']
    cond = dict(row0=row0, rows_per_cond=rows_per_cond)
    ks, vs = [], []
    for l in range(DEPTH):
        x, q, k, v, cv, hy = _ffn_inproj(x, mod4, P['g_ffn1'], P['g_mix'], P['w1_gate'], P['w1_up'],
                                         P['w1_down'], w_in, rope_tables, l=l, seq=seq, **cond)
        if cache is None:
            att = _ctx_attention(q, k, v, P['attn_sink'], l=l, batch=batch, seq=seq)
            ks.append(k)
            vs.append(v)
        else:
            att = _lat_attention(q, k, v, cache[0], cache[1], P['attn_sink'], l=l, batch=batch, seq=seq)
        cvo = _conformer_conv(cv, P['conv_dw'], P['conv_dw_b'], P['conv_ln_g'], P['conv_ln_b'],
                              P['conv_pw'], l=l, batch=batch, seq=seq)
        hcat = _hyena_filters(z, P['hy_w1'][l], P['hy_b1'][l], P['hy_f1'][l], P['hy_w2'][l],
                              P['hy_b2'][l], P['hy_f2'][l], P['hy_w3'][l], P['hy_log_decay'][l])
        spectra = _filter_spectra(dft, hcat, fc=fc)
        hyo = _hyena_mixer(hy, P['hy_short_w'], P['hy_short_b'], P['hy_bias'], dft, spectra,
                           l=l, batch=batch, seq=seq, bg=bg, fc=fc)
        x = _mixffn(x, att, cvo, hyo, mod4, P['w_out'], P['g_ffn2'], P['w2_gate'], P['w2_up'],
                    P['w2_down'], P['g_final'], l=l, final=(l == DEPTH - 1), **cond)
    return x, ks, vs


def kernel(x_prompt, x_sample, c, cache_k, cache_v, c_ctx, w_mod, b_mod, g_ffn1, g_mix, g_ffn2,
           g_final, w1_gate, w1_up, w1_down, w2_gate, w2_up, w2_down, w_in, w_out, attn_sink,
           conv_dw, conv_dw_b, conv_ln_g, conv_ln_b, conv_pw, hy_short_w, hy_short_b,
           hy_w1, hy_b1, hy_f1, hy_w2, hy_b2, hy_f2, hy_w3, hy_log_decay, hy_bias):
    batch, seq, d = x_prompt.shape
    dec_batch, dec_seq, _ = x_sample.shape
    assert 1 + dec_batch <= COND_ROWS

    hid = LANE
    w_in16 = w_in.astype(BF16)
    o1, o2, o3 = ATTN_W, ATTN_W + KV_W, ATTN_W + 2 * KV_W

    def dup_heads(w):
        g0, g1 = w[..., :HEAD_DIM], w[..., HEAD_DIM:]
        return jnp.concatenate([g0, g0, g1, g1], axis=-1)

    w_in_lat = jnp.concatenate([w_in16[..., :o1], dup_heads(w_in16[..., o1:o2]),
                                dup_heads(w_in16[..., o2:o3]), w_in16[..., o3:]], axis=-1)
    P = dict(
        w_in_lat=w_in_lat,
        g_ffn1=g_ffn1, g_mix=g_mix, g_ffn2=g_ffn2, g_final=g_final, attn_sink=attn_sink,
        w1_gate=w1_gate.astype(BF16), w1_up=w1_up.astype(BF16), w1_down=w1_down.astype(BF16),
        w2_gate=w2_gate.astype(BF16), w2_up=w2_up.astype(BF16), w2_down=w2_down.astype(BF16),
        w_in=w_in16, w_out=w_out.astype(BF16),
        conv_dw=conv_dw, conv_dw_b=conv_dw_b, conv_ln_g=conv_ln_g, conv_ln_b=conv_ln_b, conv_pw=conv_pw,
        hy_short_w=hy_short_w, hy_short_b=hy_short_b,
        hy_w1=jnp.pad(hy_w1, ((0, 0), (0, hid - HYENA_EMB), (0, hid - HYENA_HID))),
        hy_b1=_pad_cols(hy_b1, hid)[:, None, :], hy_f1=_pad_cols(hy_f1, hid)[:, None, :],
        hy_w2=jnp.pad(hy_w2, ((0, 0), (0, hid - HYENA_HID), (0, hid - HYENA_HID))),
        hy_b2=_pad_cols(hy_b2, hid)[:, None, :], hy_f2=_pad_cols(hy_f2, hid)[:, None, :],
        hy_w3=jnp.pad(hy_w3, ((0, 0), (0, hid - HYENA_HID), (0, 0))),
        hy_log_decay=hy_log_decay[:, None, :], hy_bias=hy_bias,
    )

    conds = jnp.concatenate([c_ctx[None, :], c, jnp.zeros((COND_ROWS - 1 - dec_batch, d), F32)], axis=0)
    mod = _modulation(conds, w_mod, b_mod)
    mod4 = mod.reshape(DEPTH, COND_ROWS, N_MOD, d)

    xp, ks, vs = _trunk(x_prompt.reshape(batch * seq, d), None, mod4, P,
                        batch=batch, seq=seq, row0=0, rows_per_cond=batch * seq)
    cache = (cache_k.reshape(*cache_k.shape[:3], KV_W), cache_v.reshape(*cache_v.shape[:3], KV_W))
    xs, _, _ = _trunk(x_sample.reshape(dec_batch * dec_seq, d), cache, mod4, P,
                      batch=dec_batch, seq=dec_seq, row0=1, rows_per_cond=dec_seq)

    kv_shape = (batch, DEPTH, seq, KV_HEADS, HEAD_DIM)
    new_k = jnp.stack([k.reshape(batch, seq, KV_W) for k in ks], axis=1).reshape(kv_shape)
    new_v = jnp.stack([v.reshape(batch, seq, KV_W) for v in vs], axis=1).reshape(kv_shape)
    return (xp.reshape(batch, seq, d), xs.reshape(dec_batch, dec_seq, d), new_k, new_v)
```

```python
import functools
import math

import numpy as np
import jax
import jax.numpy as jnp
from jax import lax
from jax.experimental import pallas as pl
from jax.experimental.pallas import tpu as pltpu

F32 = jnp.float32
BF16 = jnp.bfloat16

D_MODEL = 1024
DEPTH = 2
GRID_W = 64
N_HEADS = 8
KV_HEADS = 2
HEAD_DIM = 64
Q_PER_KV = N_HEADS // KV_HEADS
ATTN_W = N_HEADS * HEAD_DIM
KV_W = KV_HEADS * HEAD_DIM
CONV_W = D_MODEL // 4
HYENA_W = D_MODEL // 4
WINDOW = 128
BLOCK = 128
CONV_K = 31
HYENA_EMB = 33
HYENA_BANDS = (HYENA_EMB - 1) // 2
HYENA_HID = 64
D_FF = 2816
N_MOD = 9
ROPE_BASE = 10000.0
EPS = 1e-6
NEG_INF = -1e30

LANE = 128
SUBLANE = 8
MXU_W = 256
COND_ROWS = 16
TOKEN_TILE = 512
CONV_CHUNK = 128
CONV_PAD = 16
FREQ_CHUNK = 512
LONGCONV_ROWS = 4096
LONGCONV_FREQ_CHUNK = 1024
LAT_QUERIES = 256
VMEM_LIMIT = 56 * 1024 * 1024


def _params(sem, vmem=None):
    return pltpu.CompilerParams(dimension_semantics=sem, vmem_limit_bytes=vmem)


def _silu(x):
    return x * jax.nn.sigmoid(x)


def _dot(a, b):
    return jnp.dot(a, b, preferred_element_type=F32)


def _dot_nt(a, b):
    return lax.dot_general(a, b, (((1,), (1,)), ((), ())), preferred_element_type=F32)


def _dot_hi(a, b):
    return jnp.dot(a, b, preferred_element_type=F32, precision=lax.Precision.HIGHEST)


def _mod_norm(x, gain, scale, shift):
    y = x * lax.rsqrt(jnp.mean(x * x, axis=-1, keepdims=True) + EPS)
    return (y * gain) * (1.0 + scale) + shift


def _mod_kernel(c_ref, w_ref, b_ref, o_ref):
    s = _silu(c_ref[...]).astype(BF16)
    o_ref[...] = _dot(s, w_ref[...].astype(BF16)) + b_ref[...]


def _modulation(conds, w_mod, b_mod):
    d = D_MODEL
    return pl.pallas_call(
        _mod_kernel,
        grid=(DEPTH, N_MOD),
        in_specs=[
            pl.BlockSpec((COND_ROWS, d), lambda l, j: (0, 0)),
            pl.BlockSpec((None, d, d), lambda l, j: (l, 0, j)),
            pl.BlockSpec((None, 1, d), lambda l, j: (l, 0, j)),
        ],
        out_specs=pl.BlockSpec((None, COND_ROWS, d), lambda l, j: (l, 0, j)),
        out_shape=jax.ShapeDtypeStruct((DEPTH, COND_ROWS, N_MOD * d), F32),
        compiler_params=_params(("arbitrary", "arbitrary")),
        name="modulation",
    )(conds, w_mod, b_mod.reshape(DEPTH, 1, N_MOD * d))


def _swiglu_into(h16, wg_ref, wu_ref, wd_ref, a_ref):
    for c in range(D_FF // MXU_W):
        sl = slice(c * MXU_W, (c + 1) * MXU_W)
        g = _dot(h16, wg_ref[:, sl])
        u = _dot(h16, wu_ref[:, sl])
        a_ref[:, sl] = (_silu(g) * u).astype(BF16)
    return _dot(a_ref[...], wd_ref[...])


def _ffn1_then_norm(x_ref, mod_ref, g1_ref, g2_ref, wg_ref, wu_ref, wd_ref, o_ref, a_ref):
    x = x_ref[...]
    m = mod_ref[...]
    h = _mod_norm(x, g1_ref[...], m[1:2], m[0:1])
    y = _swiglu_into(h.astype(BF16), wg_ref, wu_ref, wd_ref, a_ref)
    x = x + (0.5 * m[2:3]) * y
    o_ref[...] = x
    return _mod_norm(x, g2_ref[...], m[4:5], m[3:4]).astype(BF16)


def _resident(shape, index_map):
    return pl.BlockSpec(shape, index_map, pipeline_mode=pl.Buffered(1))


def _mod_spec(l, row0, rows_per_cond, tm):
    return pl.BlockSpec((None, None, N_MOD, D_MODEL),
                        lambda i: (l, row0 + (i * tm) // rows_per_cond, 0, 0))


_IN_SPLITS = (ATTN_W, KV_W, KV_W, 2 * CONV_W, 3 * HYENA_W)
_IN_SPLITS_LAT = (ATTN_W, 2 * KV_W, 2 * KV_W, 2 * CONV_W, 3 * HYENA_W)


def _rope(x, cos, sin_signed):
    w = x.shape[-1]
    lane = lax.broadcasted_iota(jnp.int32, x.shape, 1)
    quarter = HEAD_DIM // 4
    partner = jnp.where(lane % (2 * quarter) < quarter,
                        pltpu.roll(x, w - quarter, 1), pltpu.roll(x, quarter, 1))
    return x * cos + partner * sin_signed


def _ffn_inproj_kernel(x_ref, mod_ref, g1_ref, g2_ref, wg_ref, wu_ref, wd_ref, w_ref,
                       o_ref, q_ref, k_ref, v_ref, cv_ref, hy_ref, a_ref):
    h = _ffn1_then_norm(x_ref, mod_ref, g1_ref, g2_ref, wg_ref, wu_ref, wd_ref, o_ref, a_ref)
    off = 0
    for ref, width in zip((q_ref, k_ref, v_ref, cv_ref, hy_ref), _IN_SPLITS):
        ref[...] = _dot(h, w_ref[:, off:off + width])
        off += width


def _ffn_inproj_lat_kernel(x_ref, mod_ref, g1_ref, g2_ref, wg_ref, wu_ref, wd_ref, w_ref, cos_ref, sin_ref,
                           o_ref, q_ref, k_ref, v_ref, cv_ref, hy_ref, a_ref):
    h = _ffn1_then_norm(x_ref, mod_ref, g1_ref, g2_ref, wg_ref, wu_ref, wd_ref, o_ref, a_ref)
    cos = cos_ref[...]
    sin = sin_ref[...]
    wq, wk, wv, wc, wh = _IN_SPLITS_LAT
    tile = lambda a, width: jnp.concatenate([a] * (width // a.shape[1]), axis=1)
    q = _rope(_dot(h, w_ref[:, 0:wq]), tile(cos, wq), tile(sin, wq))
    q_ref[...] = (q * (HEAD_DIM ** -0.5)).astype(BF16)
    off = wq
    k_ref[...] = _rope(_dot(h, w_ref[:, off:off + wk]), tile(cos, wk), tile(sin, wk)).astype(BF16)
    off += wk
    v_ref[...] = _dot(h, w_ref[:, off:off + wv]).astype(BF16)
    off += wv
    cv_ref[...] = _dot(h, w_ref[:, off:off + wc])
    off += wc
    hy_ref[...] = _dot(h, w_ref[:, off:off + wh])


def _ffn_inproj(x, mod4, g_ffn, g_mix, wg, wu, wd, w_in, rope_tables, *, l, row0, rows_per_cond, seq):
    t, d = x.shape
    tm = TOKEN_TILE
    lat = rope_tables is not None
    splits = _IN_SPLITS_LAT if lat else _IN_SPLITS
    dtypes = (BF16, BF16, BF16, F32, F32) if lat else (F32,) * 5
    gain_spec = pl.BlockSpec((None, 1, d), lambda i: (l, 0, 0))
    wspec_in = _resident((None, d, D_FF), lambda i: (l, 0, 0))
    in_specs = [
        pl.BlockSpec((tm, d), lambda i: (i, 0)),
        _mod_spec(l, row0, rows_per_cond, tm),
        gain_spec, gain_spec,
        wspec_in, wspec_in,
        _resident((None, D_FF, d), lambda i: (l, 0, 0)),
        _resident((None, d, sum(splits)), lambda i: (l, 0, 0)),
    ]
    args = [x, mod4, g_ffn.reshape(DEPTH, 1, d), g_mix.reshape(DEPTH, 1, d), wg, wu, wd, w_in]
    if lat:
        table_spec = pl.BlockSpec((tm, LANE), lambda i: (i % (seq // tm), 0))
        in_specs += [table_spec, table_spec]
        args += list(rope_tables)
    tok = lambda w: pl.BlockSpec((tm, w), lambda i: (i, 0))
    return pl.pallas_call(
        _ffn_inproj_lat_kernel if lat else _ffn_inproj_kernel,
        grid=(t // tm,),
        in_specs=in_specs,
        out_specs=[tok(d)] + [tok(w) for w in splits],
        out_shape=[jax.ShapeDtypeStruct((t, d), F32)]
        + [jax.ShapeDtypeStruct((t, w), dt) for w, dt in zip(splits, dtypes)],
        scratch_shapes=[pltpu.VMEM((tm, D_FF), BF16)],
        compiler_params=_params(("arbitrary",), VMEM_LIMIT),
        name="ffn_inproj",
    )(*args)


def _dup_kv_heads(x):
    swapped = pltpu.roll(x, HEAD_DIM, 1)
    low = lax.broadcasted_iota(jnp.int32, x.shape, 1) < HEAD_DIM
    return jnp.concatenate([jnp.where(low, x, swapped), jnp.where(low, swapped, x)], axis=1).astype(BF16)


def _group_attention(q, g, kd, vd, sink_ref, l, valid, o_ref):
    n = q.shape[0]
    c0 = 2 * g * LANE
    lhs = jnp.concatenate([q[:, c0:c0 + LANE], q[:, c0 + LANE:c0 + 2 * LANE]], axis=0)
    top = lax.broadcasted_iota(jnp.int32, (2 * n, 1), 0) < n
    low = lax.broadcasted_iota(jnp.int32, (2 * n, LANE), 1) < HEAD_DIM
    zero = jnp.zeros((2 * n, LANE), BF16)
    outs = []
    for half in range(2):
        s = _dot_nt(jnp.where(low, lhs, zero) if half == 0 else jnp.where(low, zero, lhs), kd)
        if valid is not None:
            nloc = valid.shape[1]
            s = jnp.concatenate([jnp.where(valid, s[:, :nloc], NEG_INF), s[:, nloc:]], axis=1)
        sk = jnp.where(top, sink_ref[l, 4 * g + half], sink_ref[l, 4 * g + 2 + half])
        mx = jnp.maximum(jnp.max(s, axis=-1, keepdims=True), sk)
        p = jnp.exp(s - mx)
        den = jnp.sum(p, axis=-1, keepdims=True) + jnp.exp(sk - mx)
        outs.append(_dot(p.astype(BF16), vd) / den)
    o = jnp.where(low, outs[0], outs[1]).astype(BF16)
    o_ref[:, c0:c0 + LANE] = o[:n]
    o_ref[:, c0 + LANE:c0 + 2 * LANE] = o[n:]


def _ctx_attn_kernel(sink_ref, q_ref, k_ref, v_ref, o_ref, *, l):
    q = (q_ref[...] * (HEAD_DIM ** -0.5)).astype(BF16)
    kd = _dup_kv_heads(k_ref[...])
    vd = _dup_kv_heads(v_ref[...])
    for g in range(KV_HEADS):
        gl = slice(g * LANE, (g + 1) * LANE)
        _group_attention(q, g, kd[:, gl], vd[:, gl], sink_ref, l, None, o_ref)


def _ctx_attention(q, k, v, sink, *, l, batch, seq):
    return pl.pallas_call(
        functools.partial(_ctx_attn_kernel, l=l),
        grid=(batch,),
        in_specs=[
            pl.BlockSpec(memory_space=pltpu.SMEM),
            pl.BlockSpec((seq, ATTN_W), lambda b: (b, 0)),
            pl.BlockSpec((seq, KV_W), lambda b: (b, 0)),
            pl.BlockSpec((seq, KV_W), lambda b: (b, 0)),
        ],
        out_specs=pl.BlockSpec((seq, ATTN_W), lambda b: (b, 0)),
        out_shape=jax.ShapeDtypeStruct((batch * seq, ATTN_W), BF16),
        compiler_params=_params(("arbitrary",)),
        name="ctx_attention",
    )(sink, q, k, v)


def _lat_attn_kernel(sink_ref, q_ref, k_ref, v_ref, kc_ref, vc_ref, o_ref, kcd_ref, vcd_ref, *, l, seq):
    i = pl.program_id(1)

    @pl.when(i == 0)
    def _():
        kcd_ref[...] = _dup_kv_heads(kc_ref[...])
        vcd_ref[...] = _dup_kv_heads(vc_ref[...])

    nq = q_ref.shape[0]
    nloc = nq + 2 * WINDOW
    q0 = i * nq
    start = pl.multiple_of(jnp.clip(q0 - WINDOW, 0, seq - nloc), WINDOW)
    jpos = start + lax.broadcasted_iota(jnp.int32, (2 * nq, nloc), 1)
    row = lax.broadcasted_iota(jnp.int32, (2 * nq, nloc), 0)
    valid = jnp.abs(jpos - (q0 + row % nq)) <= WINDOW
    q = q_ref[...]
    for g in range(KV_HEADS):
        gl = slice(g * LANE, (g + 1) * LANE)
        kd = jnp.concatenate([k_ref[pl.ds(start, nloc), gl], kcd_ref[:, gl]], axis=0)
        vd = jnp.concatenate([v_ref[pl.ds(start, nloc), gl], vcd_ref[:, gl]], axis=0)
        _group_attention(q, g, kd, vd, sink_ref, l, valid, o_ref)


def _lat_attention(q, k, v, cache_k, cache_v, sink, *, l, batch, seq):
    nq = LAT_QUERIES
    nb = seq // nq
    past = cache_k.shape[2]
    kv_spec = pl.BlockSpec((seq, 2 * KV_W), lambda b, i: (b, 0))
    cache_spec = pl.BlockSpec((None, None, past, KV_W), lambda b, i: (b, l, 0, 0))
    return pl.pallas_call(
        functools.partial(_lat_attn_kernel, l=l, seq=seq),
        grid=(batch, nb),
        in_specs=[
            pl.BlockSpec(memory_space=pltpu.SMEM),
            pl.BlockSpec((nq, ATTN_W), lambda b, i: (b * nb + i, 0)),
            kv_spec, kv_spec, cache_spec, cache_spec,
        ],
        out_specs=pl.BlockSpec((nq, ATTN_W), lambda b, i: (b * nb + i, 0)),
        out_shape=jax.ShapeDtypeStruct((batch * seq, ATTN_W), BF16),
        scratch_shapes=[pltpu.VMEM((past, 2 * KV_W), BF16), pltpu.VMEM((past, 2 * KV_W), BF16)],
        compiler_params=_params(("arbitrary", "arbitrary")),
        name="lat_attention",
    )(sink, q, k, v, cache_k, cache_v)


def _rope_tables(seq):
    rows = seq // GRID_W
    row = jnp.repeat(jnp.arange(rows), GRID_W)
    col = jnp.arange(rows * GRID_W) % GRID_W
    nf = HEAD_DIM // 4
    inv = ROPE_BASE ** (-jnp.arange(nf, dtype=F32) / nf)
    ang_r = row.astype(F32)[:, None] * inv[None, :]
    ang_c = col.astype(F32)[:, None] * inv[None, :]
    cos_h = jnp.concatenate([jnp.cos(ang_r)] * 2 + [jnp.cos(ang_c)] * 2, axis=1)
    sin_h = jnp.concatenate([-jnp.sin(ang_r), jnp.sin(ang_r), -jnp.sin(ang_c), jnp.sin(ang_c)], axis=1)
    return jnp.tile(cos_h, (1, KV_HEADS)), jnp.tile(sin_h, (1, KV_HEADS))


def _conv_kernel(cv_ref, dw_ref, dwb_ref, lng_ref, lnb_ref, pw_ref, o_ref, pad_ref, *, seq):
    x = cv_ref[...]
    zeros = jnp.zeros((CONV_PAD, CONV_W), F32)
    pad_ref[0:CONV_PAD, :] = zeros
    pad_ref[CONV_PAD:CONV_PAD + seq, :] = x[:, :CONV_W] * jax.nn.sigmoid(x[:, CONV_W:])
    pad_ref[CONV_PAD + seq:2 * CONV_PAD + seq, :] = zeros
    dw = dw_ref[...]
    pw = pw_ref[...].astype(BF16)
    first = CONV_PAD - CONV_K // 2

    def chunk(c, carry):
        base = pl.multiple_of(c * CONV_CHUNK, CONV_CHUNK)
        win = pad_ref[pl.ds(base, CONV_CHUNK + 2 * CONV_PAD), :]
        acc = jnp.zeros((CONV_CHUNK, CONV_W), F32)
        for r in range(SUBLANE):
            offs = [first + k for k in range(CONV_K) if (first + k) % SUBLANE == r]
            if not offs:
                continue
            shifted = win if r == 0 else pltpu.roll(win, win.shape[0] - r, 0)
            for off in offs:
                acc = acc + shifted[off - r:off - r + CONV_CHUNK] * dw[off - first:off - first + 1]
        acc = acc + dwb_ref[...]
        mu = jnp.mean(acc, axis=-1, keepdims=True)
        cen = acc - mu
        var = jnp.mean(cen * cen, axis=-1, keepdims=True)
        y = cen * lax.rsqrt(var + EPS) * lng_ref[...] + lnb_ref[...]
        o_ref[pl.ds(base, CONV_CHUNK), :] = _dot(_silu(y).astype(BF16), pw).astype(BF16)
        return carry

    lax.fori_loop(0, seq // CONV_CHUNK, chunk, 0)


def _conformer_conv(cv, dw, dwb, lng, lnb, pw, *, l, batch, seq):
    kpad = 2 * CONV_PAD
    dw_p = jnp.pad(dw, ((0, 0), (0, kpad - CONV_K), (0, 0)))
    vec = lambda a: a.reshape(DEPTH, 1, CONV_W)
    vspec = pl.BlockSpec((None, 1, CONV_W), lambda b: (l, 0, 0))
    return pl.pallas_call(
        functools.partial(_conv_kernel, seq=seq),
        grid=(batch,),
        in_specs=[
            pl.BlockSpec((seq, 2 * CONV_W), lambda b: (b, 0)),
            pl.BlockSpec((None, kpad, CONV_W), lambda b: (l, 0, 0)),
            vspec, vspec, vspec,
            pl.BlockSpec((None, CONV_W, CONV_W), lambda b: (l, 0, 0)),
        ],
        out_specs=pl.BlockSpec((seq, CONV_W), lambda b: (b, 0)),
        out_shape=jax.ShapeDtypeStruct((batch * seq, CONV_W), BF16),
        scratch_shapes=[pltpu.VMEM((seq + 2 * CONV_PAD, CONV_W), F32)],
        compiler_params=_params(("arbitrary",)),
        name="conformer_conv",
    )(cv, dw_p, vec(dwb), vec(lng), vec(lnb), pw)


def _short_conv3(x, w, b):
    n = x.shape[0]
    row = lax.broadcasted_iota(jnp.int32, x.shape, 0)
    prev = jnp.where(row == 0, 0.0, pltpu.roll(x, 1, 0))
    nxt = jnp.where(row == n - 1, 0.0, pltpu.roll(x, n - 1, 0))
    return prev * w[0:1] + x * w[1:2] + nxt * w[2:3] + b


def _filter_kernel(z_ref, w1_ref, b1_ref, f1_ref, w2_ref, b2_ref, f2_ref, w3_ref, ld_ref, o_ref):
    z = z_ref[...]
    h = jnp.sin(f1_ref[...] * (_dot_hi(z, w1_ref[...]) + b1_ref[...]))
    h = jnp.sin(f2_ref[...] * (_dot_hi(h, w2_ref[...]) + b2_ref[...]))
    h = _dot_hi(h, w3_ref[...])
    tn = z[:, 0:1]
    h = h * jnp.exp(-tn * jnp.exp(ld_ref[...]))
    ss = jnp.sum(h * h, axis=0, keepdims=True)
    c = HYENA_W
    scale = []
    for o in range(2):
        tot = ss[:, 2 * o * c:(2 * o + 1) * c] + ss[:, (2 * o + 1) * c:(2 * o + 2) * c]
        r = lax.rsqrt(tot + EPS)
        scale += [r, r]
    h = h * jnp.concatenate(scale, axis=1)
    row = lax.broadcasted_iota(jnp.int32, h.shape, 0)
    col = lax.broadcasted_iota(jnp.int32, h.shape, 1)
    o_ref[...] = jnp.where((row == 0) & ((col // c) % 2 == 1), 0.0, h)


def _hyena_features(seq):
    t = jnp.arange(seq, dtype=F32)
    tn = t / (seq - 1)
    bands = jnp.linspace(1e-4, HYENA_BANDS - 1, HYENA_BANDS, dtype=F32)
    ang = 2.0 * math.pi * t[:, None] * bands[None, :] / seq
    z = jnp.concatenate([tn[:, None], jnp.cos(ang), -jnp.sin(ang)], axis=-1)
    return jnp.pad(z, ((0, 0), (0, LANE - HYENA_EMB)))


def _hyena_filters(z, w1, b1, f1, w2, b2, f2, w3, log_decay):
    seq = z.shape[0]
    n = w3.shape[1]
    return pl.pallas_call(
        _filter_kernel,
        out_shape=jax.ShapeDtypeStruct((seq, n), F32),
        compiler_params=_params(None, VMEM_LIMIT),
        name="hyena_filters",
    )(z, w1, b1, f1, w2, b2, f2, w3, log_decay)


def _spectrum_kernel(m_ref, h_ref, a_ref, b_ref, d_ref, *, n_fft):
    j = pl.program_id(0)
    fc = m_ref.shape[1]
    c = HYENA_W
    h16 = h_ref[...].astype(BF16)
    gc = _dot(m_ref[0], h16)
    gs = _dot(m_ref[1], h16)
    nyq = jnp.sum(_alternate_rows(h16.astype(F32)), axis=0, keepdims=True)
    row = j * fc + lax.broadcasted_iota(jnp.int32, (fc, c), 0)
    is0 = row == 0
    wgt = jnp.where(is0, 1.0 / n_fft, 2.0 / n_fft)
    for o in range(2):
        f = slice(2 * o * c, (2 * o + 1) * c)
        b = slice((2 * o + 1) * c, (2 * o + 2) * c)
        re = gc[:, f] + gc[:, b]
        a_ref[o] = re * wgt
        b_ref[o] = jnp.where(is0, 0.0, gs[:, f] - gs[:, b]) * wgt
        d_ref[o] = jnp.where(is0, nyq[:, f] + nyq[:, b], re) * wgt


def _filter_spectra(dft, hcat, *, fc):
    _, nf, seq = dft.shape
    c = HYENA_W
    out_spec = pl.BlockSpec((2, fc, c), lambda j: (0, j, 0))
    shape = jax.ShapeDtypeStruct((2, nf, c), F32)
    return pl.pallas_call(
        functools.partial(_spectrum_kernel, n_fft=2 * seq),
        grid=(nf // fc,),
        in_specs=[
            pl.BlockSpec((2, fc, seq), lambda j: (0, j, 0)),
            pl.BlockSpec(hcat.shape, lambda j: (0, 0)),
        ],
        out_specs=[out_spec, out_spec, out_spec],
        out_shape=[shape, shape, shape],
        compiler_params=_params(("arbitrary",), VMEM_LIMIT),
        name="hyena_spectra",
    )(dft, hcat)


def _alternate_rows(x):
    row = lax.broadcasted_iota(jnp.int32, x.shape, 0)
    return jnp.where(row % 2 == 0, x, -x)


def _longconv_seq(u16, m_ref, a_ref, b_ref, d0_ref):
    seq = u16.shape[0]
    fcs = min(seq, LONGCONV_FREQ_CHUNK)
    row0 = lax.broadcasted_iota(jnp.int32, (fcs, u16.shape[1]), 0) == 0
    y = None
    for k in range(seq // fcs):
        fs = slice(k * fcs, (k + 1) * fcs)
        a = a_ref[fs, :]
        b = b_ref[fs, :]
        ur = _dot(m_ref[0, fs, :], u16)
        ui = _dot(m_ref[1, fs, :], u16)
        d = a
        if k == 0:
            ui = jnp.where(row0, jnp.sum(_alternate_rows(u16.astype(F32)), axis=0, keepdims=True), ui)
            d = jnp.where(row0, d0_ref[...], a)
        yr = (ur * a - ui * b).astype(BF16)
        yi = (ur * b + ui * d).astype(BF16)
        part = _dot(m_ref[0, :, fs], yr) + _dot(m_ref[1, :, fs], yi)
        if k == 0:
            y = part + _alternate_rows(jnp.broadcast_to(yi[0:1].astype(F32), u16.shape))
        else:
            y = y + part
    return y


def _hyena_conv0_kernel(hv_ref, hx_ref, sw_ref, sb_ref, bias_ref, m_ref, a_ref, b_ref, d_ref,
                        y32_ref, y16_ref, *, seq):
    c = HYENA_W
    for bi in range(hv_ref.shape[0] // seq):
        rows = slice(bi * seq, (bi + 1) * seq)
        v = _short_conv3(hv_ref[rows, :], sw_ref[:, 0:c], sb_ref[:, 0:c])
        x1 = _short_conv3(hx_ref[rows, :], sw_ref[:, c:2 * c], sb_ref[:, c:2 * c])
        conv = _longconv_seq(v.astype(BF16), m_ref, a_ref, b_ref, d_ref)
        y = x1 * (conv + bias_ref[0:1] * v)
        y32_ref[rows, :] = y
        y16_ref[rows, :] = y.astype(BF16)


def _hyena_conv1_kernel(y16_ref, y32_ref, hx_ref, sw_ref, sb_ref, bias_ref, m_ref, a_ref, b_ref, d_ref,
                        o_ref, *, seq):
    c = HYENA_W
    for bi in range(y16_ref.shape[0] // seq):
        rows = slice(bi * seq, (bi + 1) * seq)
        x2 = _short_conv3(hx_ref[rows, :], sw_ref[:, 2 * c:3 * c], sb_ref[:, 2 * c:3 * c])
        conv = _longconv_seq(y16_ref[rows, :], m_ref, a_ref, b_ref, d_ref)
        o_ref[rows, :] = (x2 * (conv + bias_ref[1:2] * y32_ref[rows, :])).astype(BF16)


def _hyena_mixer(hy, sw, sb, bias, dft, spectra, *, l, batch, seq, bg):
    c = HYENA_W
    t = batch * seq
    rows = bg * seq
    sw_p = jnp.pad(sw, ((0, 0), (0, SUBLANE - sw.shape[1]), (0, 0)))
    sb3 = sb.reshape(DEPTH, 1, 3 * c)
    chan = lambda ch: pl.BlockSpec((rows, c), lambda g: (g, ch))
    sw_spec = pl.BlockSpec((None, SUBLANE, 3 * c), lambda g: (l, 0, 0))
    sb_spec = pl.BlockSpec((None, 1, 3 * c), lambda g: (l, 0, 0))
    bias_spec = pl.BlockSpec((None, 2, c), lambda g: (l, 0, 0))
    dft_spec = _resident(dft.shape, lambda g: (0, 0, 0))
    coef = lambda order: ([_resident((None, seq, c), lambda g: (order, 0, 0))] * 2
                          + [pl.BlockSpec((None, 1, c), lambda g: (order, 0, 0))])
    spec_a, spec_b, spec_d = spectra
    spectra = (spec_a, spec_b, spec_d[:, 0:1, :])
    params = _params(("arbitrary",), VMEM_LIMIT)

    y32, y16 = pl.pallas_call(
        functools.partial(_hyena_conv0_kernel, seq=seq),
        grid=(batch // bg,),
        in_specs=[_resident((rows, c), lambda g: (g, 0)), _resident((rows, c), lambda g: (g, 1)),
                  sw_spec, sb_spec, bias_spec, dft_spec] + coef(0),
        out_specs=[chan(0), chan(0)],
        out_shape=[jax.ShapeDtypeStruct((t, c), F32), jax.ShapeDtypeStruct((t, c), BF16)],
        compiler_params=params,
        name="hyena_conv0",
    )(hy, hy, sw_p, sb3, bias, dft, *spectra)

    return pl.pallas_call(
        functools.partial(_hyena_conv1_kernel, seq=seq),
        grid=(batch // bg,),
        in_specs=[chan(0), chan(0), chan(2), sw_spec, sb_spec, bias_spec, dft_spec] + coef(1),
        out_specs=chan(0),
        out_shape=jax.ShapeDtypeStruct((t, c), BF16),
        compiler_params=params,
        name="hyena_conv1",
    )(y16, y32, hy, sw_p, sb3, bias, dft, *spectra)


def _dft_matrix(seq):
    n = 2 * seq
    t = np.arange(seq, dtype=np.int64)
    if seq <= 256:
        k = (t[:, None] * t[None, :]) % n
        ang = 2.0 * np.pi * k.astype(np.float64) / n
        return jnp.asarray(np.stack([np.cos(ang), -np.sin(ang)]).astype(np.float32))
    f0n = 64
    f1n = seq // f0n
    ang1 = 2.0 * np.pi * ((np.arange(f1n)[:, None] * f0n * t[None, :]) % n).astype(np.float64) / n
    ang0 = 2.0 * np.pi * ((np.arange(f0n)[:, None] * t[None, :]) % n).astype(np.float64) / n
    c1 = jnp.asarray(np.cos(ang1), F32)[:, None, :]
    s1 = jnp.asarray(np.sin(ang1), F32)[:, None, :]
    c0 = jnp.asarray(np.cos(ang0), F32)[None, :, :]
    s0 = jnp.asarray(np.sin(ang0), F32)[None, :, :]
    mc = (c1 * c0 - s1 * s0).reshape(seq, seq)
    ms = (-(s1 * c0 + c1 * s0)).reshape(seq, seq)
    return jnp.stack([mc, ms])


def _mixffn_kernel(x_ref, att_ref, cvo_ref, hyo_ref, mod_ref, wo_ref, g_ref,
                   wg_ref, wu_ref, wd_ref, gf_ref, o_ref, a_ref, *, final):
    m = mod_ref[...]
    o1 = ATTN_W
    o2 = ATTN_W + CONV_W
    mix = (_dot(att_ref[...], wo_ref[0:o1, :]) + _dot(cvo_ref[...], wo_ref[o1:o2, :])
           + _dot(hyo_ref[...], wo_ref[o2:, :]))
    x = x_ref[...] + m[5:6] * mix
    h = _mod_norm(x, g_ref[...], m[7:8], m[6:7])
    y = _swiglu_into(h.astype(BF16), wg_ref, wu_ref, wd_ref, a_ref)
    x = x + (0.5 * m[8:9]) * y
    if final:
        x = x * lax.rsqrt(jnp.mean(x * x, axis=-1, keepdims=True) + EPS) * gf_ref[...]
    o_ref[...] = x


def _mixffn(x, att, cvo, hyo, mod4, w_out, gain, wg, wu, wd, g_final, *, l, row0, rows_per_cond, final):
    t, d = x.shape
    tm = TOKEN_TILE
    tok = lambda w: pl.BlockSpec((tm, w), lambda i: (i, 0))
    wspec_in = _resident((None, d, D_FF), lambda i: (l, 0, 0))
    return pl.pallas_call(
        functools.partial(_mixffn_kernel, final=final),
        grid=(t // tm,),
        in_specs=[
            tok(d), tok(ATTN_W), tok(CONV_W), tok(HYENA_W),
            _mod_spec(l, row0, rows_per_cond, tm),
            _resident((None, d, d), lambda i: (l, 0, 0)),
            pl.BlockSpec((None, 1, d), lambda i: (l, 0, 0)),
            wspec_in, wspec_in,
            _resident((None, D_FF, d), lambda i: (l, 0, 0)),
            pl.BlockSpec((1, d), lambda i: (0, 0)),
        ],
        out_specs=tok(d),
        out_shape=jax.ShapeDtypeStruct((t, d), F32),
        scratch_shapes=[pltpu.VMEM((tm, D_FF), BF16)],
        compiler_params=_params(("arbitrary",), VMEM_LIMIT),
        name="mixffn",
    )(x, att, cvo, hyo, mod4, w_out, gain.reshape(DEPTH, 1, d), wg, wu, wd, g_final.reshape(1, d))


def _pad_cols(a, width):
    return jnp.pad(a, [(0, 0)] * (a.ndim - 1) + [(0, width - a.shape[-1])])


def _trunk(x, cache, mod4, P, *, batch, seq, row0, rows_per_cond):
    fc = min(seq, FREQ_CHUNK)
    bg = max(1, min(batch, LONGCONV_ROWS // seq))
    dft = _dft_matrix(seq).astype(BF16)
    z = _hyena_features(seq)
    rope_tables = None if cache is None else _rope_tables(seq)
    w_in = P['w_in'] if cache is None else P['w_in_lat']
    cond = dict(row0=row0, rows_per_cond=rows_per_cond)
    ks, vs = [], []
    for l in range(DEPTH):
        x, q, k, v, cv, hy = _ffn_inproj(x, mod4, P['g_ffn1'], P['g_mix'], P['w1_gate'], P['w1_up'],
                                         P['w1_down'], w_in, rope_tables, l=l, seq=seq, **cond)
        if cache is None:
            att = _ctx_attention(q, k, v, P['attn_sink'], l=l, batch=batch, seq=seq)
            ks.append(k)
            vs.append(v)
        else:
            att = _lat_attention(q, k, v, cache[0], cache[1], P['attn_sink'], l=l, batch=batch, seq=seq)
        cvo = _conformer_conv(cv, P['conv_dw'], P['conv_dw_b'], P['conv_ln_g'], P['conv_ln_b'],
                              P['conv_pw'], l=l, batch=batch, seq=seq)
        hcat = _hyena_filters(z, P['hy_w1'][l], P['hy_b1'][l], P['hy_f1'][l], P['hy_w2'][l],
                              P['hy_b2'][l], P['hy_f2'][l], P['hy_w3'][l], P['hy_log_decay'][l])
        spectra = _filter_spectra(dft, hcat, fc=fc)
        hyo = _hyena_mixer(hy, P['hy_short_w'], P['hy_short_b'], P['hy_bias'], dft, spectra,
                           l=l, batch=batch, seq=seq, bg=bg)
        x = _mixffn(x, att, cvo, hyo, mod4, P['w_out'], P['g_ffn2'], P['w2_gate'], P['w2_up'],
                    P['w2_down'], P['g_final'], l=l, final=(l == DEPTH - 1), **cond)
    return x, ks, vs


def kernel(x_prompt, x_sample, c, cache_k, cache_v, c_ctx, w_mod, b_mod, g_ffn1, g_mix, g_ffn2,
           g_final, w1_gate, w1_up, w1_down, w2_gate, w2_up, w2_down, w_in, w_out, attn_sink,
           conv_dw, conv_dw_b, conv_ln_g, conv_ln_b, conv_pw, hy_short_w, hy_short_b,
           hy_w1, hy_b1, hy_f1, hy_w2, hy_b2, hy_f2, hy_w3, hy_log_decay, hy_bias):
    batch, seq, d = x_prompt.shape
    dec_batch, dec_seq, _ = x_sample.shape
    assert 1 + dec_batch <= COND_ROWS

    hid = LANE
    w_in16 = w_in.astype(BF16)
    o1, o2, o3 = ATTN_W, ATTN_W + KV_W, ATTN_W + 2 * KV_W

    def dup_heads(w):
        g0, g1 = w[..., :HEAD_DIM], w[..., HEAD_DIM:]
        return jnp.concatenate([g0, g0, g1, g1], axis=-1)

    w_in_lat = jnp.concatenate([w_in16[..., :o1], dup_heads(w_in16[..., o1:o2]),
                                dup_heads(w_in16[..., o2:o3]), w_in16[..., o3:]], axis=-1)
    P = dict(
        w_in_lat=w_in_lat,
        g_ffn1=g_ffn1, g_mix=g_mix, g_ffn2=g_ffn2, g_final=g_final, attn_sink=attn_sink,
        w1_gate=w1_gate.astype(BF16), w1_up=w1_up.astype(BF16), w1_down=w1_down.astype(BF16),
        w2_gate=w2_gate.astype(BF16), w2_up=w2_up.astype(BF16), w2_down=w2_down.astype(BF16),
        w_in=w_in16, w_out=w_out.astype(BF16),
        conv_dw=conv_dw, conv_dw_b=conv_dw_b, conv_ln_g=conv_ln_g, conv_ln_b=conv_ln_b, conv_pw=conv_pw,
        hy_short_w=hy_short_w, hy_short_b=hy_short_b,
        hy_w1=jnp.pad(hy_w1, ((0, 0), (0, hid - HYENA_EMB), (0, hid - HYENA_HID))),
        hy_b1=_pad_cols(hy_b1, hid)[:, None, :], hy_f1=_pad_cols(hy_f1, hid)[:, None, :],
        hy_w2=jnp.pad(hy_w2, ((0, 0), (0, hid - HYENA_HID), (0, hid - HYENA_HID))),
        hy_b2=_pad_cols(hy_b2, hid)[:, None, :], hy_f2=_pad_cols(hy_f2, hid)[:, None, :],
        hy_w3=jnp.pad(hy_w3, ((0, 0), (0, hid - HYENA_HID), (0, 0))),
        hy_log_decay=hy_log_decay[:, None, :], hy_bias=hy_bias,
    )

    conds = jnp.concatenate([c_ctx[None, :], c, jnp.zeros((COND_ROWS - 1 - dec_batch, d), F32)], axis=0)
    mod = _modulation(conds, w_mod, b_mod)
    mod4 = mod.reshape(DEPTH, COND_ROWS, N_MOD, d)

    xp, ks, vs = _trunk(x_prompt.reshape(batch * seq, d), None, mod4, P,
                        batch=batch, seq=seq, row0=0, rows_per_cond=batch * seq)
    cache = (cache_k.reshape(*cache_k.shape[:3], KV_W), cache_v.reshape(*cache_v.shape[:3], KV_W))
    xs, _, _ = _trunk(x_sample.reshape(dec_batch * dec_seq, d), cache, mod4, P,
                      batch=dec_batch, seq=dec_seq, row0=1, rows_per_cond=dec_seq)

    kv_shape = (batch, DEPTH, seq, KV_HEADS, HEAD_DIM)
    new_k = jnp.stack([k.reshape(batch, seq, KV_W) for k in ks], axis=1).reshape(kv_shape)
    new_v = jnp.stack([v.reshape(batch, seq, KV_W) for v in vs], axis=1).reshape(kv_shape)
    return (xp.reshape(batch, seq, d), xs.reshape(dec_batch, dec_seq, d), new_k, new_v)
```

```python
import functools
import math

import numpy as np
import jax
import jax.numpy as jnp
from jax import lax
from jax.experimental import pallas as pl
from jax.experimental.pallas import tpu as pltpu

F32 = jnp.float32
BF16 = jnp.bfloat16

D_MODEL = 1024
DEPTH = 2
GRID_W = 64
N_HEADS = 8
KV_HEADS = 2
HEAD_DIM = 64
Q_PER_KV = N_HEADS // KV_HEADS
ATTN_W = N_HEADS * HEAD_DIM
KV_W = KV_HEADS * HEAD_DIM
CONV_W = D_MODEL // 4
HYENA_W = D_MODEL // 4
WINDOW = 128
BLOCK = 128
CONV_K = 31
HYENA_EMB = 33
HYENA_BANDS = (HYENA_EMB - 1) // 2
HYENA_HID = 64
D_FF = 2816
N_MOD = 9
ROPE_BASE = 10000.0
EPS = 1e-6
NEG_INF = -1e30
LOG2_E = math.log2(math.e)
Q_SCALE = HEAD_DIM ** -0.5 * LOG2_E

LANE = 128
SUBLANE = 8
MXU_W = 256
COND_ROWS = 16
TOKEN_TILE = 512
CONV_CHUNK = 128
CONV_PAD = 16
DFT_MINOR = 64
FREQ_CHUNK = 512
LONGCONV_ROWS = 4096
LONGCONV_FREQ_CHUNK = 1024
LAT_QUERIES = 256
VMEM_LIMIT = 56 * 1024 * 1024


def _params(sem, vmem=None):
    return pltpu.CompilerParams(dimension_semantics=sem, vmem_limit_bytes=vmem)


def _silu(x):
    return x * jax.nn.sigmoid(x)


def _dot(a, b):
    return jnp.dot(a, b, preferred_element_type=F32)


def _dot_nt(a, b):
    return lax.dot_general(a, b, (((1,), (1,)), ((), ())), preferred_element_type=F32)


def _dot_hi(a, b):
    return jnp.dot(a, b, preferred_element_type=F32, precision=lax.Precision.HIGHEST)


def _mod_norm(x, gain, scale, shift):
    y = x * lax.rsqrt(jnp.mean(x * x, axis=-1, keepdims=True) + EPS)
    return (y * gain) * (1.0 + scale) + shift


def _mod_kernel(c_ref, w_ref, b_ref, o_ref):
    s = _silu(c_ref[...]).astype(BF16)
    o_ref[...] = _dot(s, w_ref[...].astype(BF16)) + b_ref[...]


def _modulation(conds, w_mod, b_mod):
    d = D_MODEL
    return pl.pallas_call(
        _mod_kernel,
        grid=(DEPTH, N_MOD),
        in_specs=[
            pl.BlockSpec((COND_ROWS, d), lambda l, j: (0, 0)),
            pl.BlockSpec((None, d, d), lambda l, j: (l, 0, j)),
            pl.BlockSpec((None, 1, d), lambda l, j: (l, 0, j)),
        ],
        out_specs=pl.BlockSpec((None, COND_ROWS, d), lambda l, j: (l, 0, j)),
        out_shape=jax.ShapeDtypeStruct((DEPTH, COND_ROWS, N_MOD * d), F32),
        compiler_params=_params(("arbitrary", "arbitrary")),
        name="modulation",
    )(conds, w_mod, b_mod.reshape(DEPTH, 1, N_MOD * d))


def _swiglu_into(h16, wg_ref, wu_ref, wd_ref, a_ref):
    for c in range(D_FF // MXU_W):
        sl = slice(c * MXU_W, (c + 1) * MXU_W)
        g = _dot(h16, wg_ref[:, sl])
        u = _dot(h16, wu_ref[:, sl])
        a_ref[:, sl] = (_silu(g) * u).astype(BF16)
    return _dot(a_ref[...], wd_ref[...])


def _ffn1_then_norm(x_ref, mod_ref, g1_ref, g2_ref, wg_ref, wu_ref, wd_ref, o_ref, a_ref):
    x = x_ref[...]
    m = mod_ref[...]
    h = _mod_norm(x, g1_ref[...], m[1:2], m[0:1])
    y = _swiglu_into(h.astype(BF16), wg_ref, wu_ref, wd_ref, a_ref)
    x = x + (0.5 * m[2:3]) * y
    o_ref[...] = x
    return _mod_norm(x, g2_ref[...], m[4:5], m[3:4]).astype(BF16)


def _resident(shape, index_map):
    return pl.BlockSpec(shape, index_map, pipeline_mode=pl.Buffered(1))


def _mod_spec(l, row0, rows_per_cond, tm):
    return pl.BlockSpec((None, None, N_MOD, D_MODEL),
                        lambda i: (l, row0 + (i * tm) // rows_per_cond, 0, 0))


_IN_SPLITS = (ATTN_W, KV_W, KV_W, 2 * CONV_W, 3 * HYENA_W)
_IN_SPLITS_LAT = (ATTN_W, 2 * KV_W, 2 * KV_W, 2 * CONV_W, 3 * HYENA_W)


def _rope(x, cos, sin_signed):
    w = x.shape[-1]
    lane = lax.broadcasted_iota(jnp.int32, x.shape, 1)
    quarter = HEAD_DIM // 4
    partner = jnp.where(lane % (2 * quarter) < quarter,
                        pltpu.roll(x, w - quarter, 1), pltpu.roll(x, quarter, 1))
    return x * cos + partner * sin_signed


def _ffn_inproj_kernel(x_ref, mod_ref, g1_ref, g2_ref, wg_ref, wu_ref, wd_ref, w_ref,
                       o_ref, q_ref, k_ref, v_ref, cv_ref, hy_ref, a_ref):
    h = _ffn1_then_norm(x_ref, mod_ref, g1_ref, g2_ref, wg_ref, wu_ref, wd_ref, o_ref, a_ref)
    off = 0
    for ref, width in zip((q_ref, k_ref, v_ref, cv_ref, hy_ref), _IN_SPLITS):
        ref[...] = _dot(h, w_ref[:, off:off + width])
        off += width


def _ffn_inproj_lat_kernel(x_ref, mod_ref, g1_ref, g2_ref, wg_ref, wu_ref, wd_ref, w_ref, cos_ref, sin_ref,
                           o_ref, q_ref, k_ref, v_ref, cv_ref, hy_ref, a_ref):
    h = _ffn1_then_norm(x_ref, mod_ref, g1_ref, g2_ref, wg_ref, wu_ref, wd_ref, o_ref, a_ref)
    cos = cos_ref[...]
    sin = sin_ref[...]
    wq, wk, wv, wc, wh = _IN_SPLITS
    tile = lambda a, width: jnp.concatenate([a] * (width // a.shape[1]), axis=1)
    q = _rope(_dot(h, w_ref[:, 0:wq]), tile(cos, wq), tile(sin, wq))
    q_ref[...] = (q * Q_SCALE).astype(BF16)
    off = wq
    k_ref[...] = _dup_kv_heads(_rope(_dot(h, w_ref[:, off:off + wk]), cos, sin))
    off += wk
    v_ref[...] = _dup_kv_heads(_dot(h, w_ref[:, off:off + wv]))
    off += wv
    cv_ref[...] = _dot(h, w_ref[:, off:off + wc])
    off += wc
    hy_ref[...] = _dot(h, w_ref[:, off:off + wh])


def _ffn_inproj(x, mod4, g_ffn, g_mix, wg, wu, wd, w_in, rope_tables, *, l, row0, rows_per_cond, seq):
    t, d = x.shape
    tm = TOKEN_TILE
    lat = rope_tables is not None
    splits = _IN_SPLITS_LAT if lat else _IN_SPLITS
    dtypes = (BF16, BF16, BF16, F32, F32) if lat else (F32,) * 5
    gain_spec = pl.BlockSpec((None, 1, d), lambda i: (l, 0, 0))
    wspec_in = _resident((None, d, D_FF), lambda i: (l, 0, 0))
    in_specs = [
        pl.BlockSpec((tm, d), lambda i: (i, 0)),
        _mod_spec(l, row0, rows_per_cond, tm),
        gain_spec, gain_spec,
        wspec_in, wspec_in,
        _resident((None, D_FF, d), lambda i: (l, 0, 0)),
        _resident((None, d, sum(_IN_SPLITS)), lambda i: (l, 0, 0)),
    ]
    args = [x, mod4, g_ffn.reshape(DEPTH, 1, d), g_mix.reshape(DEPTH, 1, d), wg, wu, wd, w_in]
    if lat:
        table_spec = pl.BlockSpec((tm, LANE), lambda i: (i % (seq // tm), 0))
        in_specs += [table_spec, table_spec]
        args += list(rope_tables)
    tok = lambda w: pl.BlockSpec((tm, w), lambda i: (i, 0))
    return pl.pallas_call(
        _ffn_inproj_lat_kernel if lat else _ffn_inproj_kernel,
        grid=(t // tm,),
        in_specs=in_specs,
        out_specs=[tok(d)] + [tok(w) for w in splits],
        out_shape=[jax.ShapeDtypeStruct((t, d), F32)]
        + [jax.ShapeDtypeStruct((t, w), dt) for w, dt in zip(splits, dtypes)],
        scratch_shapes=[pltpu.VMEM((tm, D_FF), BF16)],
        compiler_params=_params(("arbitrary",), VMEM_LIMIT),
        name="ffn_inproj",
    )(*args)


def _dup_kv_heads(x):
    swapped = pltpu.roll(x, HEAD_DIM, 1)
    low = lax.broadcasted_iota(jnp.int32, x.shape, 1) < HEAD_DIM
    return jnp.concatenate([jnp.where(low, x, swapped), jnp.where(low, swapped, x)], axis=1).astype(BF16)


def _group_attention(q, g, kd, vd, sink_ref, l, valid, o_ref):
    n = q.shape[0]
    c0 = 2 * g * LANE
    lhs = jnp.concatenate([q[:, c0:c0 + LANE], q[:, c0 + LANE:c0 + 2 * LANE]], axis=0)
    top = lax.broadcasted_iota(jnp.int32, (2 * n, 1), 0) < n
    low = lax.broadcasted_iota(jnp.int32, (2 * n, LANE), 1) < HEAD_DIM
    zero = jnp.zeros((2 * n, LANE), BF16)
    outs = []
    for half in range(2):
        s = _dot_nt(jnp.where(low, lhs, zero) if half == 0 else jnp.where(low, zero, lhs), kd)
        if valid is not None:
            nloc = valid.shape[1]
            s = jnp.concatenate([jnp.where(valid, s[:, :nloc], NEG_INF), s[:, nloc:]], axis=1)
        sk = jnp.where(top, sink_ref[l, 4 * g + half], sink_ref[l, 4 * g + 2 + half]) * LOG2_E
        mx = jnp.maximum(jnp.max(s, axis=-1, keepdims=True), sk)
        p = jnp.exp2(s - mx)
        den = jnp.sum(p, axis=-1, keepdims=True) + jnp.exp2(sk - mx)
        outs.append(_dot(p.astype(BF16), vd) / den)
    o = jnp.where(low, outs[0], outs[1]).astype(BF16)
    o_ref[:, c0:c0 + LANE] = o[:n]
    o_ref[:, c0 + LANE:c0 + 2 * LANE] = o[n:]


def _ctx_attn_kernel(sink_ref, q_ref, k_ref, v_ref, o_ref, *, l):
    q = (q_ref[...] * Q_SCALE).astype(BF16)
    kd = _dup_kv_heads(k_ref[...])
    vd = _dup_kv_heads(v_ref[...])
    for g in range(KV_HEADS):
        gl = slice(g * LANE, (g + 1) * LANE)
        _group_attention(q, g, kd[:, gl], vd[:, gl], sink_ref, l, None, o_ref)


def _ctx_attention(q, k, v, sink, *, l, batch, seq):
    return pl.pallas_call(
        functools.partial(_ctx_attn_kernel, l=l),
        grid=(batch,),
        in_specs=[
            pl.BlockSpec(memory_space=pltpu.SMEM),
            pl.BlockSpec((seq, ATTN_W), lambda b: (b, 0)),
            pl.BlockSpec((seq, KV_W), lambda b: (b, 0)),
            pl.BlockSpec((seq, KV_W), lambda b: (b, 0)),
        ],
        out_specs=pl.BlockSpec((seq, ATTN_W), lambda b: (b, 0)),
        out_shape=jax.ShapeDtypeStruct((batch * seq, ATTN_W), BF16),
        compiler_params=_params(("arbitrary",)),
        name="ctx_attention",
    )(sink, q, k, v)


def _lat_attn_kernel(sink_ref, q_ref, k_ref, v_ref, kc_ref, vc_ref, o_ref, kcd_ref, vcd_ref, *, l, seq):
    i = pl.program_id(1)

    @pl.when(i == 0)
    def _():
        kcd_ref[...] = _dup_kv_heads(kc_ref[...])
        vcd_ref[...] = _dup_kv_heads(vc_ref[...])

    nq = q_ref.shape[0]
    nloc = nq + 2 * WINDOW
    q0 = i * nq
    start = pl.multiple_of(jnp.clip(q0 - WINDOW, 0, seq - nloc), WINDOW)
    jpos = start + lax.broadcasted_iota(jnp.int32, (2 * nq, nloc), 1)
    row = lax.broadcasted_iota(jnp.int32, (2 * nq, nloc), 0)
    valid = jnp.abs(jpos - (q0 + row % nq)) <= WINDOW
    q = q_ref[...]
    for g in range(KV_HEADS):
        gl = slice(g * LANE, (g + 1) * LANE)
        kd = jnp.concatenate([k_ref[pl.ds(start, nloc), gl], kcd_ref[:, gl]], axis=0)
        vd = jnp.concatenate([v_ref[pl.ds(start, nloc), gl], vcd_ref[:, gl]], axis=0)
        _group_attention(q, g, kd, vd, sink_ref, l, valid, o_ref)


def _lat_attention(q, k, v, cache_k, cache_v, sink, *, l, batch, seq):
    nq = LAT_QUERIES
    nb = seq // nq
    past = cache_k.shape[2]
    kv_spec = pl.BlockSpec((seq, 2 * KV_W), lambda b, i: (b, 0))
    cache_spec = pl.BlockSpec((None, None, past, KV_W), lambda b, i: (b, l, 0, 0))
    return pl.pallas_call(
        functools.partial(_lat_attn_kernel, l=l, seq=seq),
        grid=(batch, nb),
        in_specs=[
            pl.BlockSpec(memory_space=pltpu.SMEM),
            pl.BlockSpec((nq, ATTN_W), lambda b, i: (b * nb + i, 0)),
            kv_spec, kv_spec, cache_spec, cache_spec,
        ],
        out_specs=pl.BlockSpec((nq, ATTN_W), lambda b, i: (b * nb + i, 0)),
        out_shape=jax.ShapeDtypeStruct((batch * seq, ATTN_W), BF16),
        scratch_shapes=[pltpu.VMEM((past, 2 * KV_W), BF16), pltpu.VMEM((past, 2 * KV_W), BF16)],
        compiler_params=_params(("arbitrary", "arbitrary")),
        name="lat_attention",
    )(sink, q, k, v, cache_k, cache_v)


def _rope_tables(seq):
    rows = seq // GRID_W
    row = jnp.repeat(jnp.arange(rows), GRID_W)
    col = jnp.arange(rows * GRID_W) % GRID_W
    nf = HEAD_DIM // 4
    inv = ROPE_BASE ** (-jnp.arange(nf, dtype=F32) / nf)
    ang_r = row.astype(F32)[:, None] * inv[None, :]
    ang_c = col.astype(F32)[:, None] * inv[None, :]
    cos_h = jnp.concatenate([jnp.cos(ang_r)] * 2 + [jnp.cos(ang_c)] * 2, axis=1)
    sin_h = jnp.concatenate([-jnp.sin(ang_r), jnp.sin(ang_r), -jnp.sin(ang_c), jnp.sin(ang_c)], axis=1)
    return jnp.tile(cos_h, (1, KV_HEADS)), jnp.tile(sin_h, (1, KV_HEADS))


def _conv_kernel(cv_ref, dw_ref, dwb_ref, lng_ref, lnb_ref, pw_ref, o_ref, pad_ref, *, seq):
    x = cv_ref[...]
    zeros = jnp.zeros((CONV_PAD, CONV_W), F32)
    pad_ref[0:CONV_PAD, :] = zeros
    pad_ref[CONV_PAD:CONV_PAD + seq, :] = x[:, :CONV_W] * jax.nn.sigmoid(x[:, CONV_W:])
    pad_ref[CONV_PAD + seq:2 * CONV_PAD + seq, :] = zeros
    dw = dw_ref[...]
    pw = pw_ref[...].astype(BF16)
    first = CONV_PAD - CONV_K // 2

    def chunk(c, carry):
        base = pl.multiple_of(c * CONV_CHUNK, CONV_CHUNK)
        win = pad_ref[pl.ds(base, CONV_CHUNK + 2 * CONV_PAD), :]
        acc = jnp.zeros((CONV_CHUNK, CONV_W), F32)
        for r in range(SUBLANE):
            offs = [first + k for k in range(CONV_K) if (first + k) % SUBLANE == r]
            if not offs:
                continue
            shifted = win if r == 0 else pltpu.roll(win, win.shape[0] - r, 0)
            for off in offs:
                acc = acc + shifted[off - r:off - r + CONV_CHUNK] * dw[off - first:off - first + 1]
        acc = acc + dwb_ref[...]
        mu = jnp.mean(acc, axis=-1, keepdims=True)
        cen = acc - mu
        var = jnp.mean(cen * cen, axis=-1, keepdims=True)
        y = cen * lax.rsqrt(var + EPS) * lng_ref[...] + lnb_ref[...]
        o_ref[pl.ds(base, CONV_CHUNK), :] = _dot(_silu(y).astype(BF16), pw).astype(BF16)
        return carry

    lax.fori_loop(0, seq // CONV_CHUNK, chunk, 0)


def _conformer_conv(cv, dw, dwb, lng, lnb, pw, *, l, batch, seq):
    kpad = 2 * CONV_PAD
    dw_p = jnp.pad(dw, ((0, 0), (0, kpad - CONV_K), (0, 0)))
    vec = lambda a: a.reshape(DEPTH, 1, CONV_W)
    vspec = pl.BlockSpec((None, 1, CONV_W), lambda b: (l, 0, 0))
    return pl.pallas_call(
        functools.partial(_conv_kernel, seq=seq),
        grid=(batch,),
        in_specs=[
            pl.BlockSpec((seq, 2 * CONV_W), lambda b: (b, 0)),
            pl.BlockSpec((None, kpad, CONV_W), lambda b: (l, 0, 0)),
            vspec, vspec, vspec,
            pl.BlockSpec((None, CONV_W, CONV_W), lambda b: (l, 0, 0)),
        ],
        out_specs=pl.BlockSpec((seq, CONV_W), lambda b: (b, 0)),
        out_shape=jax.ShapeDtypeStruct((batch * seq, CONV_W), BF16),
        scratch_shapes=[pltpu.VMEM((seq + 2 * CONV_PAD, CONV_W), F32)],
        compiler_params=_params(("arbitrary",)),
        name="conformer_conv",
    )(cv, dw_p, vec(dwb), vec(lng), vec(lnb), pw)


def _short_conv3(x, w, b):
    n = x.shape[0]
    row = lax.broadcasted_iota(jnp.int32, x.shape, 0)
    prev = jnp.where(row == 0, 0.0, pltpu.roll(x, 1, 0))
    nxt = jnp.where(row == n - 1, 0.0, pltpu.roll(x, n - 1, 0))
    return prev * w[0:1] + x * w[1:2] + nxt * w[2:3] + b


def _filter_kernel(z_ref, w1_ref, b1_ref, f1_ref, w2_ref, b2_ref, f2_ref, w3_ref, ld_ref, o_ref):
    z = z_ref[...]
    h = jnp.sin(f1_ref[...] * (_dot_hi(z, w1_ref[...]) + b1_ref[...]))
    h = jnp.sin(f2_ref[...] * (_dot_hi(h, w2_ref[...]) + b2_ref[...]))
    h = _dot_hi(h, w3_ref[...])
    tn = z[:, 0:1]
    h = h * jnp.exp(-tn * jnp.exp(ld_ref[...]))
    ss = jnp.sum(h * h, axis=0, keepdims=True)
    c = HYENA_W
    scale = []
    for o in range(2):
        tot = ss[:, 2 * o * c:(2 * o + 1) * c] + ss[:, (2 * o + 1) * c:(2 * o + 2) * c]
        r = lax.rsqrt(tot + EPS)
        scale += [r, r]
    h = h * jnp.concatenate(scale, axis=1)
    row = lax.broadcasted_iota(jnp.int32, h.shape, 0)
    col = lax.broadcasted_iota(jnp.int32, h.shape, 1)
    o_ref[...] = jnp.where((row == 0) & ((col // c) % 2 == 1), 0.0, h)


def _hyena_features(seq):
    t = jnp.arange(seq, dtype=F32)
    tn = t / (seq - 1)
    bands = jnp.linspace(1e-4, HYENA_BANDS - 1, HYENA_BANDS, dtype=F32)
    ang = 2.0 * math.pi * t[:, None] * bands[None, :] / seq
    z = jnp.concatenate([tn[:, None], jnp.cos(ang), -jnp.sin(ang)], axis=-1)
    return jnp.pad(z, ((0, 0), (0, LANE - HYENA_EMB)))


def _hyena_filters(z, w1, b1, f1, w2, b2, f2, w3, log_decay):
    seq = z.shape[0]
    n = w3.shape[1]
    return pl.pallas_call(
        _filter_kernel,
        out_shape=jax.ShapeDtypeStruct((seq, n), F32),
        compiler_params=_params(None, VMEM_LIMIT),
        name="hyena_filters",
    )(z, w1, b1, f1, w2, b2, f2, w3, log_decay)


def _spectrum_kernel(m_ref, h_ref, a_ref, b_ref, d_ref, *, n_fft):
    j = pl.program_id(0)
    fc = m_ref.shape[1]
    c = HYENA_W
    h16 = h_ref[...].astype(BF16)
    gc = _dot(m_ref[0], h16)
    gs = _dot(m_ref[1], h16)
    nyq = jnp.sum(_alternate_rows(h16.astype(F32)), axis=0, keepdims=True)
    row = j * fc + lax.broadcasted_iota(jnp.int32, (fc, c), 0)
    is0 = row == 0
    wgt = jnp.where(is0, 1.0 / n_fft, 2.0 / n_fft)
    for o in range(2):
        f = slice(2 * o * c, (2 * o + 1) * c)
        b = slice((2 * o + 1) * c, (2 * o + 2) * c)
        re = gc[:, f] + gc[:, b]
        a_ref[o] = re * wgt
        b_ref[o] = jnp.where(is0, 0.0, gs[:, f] - gs[:, b]) * wgt
        d_ref[o] = jnp.where(is0, nyq[:, f] + nyq[:, b], re) * wgt


def _filter_spectra(dft, hcat, *, fc):
    _, nf, seq = dft.shape
    c = HYENA_W
    out_spec = pl.BlockSpec((2, fc, c), lambda j: (0, j, 0))
    shape = jax.ShapeDtypeStruct((2, nf, c), F32)
    return pl.pallas_call(
        functools.partial(_spectrum_kernel, n_fft=2 * seq),
        grid=(nf // fc,),
        in_specs=[
            pl.BlockSpec((2, fc, seq), lambda j: (0, j, 0)),
            pl.BlockSpec(hcat.shape, lambda j: (0, 0)),
        ],
        out_specs=[out_spec, out_spec, out_spec],
        out_shape=[shape, shape, shape],
        compiler_params=_params(("arbitrary",), VMEM_LIMIT),
        name="hyena_spectra",
    )(dft, hcat)


def _alternate_rows(x):
    row = lax.broadcasted_iota(jnp.int32, x.shape, 0)
    return jnp.where(row % 2 == 0, x, -x)


def _longconv_seq(u16, m_ref, a_ref, b_ref, d0_ref):
    seq = u16.shape[0]
    fcs = min(seq, LONGCONV_FREQ_CHUNK)
    row0 = lax.broadcasted_iota(jnp.int32, (fcs, u16.shape[1]), 0) == 0
    y = None
    for k in range(seq // fcs):
        fs = slice(k * fcs, (k + 1) * fcs)
        a = a_ref[fs, :]
        b = b_ref[fs, :]
        ur = _dot(m_ref[0, fs, :], u16)
        ui = _dot(m_ref[1, fs, :], u16)
        d = a
        if k == 0:
            ui = jnp.where(row0, jnp.sum(_alternate_rows(u16.astype(F32)), axis=0, keepdims=True), ui)
            d = jnp.where(row0, d0_ref[...], a)
        yr = (ur * a - ui * b).astype(BF16)
        yi = (ur * b + ui * d).astype(BF16)
        part = _dot(m_ref[0, :, fs], yr) + _dot(m_ref[1, :, fs], yi)
        if k == 0:
            y = part + _alternate_rows(jnp.broadcast_to(yi[0:1].astype(F32), u16.shape))
        else:
            y = y + part
    return y


def _hyena_conv0_kernel(hv_ref, hx_ref, sw_ref, sb_ref, bias_ref, m_ref, a_ref, b_ref, d_ref,
                        y32_ref, y16_ref, *, seq):
    c = HYENA_W
    for bi in range(hv_ref.shape[0] // seq):
        rows = slice(bi * seq, (bi + 1) * seq)
        v = _short_conv3(hv_ref[rows, :], sw_ref[:, 0:c], sb_ref[:, 0:c])
        x1 = _short_conv3(hx_ref[rows, :], sw_ref[:, c:2 * c], sb_ref[:, c:2 * c])
        conv = _longconv_seq(v.astype(BF16), m_ref, a_ref, b_ref, d_ref)
        y = x1 * (conv + bias_ref[0:1] * v)
        y32_ref[rows, :] = y
        y16_ref[rows, :] = y.astype(BF16)


def _hyena_conv1_kernel(y16_ref, y32_ref, hx_ref, sw_ref, sb_ref, bias_ref, m_ref, a_ref, b_ref, d_ref,
                        o_ref, *, seq):
    c = HYENA_W
    for bi in range(y16_ref.shape[0] // seq):
        rows = slice(bi * seq, (bi + 1) * seq)
        x2 = _short_conv3(hx_ref[rows, :], sw_ref[:, 2 * c:3 * c], sb_ref[:, 2 * c:3 * c])
        conv = _longconv_seq(y16_ref[rows, :], m_ref, a_ref, b_ref, d_ref)
        o_ref[rows, :] = (x2 * (conv + bias_ref[1:2] * y32_ref[rows, :])).astype(BF16)


def _hyena_mixer(hy, sw, sb, bias, dft, spectra, *, l, batch, seq, bg):
    c = HYENA_W
    t = batch * seq
    rows = bg * seq
    sw_p = jnp.pad(sw, ((0, 0), (0, SUBLANE - sw.shape[1]), (0, 0)))
    sb3 = sb.reshape(DEPTH, 1, 3 * c)
    chan = lambda ch: pl.BlockSpec((rows, c), lambda g: (g, ch))
    sw_spec = pl.BlockSpec((None, SUBLANE, 3 * c), lambda g: (l, 0, 0))
    sb_spec = pl.BlockSpec((None, 1, 3 * c), lambda g: (l, 0, 0))
    bias_spec = pl.BlockSpec((None, 2, c), lambda g: (l, 0, 0))
    dft_spec = _resident(dft.shape, lambda g: (0, 0, 0))
    coef = lambda order: ([_resident((None, seq, c), lambda g: (order, 0, 0))] * 2
                          + [pl.BlockSpec((None, 1, c), lambda g: (order, 0, 0))])
    spec_a, spec_b, spec_d = spectra
    spectra = (spec_a, spec_b, spec_d[:, 0:1, :])
    params = _params(("arbitrary",), VMEM_LIMIT)

    y32, y16 = pl.pallas_call(
        functools.partial(_hyena_conv0_kernel, seq=seq),
        grid=(batch // bg,),
        in_specs=[_resident((rows, c), lambda g: (g, 0)), _resident((rows, c), lambda g: (g, 1)),
                  sw_spec, sb_spec, bias_spec, dft_spec] + coef(0),
        out_specs=[chan(0), chan(0)],
        out_shape=[jax.ShapeDtypeStruct((t, c), F32), jax.ShapeDtypeStruct((t, c), BF16)],
        compiler_params=params,
        name="hyena_conv0",
    )(hy, hy, sw_p, sb3, bias, dft, *spectra)

    return pl.pallas_call(
        functools.partial(_hyena_conv1_kernel, seq=seq),
        grid=(batch // bg,),
        in_specs=[chan(0), chan(0), chan(2), sw_spec, sb_spec, bias_spec, dft_spec] + coef(1),
        out_specs=chan(0),
        out_shape=jax.ShapeDtypeStruct((t, c), BF16),
        compiler_params=params,
        name="hyena_conv1",
    )(y16, y32, hy, sw_p, sb3, bias, dft, *spectra)


def _dft_kernel(c1_ref, s1_ref, c0_ref, s0_ref, o_ref):
    c0 = c0_ref[...]
    s0 = s0_ref[...]
    for i in range(c1_ref.shape[0]):
        c1 = c1_ref[i:i + 1, :]
        s1 = s1_ref[i:i + 1, :]
        rows = slice(i * DFT_MINOR, (i + 1) * DFT_MINOR)
        o_ref[0, rows, :] = (c1 * c0 - s1 * s0).astype(BF16)
        o_ref[1, rows, :] = (-(s1 * c0 + c1 * s0)).astype(BF16)


def _dft_matrix(seq):
    n = 2 * seq
    t = np.arange(seq, dtype=np.int64)
    f1n = seq // DFT_MINOR
    ang1 = 2.0 * np.pi * ((np.arange(f1n)[:, None] * DFT_MINOR * t[None, :]) % n).astype(np.float64) / n
    ang0 = 2.0 * np.pi * ((np.arange(DFT_MINOR)[:, None] * t[None, :]) % n).astype(np.float64) / n
    tables = [jnp.asarray(f(a), F32) for a in (ang1, ang0) for f in (np.cos, np.sin)]
    nb = min(f1n, SUBLANE)
    coarse = pl.BlockSpec((nb, seq), lambda j: (j, 0))
    fine = pl.BlockSpec((DFT_MINOR, seq), lambda j: (0, 0))
    return pl.pallas_call(
        _dft_kernel,
        grid=(f1n // nb,),
        in_specs=[coarse, coarse, fine, fine],
        out_specs=pl.BlockSpec((2, nb * DFT_MINOR, seq), lambda j: (0, j, 0)),
        out_shape=jax.ShapeDtypeStruct((2, seq, seq), BF16),
        compiler_params=_params(("arbitrary",)),
        name="dft_matrix",
    )(*tables)


def _mixffn_kernel(x_ref, att_ref, cvo_ref, hyo_ref, mod_ref, wo_ref, g_ref,
                   wg_ref, wu_ref, wd_ref, gf_ref, o_ref, a_ref, *, final):
    m = mod_ref[...]
    o1 = ATTN_W
    o2 = ATTN_W + CONV_W
    mix = (_dot(att_ref[...], wo_ref[0:o1, :]) + _dot(cvo_ref[...], wo_ref[o1:o2, :])
           + _dot(hyo_ref[...], wo_ref[o2:, :]))
    x = x_ref[...] + m[5:6] * mix
    h = _mod_norm(x, g_ref[...], m[7:8], m[6:7])
    y = _swiglu_into(h.astype(BF16), wg_ref, wu_ref, wd_ref, a_ref)
    x = x + (0.5 * m[8:9]) * y
    if final:
        x = x * lax.rsqrt(jnp.mean(x * x, axis=-1, keepdims=True) + EPS) * gf_ref[...]
    o_ref[...] = x


def _mixffn(x, att, cvo, hyo, mod4, w_out, gain, wg, wu, wd, g_final, *, l, row0, rows_per_cond, final):
    t, d = x.shape
    tm = TOKEN_TILE
    tok = lambda w: pl.BlockSpec((tm, w), lambda i: (i, 0))
    wspec_in = _resident((None, d, D_FF), lambda i: (l, 0, 0))
    return pl.pallas_call(
        functools.partial(_mixffn_kernel, final=final),
        grid=(t // tm,),
        in_specs=[
            tok(d), tok(ATTN_W), tok(CONV_W), tok(HYENA_W),
            _mod_spec(l, row0, rows_per_cond, tm),
            _resident((None, d, d), lambda i: (l, 0, 0)),
            pl.BlockSpec((None, 1, d), lambda i: (l, 0, 0)),
            wspec_in, wspec_in,
            _resident((None, D_FF, d), lambda i: (l, 0, 0)),
            pl.BlockSpec((1, d), lambda i: (0, 0)),
        ],
        out_specs=tok(d),
        out_shape=jax.ShapeDtypeStruct((t, d), F32),
        scratch_shapes=[pltpu.VMEM((tm, D_FF), BF16)],
        compiler_params=_params(("arbitrary",), VMEM_LIMIT),
        name="mixffn",
    )(x, att, cvo, hyo, mod4, w_out, gain.reshape(DEPTH, 1, d), wg, wu, wd, g_final.reshape(1, d))


def _pad_cols(a, width):
    return jnp.pad(a, [(0, 0)] * (a.ndim - 1) + [(0, width - a.shape[-1])])


def _trunk(x, cache, mod4, P, *, batch, seq, row0, rows_per_cond):
    fc = min(seq, FREQ_CHUNK)
    bg = max(1, min(batch, LONGCONV_ROWS // seq))
    dft = _dft_matrix(seq)
    z = _hyena_features(seq)
    rope_tables = None if cache is None else _rope_tables(seq)
    w_in = P['w_in']
    cond = dict(row0=row0, rows_per_cond=rows_per_cond)
    ks, vs = [], []
    for l in range(DEPTH):
        x, q, k, v, cv, hy = _ffn_inproj(x, mod4, P['g_ffn1'], P['g_mix'], P['w1_gate'], P['w1_up'],
                                         P['w1_down'], w_in, rope_tables, l=l, seq=seq, **cond)
        if cache is None:
            att = _ctx_attention(q, k, v, P['attn_sink'], l=l, batch=batch, seq=seq)
            ks.append(k)
            vs.append(v)
        else:
            att = _lat_attention(q, k, v, cache[0], cache[1], P['attn_sink'], l=l, batch=batch, seq=seq)
        cvo = _conformer_conv(cv, P['conv_dw'], P['conv_dw_b'], P['conv_ln_g'], P['conv_ln_b'],
                              P['conv_pw'], l=l, batch=batch, seq=seq)
        hcat = _hyena_filters(z, P['hy_w1'][l], P['hy_b1'][l], P['hy_f1'][l], P['hy_w2'][l],
                              P['hy_b2'][l], P['hy_f2'][l], P['hy_w3'][l], P['hy_log_decay'][l])
        spectra = _filter_spectra(dft, hcat, fc=fc)
        hyo = _hyena_mixer(hy, P['hy_short_w'], P['hy_short_b'], P['hy_bias'], dft, spectra,
                           l=l, batch=batch, seq=seq, bg=bg)
        x = _mixffn(x, att, cvo, hyo, mod4, P['w_out'], P['g_ffn2'], P['w2_gate'], P['w2_up'],
                    P['w2_down'], P['g_final'], l=l, final=(l == DEPTH - 1), **cond)
    return x, ks, vs


def kernel(x_prompt, x_sample, c, cache_k, cache_v, c_ctx, w_mod, b_mod, g_ffn1, g_mix, g_ffn2,
           g_final, w1_gate, w1_up, w1_down, w2_gate, w2_up, w2_down, w_in, w_out, attn_sink,
           conv_dw, conv_dw_b, conv_ln_g, conv_ln_b, conv_pw, hy_short_w, hy_short_b,
           hy_w1, hy_b1, hy_f1, hy_w2, hy_b2, hy_f2, hy_w3, hy_log_decay, hy_bias):
    batch, seq, d = x_prompt.shape
    dec_batch, dec_seq, _ = x_sample.shape
    assert 1 + dec_batch <= COND_ROWS

    hid = LANE
    P = dict(
        g_ffn1=g_ffn1, g_mix=g_mix, g_ffn2=g_ffn2, g_final=g_final, attn_sink=attn_sink,
        w1_gate=w1_gate.astype(BF16), w1_up=w1_up.astype(BF16), w1_down=w1_down.astype(BF16),
        w2_gate=w2_gate.astype(BF16), w2_up=w2_up.astype(BF16), w2_down=w2_down.astype(BF16),
        w_in=w_in.astype(BF16), w_out=w_out.astype(BF16),
        conv_dw=conv_dw, conv_dw_b=conv_dw_b, conv_ln_g=conv_ln_g, conv_ln_b=conv_ln_b, conv_pw=conv_pw,
        hy_short_w=hy_short_w, hy_short_b=hy_short_b,
        hy_w1=jnp.pad(hy_w1, ((0, 0), (0, hid - HYENA_EMB), (0, hid - HYENA_HID))),
        hy_b1=_pad_cols(hy_b1, hid)[:, None, :], hy_f1=_pad_cols(hy_f1, hid)[:, None, :],
        hy_w2=jnp.pad(hy_w2, ((0, 0), (0, hid - HYENA_HID), (0, hid - HYENA_HID))),
        hy_b2=_pad_cols(hy_b2, hid)[:, None, :], hy_f2=_pad_cols(hy_f2, hid)[:, None, :],
        hy_w3=jnp.pad(hy_w3, ((0, 0), (0, hid - HYENA_HID), (0, 0))),
        hy_log_decay=hy_log_decay[:, None, :], hy_bias=hy_bias,
    )

    conds = jnp.concatenate([c_ctx[None, :], c, jnp.zeros((COND_ROWS - 1 - dec_batch, d), F32)], axis=0)
    mod = _modulation(conds, w_mod, b_mod)
    mod4 = mod.reshape(DEPTH, COND_ROWS, N_MOD, d)

    xp, ks, vs = _trunk(x_prompt.reshape(batch * seq, d), None, mod4, P,
                        batch=batch, seq=seq, row0=0, rows_per_cond=batch * seq)
    cache = (cache_k.reshape(*cache_k.shape[:3], KV_W), cache_v.reshape(*cache_v.shape[:3], KV_W))
    xs, _, _ = _trunk(x_sample.reshape(dec_batch * dec_seq, d), cache, mod4, P,
                      batch=dec_batch, seq=dec_seq, row0=1, rows_per_cond=dec_seq)

    kv_shape = (batch, DEPTH, seq, KV_HEADS, HEAD_DIM)
    new_k = jnp.stack([k.reshape(batch, seq, KV_W) for k in ks], axis=1).reshape(kv_shape)
    new_v = jnp.stack([v.reshape(batch, seq, KV_W) for v in vs], axis=1).reshape(kv_shape)
    return (xp.reshape(batch, seq, d), xs.reshape(dec_batch, dec_seq, d), new_k, new_v)
```

```python
import functools
import math

import numpy as np
import jax
import jax.numpy as jnp
from jax import lax
from jax.experimental import pallas as pl
from jax.experimental.pallas import tpu as pltpu

F32 = jnp.float32
BF16 = jnp.bfloat16

D_MODEL = 1024
DEPTH = 2
GRID_W = 64
N_HEADS = 8
KV_HEADS = 2
HEAD_DIM = 64
Q_PER_KV = N_HEADS // KV_HEADS
ATTN_W = N_HEADS * HEAD_DIM
KV_W = KV_HEADS * HEAD_DIM
CONV_W = D_MODEL // 4
HYENA_W = D_MODEL // 4
WINDOW = 128
BLOCK = 128
CONV_K = 31
HYENA_EMB = 33
HYENA_BANDS = (HYENA_EMB - 1) // 2
HYENA_HID = 64
D_FF = 2816
N_MOD = 9
ROPE_BASE = 10000.0
EPS = 1e-6
NEG_INF = -1e30
LOG2_E = math.log2(math.e)
Q_SCALE = HEAD_DIM ** -0.5 * LOG2_E

LANE = 128
SUBLANE = 8
MXU_W = 256
COND_ROWS = 16
TOKEN_TILE = 512
CONV_CHUNK = 128
CONV_PAD = 16
DFT_MINOR = 64
FREQ_CHUNK = 512
LONGCONV_ROWS = 4096
LONGCONV_FREQ_CHUNK = 1024
LAT_QUERIES = 256
VMEM_LIMIT = 56 * 1024 * 1024


def _params(sem, vmem=None):
    return pltpu.CompilerParams(dimension_semantics=sem, vmem_limit_bytes=vmem)


def _silu(x):
    return x * jax.nn.sigmoid(x)


def _dot(a, b):
    return jnp.dot(a, b, preferred_element_type=F32)


def _dot_nt(a, b):
    return lax.dot_general(a, b, (((1,), (1,)), ((), ())), preferred_element_type=F32)


def _dot_hi(a, b):
    return jnp.dot(a, b, preferred_element_type=F32, precision=lax.Precision.HIGHEST)


def _mod_norm(x, gain, scale, shift):
    y = x * lax.rsqrt(jnp.mean(x * x, axis=-1, keepdims=True) + EPS)
    return (y * gain) * (1.0 + scale) + shift


def _mod_kernel(c_ref, w_ref, b_ref, o_ref):
    s = _silu(c_ref[...]).astype(BF16)
    o_ref[...] = _dot(s, w_ref[...].astype(BF16)) + b_ref[...]


def _modulation(conds, w_mod, b_mod):
    d = D_MODEL
    return pl.pallas_call(
        _mod_kernel,
        grid=(DEPTH, N_MOD),
        in_specs=[
            pl.BlockSpec((COND_ROWS, d), lambda l, j: (0, 0)),
            pl.BlockSpec((None, d, d), lambda l, j: (l, 0, j)),
            pl.BlockSpec((None, 1, d), lambda l, j: (l, 0, j)),
        ],
        out_specs=pl.BlockSpec((None, COND_ROWS, d), lambda l, j: (l, 0, j)),
        out_shape=jax.ShapeDtypeStruct((DEPTH, COND_ROWS, N_MOD * d), F32),
        compiler_params=_params(("arbitrary", "arbitrary")),
        name="modulation",
    )(conds, w_mod, b_mod.reshape(DEPTH, 1, N_MOD * d))


def _swiglu_into(h16, wg_ref, wu_ref, wd_ref, a_ref):
    for c in range(D_FF // MXU_W):
        sl = slice(c * MXU_W, (c + 1) * MXU_W)
        g = _dot(h16, wg_ref[:, sl])
        u = _dot(h16, wu_ref[:, sl])
        a_ref[:, sl] = (_silu(g) * u).astype(BF16)
    return _dot(a_ref[...], wd_ref[...])


def _ffn1_then_norm(x_ref, mod_ref, g1_ref, g2_ref, wg_ref, wu_ref, wd_ref, o_ref, a_ref):
    x = x_ref[...]
    m = mod_ref[...]
    h = _mod_norm(x, g1_ref[...], m[1:2], m[0:1])
    y = _swiglu_into(h.astype(BF16), wg_ref, wu_ref, wd_ref, a_ref)
    x = x + (0.5 * m[2:3]) * y
    o_ref[...] = x
    return _mod_norm(x, g2_ref[...], m[4:5], m[3:4]).astype(BF16)


def _resident(shape, index_map):
    return pl.BlockSpec(shape, index_map, pipeline_mode=pl.Buffered(1))


def _mod_spec(l, row0, rows_per_cond, tm):
    return pl.BlockSpec((None, None, N_MOD, D_MODEL),
                        lambda i: (l, row0 + (i * tm) // rows_per_cond, 0, 0))


_IN_SPLITS = (ATTN_W, KV_W, KV_W, 2 * CONV_W, 3 * HYENA_W)
_IN_SPLITS_LAT = (ATTN_W, 2 * KV_W, 2 * KV_W, 2 * CONV_W, 3 * HYENA_W)


def _rope(x, cos, sin_signed):
    w = x.shape[-1]
    lane = lax.broadcasted_iota(jnp.int32, x.shape, 1)
    quarter = HEAD_DIM // 4
    partner = jnp.where(lane % (2 * quarter) < quarter,
                        pltpu.roll(x, w - quarter, 1), pltpu.roll(x, quarter, 1))
    return x * cos + partner * sin_signed


def _ffn_inproj_kernel(x_ref, mod_ref, g1_ref, g2_ref, wg_ref, wu_ref, wd_ref, w_ref,
                       o_ref, q_ref, k_ref, v_ref, cv_ref, hy_ref, a_ref):
    h = _ffn1_then_norm(x_ref, mod_ref, g1_ref, g2_ref, wg_ref, wu_ref, wd_ref, o_ref, a_ref)
    off = 0
    for ref, width in zip((q_ref, k_ref, v_ref, cv_ref, hy_ref), _IN_SPLITS):
        ref[...] = _dot(h, w_ref[:, off:off + width])
        off += width


def _ffn_inproj_lat_kernel(x_ref, mod_ref, g1_ref, g2_ref, wg_ref, wu_ref, wd_ref, w_ref, cos_ref, sin_ref,
                           o_ref, q_ref, k_ref, v_ref, cv_ref, hy_ref, a_ref):
    wq, wk, wv, wc, wh = _IN_SPLITS
    tile = lambda a, width: jnp.concatenate([a] * (width // a.shape[1]), axis=1)
    h = _ffn1_then_norm(x_ref, mod_ref, g1_ref, g2_ref, wg_ref, wu_ref, wd_ref, o_ref, a_ref)
    cos = cos_ref[...]
    sin = sin_ref[...]
    q = _rope(_dot(h, w_ref[:, 0:wq]), tile(cos, wq), tile(sin, wq))
    q_ref[...] = (q * Q_SCALE).astype(BF16)
    off = wq
    k_ref[...] = _dup_kv_heads(_rope(_dot(h, w_ref[:, off:off + wk]), cos, sin))
    off += wk
    v_ref[...] = _dup_kv_heads(_dot(h, w_ref[:, off:off + wv]))
    off += wv
    cv_ref[...] = _dot(h, w_ref[:, off:off + wc])
    off += wc
    hy_ref[...] = _dot(h, w_ref[:, off:off + wh])


def _ffn_inproj(x, mod4, g_ffn, g_mix, wg, wu, wd, w_in, rope_tables, *, l, row0, rows_per_cond, seq):
    t, d = x.shape
    tm = TOKEN_TILE
    lat = rope_tables is not None
    splits = _IN_SPLITS_LAT if lat else _IN_SPLITS
    dtypes = (BF16, BF16, BF16, F32, F32) if lat else (F32,) * 5
    gain_spec = pl.BlockSpec((None, 1, d), lambda i: (l, 0, 0))
    wspec_in = _resident((None, d, D_FF), lambda i: (l, 0, 0))
    in_specs = [
        pl.BlockSpec((tm, d), lambda i: (i, 0)),
        _mod_spec(l, row0, rows_per_cond, tm),
        gain_spec, gain_spec,
        wspec_in, wspec_in,
        _resident((None, D_FF, d), lambda i: (l, 0, 0)),
        _resident((None, d, sum(_IN_SPLITS)), lambda i: (l, 0, 0)),
    ]
    args = [x, mod4, g_ffn.reshape(DEPTH, 1, d), g_mix.reshape(DEPTH, 1, d), wg, wu, wd, w_in]
    if lat:
        table_spec = pl.BlockSpec((tm, LANE), lambda i: (i % (seq // tm), 0))
        in_specs += [table_spec, table_spec]
        args += list(rope_tables)
    tok = lambda w: pl.BlockSpec((tm, w), lambda i: (i, 0))
    return pl.pallas_call(
        _ffn_inproj_lat_kernel if lat else _ffn_inproj_kernel,
        grid=(t // tm,),
        in_specs=in_specs,
        out_specs=[tok(d)] + [tok(w) for w in splits],
        out_shape=[jax.ShapeDtypeStruct((t, d), F32)]
        + [jax.ShapeDtypeStruct((t, w), dt) for w, dt in zip(splits, dtypes)],
        scratch_shapes=[pltpu.VMEM((tm, D_FF), BF16)],
        compiler_params=_params(("arbitrary",), VMEM_LIMIT),
        name="ffn_inproj",
    )(*args)


def _dup_kv_heads(x):
    swapped = pltpu.roll(x, HEAD_DIM, 1)
    low = lax.broadcasted_iota(jnp.int32, x.shape, 1) < HEAD_DIM
    return jnp.concatenate([jnp.where(low, x, swapped), jnp.where(low, swapped, x)], axis=1).astype(BF16)


def _group_attention(q, g, kd, vd, sink_ref, l, valid, o_ref):
    n = q.shape[0]
    c0 = 2 * g * LANE
    lhs = jnp.concatenate([q[:, c0:c0 + LANE], q[:, c0 + LANE:c0 + 2 * LANE]], axis=0)
    top = lax.broadcasted_iota(jnp.int32, (2 * n, 1), 0) < n
    low = lax.broadcasted_iota(jnp.int32, (2 * n, LANE), 1) < HEAD_DIM
    zero = jnp.zeros((2 * n, LANE), BF16)
    outs = []
    for half in range(2):
        s = _dot_nt(jnp.where(low, lhs, zero) if half == 0 else jnp.where(low, zero, lhs), kd)
        if valid is not None:
            nloc = valid.shape[1]
            s = jnp.concatenate([jnp.where(valid, s[:, :nloc], NEG_INF), s[:, nloc:]], axis=1)
        sk = jnp.where(top, sink_ref[l, 4 * g + half], sink_ref[l, 4 * g + 2 + half]) * LOG2_E
        mx = jnp.maximum(jnp.max(s, axis=-1, keepdims=True), sk)
        p = jnp.exp2(s - mx)
        den = jnp.sum(p, axis=-1, keepdims=True) + jnp.exp2(sk - mx)
        outs.append(_dot(p.astype(BF16), vd) / den)
    o = jnp.where(low, outs[0], outs[1]).astype(BF16)
    o_ref[:, c0:c0 + LANE] = o[:n]
    o_ref[:, c0 + LANE:c0 + 2 * LANE] = o[n:]


def _ctx_attn_kernel(sink_ref, q_ref, k_ref, v_ref, o_ref, *, l):
    q = (q_ref[...] * Q_SCALE).astype(BF16)
    kd = _dup_kv_heads(k_ref[...])
    vd = _dup_kv_heads(v_ref[...])
    for g in range(KV_HEADS):
        gl = slice(g * LANE, (g + 1) * LANE)
        _group_attention(q, g, kd[:, gl], vd[:, gl], sink_ref, l, None, o_ref)


def _ctx_attention(q, k, v, sink, *, l, batch, seq):
    return pl.pallas_call(
        functools.partial(_ctx_attn_kernel, l=l),
        grid=(batch,),
        in_specs=[
            pl.BlockSpec(memory_space=pltpu.SMEM),
            pl.BlockSpec((seq, ATTN_W), lambda b: (b, 0)),
            pl.BlockSpec((seq, KV_W), lambda b: (b, 0)),
            pl.BlockSpec((seq, KV_W), lambda b: (b, 0)),
        ],
        out_specs=pl.BlockSpec((seq, ATTN_W), lambda b: (b, 0)),
        out_shape=jax.ShapeDtypeStruct((batch * seq, ATTN_W), BF16),
        compiler_params=_params(("arbitrary",)),
        name="ctx_attention",
    )(sink, q, k, v)


def _lat_attn_kernel(sink_ref, q_ref, k_ref, v_ref, kc_ref, vc_ref, o_ref, kcd_ref, vcd_ref, *, l, seq):
    i = pl.program_id(1)

    @pl.when(i == 0)
    def _():
        kcd_ref[...] = _dup_kv_heads(kc_ref[...])
        vcd_ref[...] = _dup_kv_heads(vc_ref[...])

    nq = q_ref.shape[0]
    nloc = nq + 2 * WINDOW
    q0 = i * nq
    start = pl.multiple_of(jnp.clip(q0 - WINDOW, 0, seq - nloc), WINDOW)
    jpos = start + lax.broadcasted_iota(jnp.int32, (2 * nq, nloc), 1)
    row = lax.broadcasted_iota(jnp.int32, (2 * nq, nloc), 0)
    valid = jnp.abs(jpos - (q0 + row % nq)) <= WINDOW
    q = q_ref[...]
    for g in range(KV_HEADS):
        gl = slice(g * LANE, (g + 1) * LANE)
        kd = jnp.concatenate([k_ref[pl.ds(start, nloc), gl], kcd_ref[:, gl]], axis=0)
        vd = jnp.concatenate([v_ref[pl.ds(start, nloc), gl], vcd_ref[:, gl]], axis=0)
        _group_attention(q, g, kd, vd, sink_ref, l, valid, o_ref)


def _lat_attention(q, k, v, cache_k, cache_v, sink, *, l, batch, seq):
    nq = LAT_QUERIES
    nb = seq // nq
    past = cache_k.shape[2]
    kv_spec = pl.BlockSpec((seq, 2 * KV_W), lambda b, i: (b, 0))
    cache_spec = pl.BlockSpec((None, None, past, KV_W), lambda b, i: (b, l, 0, 0))
    return pl.pallas_call(
        functools.partial(_lat_attn_kernel, l=l, seq=seq),
        grid=(batch, nb),
        in_specs=[
            pl.BlockSpec(memory_space=pltpu.SMEM),
            pl.BlockSpec((nq, ATTN_W), lambda b, i: (b * nb + i, 0)),
            kv_spec, kv_spec, cache_spec, cache_spec,
        ],
        out_specs=pl.BlockSpec((nq, ATTN_W), lambda b, i: (b * nb + i, 0)),
        out_shape=jax.ShapeDtypeStruct((batch * seq, ATTN_W), BF16),
        scratch_shapes=[pltpu.VMEM((past, 2 * KV_W), BF16), pltpu.VMEM((past, 2 * KV_W), BF16)],
        compiler_params=_params(("arbitrary", "arbitrary")),
        name="lat_attention",
    )(sink, q, k, v, cache_k, cache_v)


def _rope_tables(seq):
    rows = seq // GRID_W
    row = jnp.repeat(jnp.arange(rows), GRID_W)
    col = jnp.arange(rows * GRID_W) % GRID_W
    nf = HEAD_DIM // 4
    inv = ROPE_BASE ** (-jnp.arange(nf, dtype=F32) / nf)
    ang_r = row.astype(F32)[:, None] * inv[None, :]
    ang_c = col.astype(F32)[:, None] * inv[None, :]
    cos_h = jnp.concatenate([jnp.cos(ang_r)] * 2 + [jnp.cos(ang_c)] * 2, axis=1)
    sin_h = jnp.concatenate([-jnp.sin(ang_r), jnp.sin(ang_r), -jnp.sin(ang_c), jnp.sin(ang_c)], axis=1)
    return jnp.tile(cos_h, (1, KV_HEADS)), jnp.tile(sin_h, (1, KV_HEADS))


def _conv_kernel(cv_ref, dw_ref, dwb_ref, lng_ref, lnb_ref, pw_ref, o_ref, pad_ref, *, seq):
    x = cv_ref[...]
    zeros = jnp.zeros((CONV_PAD, CONV_W), F32)
    pad_ref[0:CONV_PAD, :] = zeros
    pad_ref[CONV_PAD:CONV_PAD + seq, :] = x[:, :CONV_W] * jax.nn.sigmoid(x[:, CONV_W:])
    pad_ref[CONV_PAD + seq:2 * CONV_PAD + seq, :] = zeros
    dw = dw_ref[...]
    pw = pw_ref[...].astype(BF16)
    first = CONV_PAD - CONV_K // 2

    def chunk(c, carry):
        base = pl.multiple_of(c * CONV_CHUNK, CONV_CHUNK)
        win = pad_ref[pl.ds(base, CONV_CHUNK + 2 * CONV_PAD), :]
        acc = jnp.zeros((CONV_CHUNK, CONV_W), F32)
        for r in range(SUBLANE):
            offs = [first + k for k in range(CONV_K) if (first + k) % SUBLANE == r]
            if not offs:
                continue
            shifted = win if r == 0 else pltpu.roll(win, win.shape[0] - r, 0)
            for off in offs:
                acc = acc + shifted[off - r:off - r + CONV_CHUNK] * dw[off - first:off - first + 1]
        acc = acc + dwb_ref[...]
        mu = jnp.mean(acc, axis=-1, keepdims=True)
        cen = acc - mu
        var = jnp.mean(cen * cen, axis=-1, keepdims=True)
        y = cen * lax.rsqrt(var + EPS) * lng_ref[...] + lnb_ref[...]
        o_ref[pl.ds(base, CONV_CHUNK), :] = _dot(_silu(y).astype(BF16), pw).astype(BF16)
        return carry

    lax.fori_loop(0, seq // CONV_CHUNK, chunk, 0, unroll=2)


def _conformer_conv(cv, dw, dwb, lng, lnb, pw, *, l, batch, seq):
    kpad = 2 * CONV_PAD
    dw_p = jnp.pad(dw, ((0, 0), (0, kpad - CONV_K), (0, 0)))
    vec = lambda a: a.reshape(DEPTH, 1, CONV_W)
    vspec = pl.BlockSpec((None, 1, CONV_W), lambda b: (l, 0, 0))
    return pl.pallas_call(
        functools.partial(_conv_kernel, seq=seq),
        grid=(batch,),
        in_specs=[
            pl.BlockSpec((seq, 2 * CONV_W), lambda b: (b, 0)),
            pl.BlockSpec((None, kpad, CONV_W), lambda b: (l, 0, 0)),
            vspec, vspec, vspec,
            pl.BlockSpec((None, CONV_W, CONV_W), lambda b: (l, 0, 0)),
        ],
        out_specs=pl.BlockSpec((seq, CONV_W), lambda b: (b, 0)),
        out_shape=jax.ShapeDtypeStruct((batch * seq, CONV_W), BF16),
        scratch_shapes=[pltpu.VMEM((seq + 2 * CONV_PAD, CONV_W), F32)],
        compiler_params=_params(("arbitrary",)),
        name="conformer_conv",
    )(cv, dw_p, vec(dwb), vec(lng), vec(lnb), pw)


def _short_conv3(x, w, b):
    n = x.shape[0]
    row = lax.broadcasted_iota(jnp.int32, x.shape, 0)
    prev = jnp.where(row == 0, 0.0, pltpu.roll(x, 1, 0))
    nxt = jnp.where(row == n - 1, 0.0, pltpu.roll(x, n - 1, 0))
    return prev * w[0:1] + x * w[1:2] + nxt * w[2:3] + b


def _filter_kernel(z_ref, w1_ref, b1_ref, f1_ref, w2_ref, b2_ref, f2_ref, w3_ref, ld_ref, o_ref):
    z = z_ref[...]
    h = jnp.sin(f1_ref[...] * (_dot_hi(z, w1_ref[...]) + b1_ref[...]))
    h = jnp.sin(f2_ref[...] * (_dot_hi(h, w2_ref[...]) + b2_ref[...]))
    h = _dot_hi(h, w3_ref[...])
    tn = z[:, 0:1]
    h = h * jnp.exp(-tn * jnp.exp(ld_ref[...]))
    ss = jnp.sum(h * h, axis=0, keepdims=True)
    c = HYENA_W
    scale = []
    for o in range(2):
        tot = ss[:, 2 * o * c:(2 * o + 1) * c] + ss[:, (2 * o + 1) * c:(2 * o + 2) * c]
        r = lax.rsqrt(tot + EPS)
        scale += [r, r]
    h = h * jnp.concatenate(scale, axis=1)
    row = lax.broadcasted_iota(jnp.int32, h.shape, 0)
    col = lax.broadcasted_iota(jnp.int32, h.shape, 1)
    o_ref[...] = jnp.where((row == 0) & ((col // c) % 2 == 1), 0.0, h)


def _hyena_features(seq):
    t = jnp.arange(seq, dtype=F32)
    tn = t / (seq - 1)
    bands = jnp.linspace(1e-4, HYENA_BANDS - 1, HYENA_BANDS, dtype=F32)
    ang = 2.0 * math.pi * t[:, None] * bands[None, :] / seq
    z = jnp.concatenate([tn[:, None], jnp.cos(ang), -jnp.sin(ang)], axis=-1)
    return jnp.pad(z, ((0, 0), (0, LANE - HYENA_EMB)))


def _hyena_filters(z, w1, b1, f1, w2, b2, f2, w3, log_decay):
    seq = z.shape[0]
    n = w3.shape[1]
    return pl.pallas_call(
        _filter_kernel,
        out_shape=jax.ShapeDtypeStruct((seq, n), F32),
        compiler_params=_params(None, VMEM_LIMIT),
        name="hyena_filters",
    )(z, w1, b1, f1, w2, b2, f2, w3, log_decay)


def _spectrum_kernel(m_ref, h_ref, a_ref, b_ref, d_ref, *, n_fft):
    j = pl.program_id(0)
    fc = m_ref.shape[1]
    c = HYENA_W
    h16 = h_ref[...].astype(BF16)
    gc = _dot(m_ref[0], h16)
    gs = _dot(m_ref[1], h16)
    nyq = jnp.sum(_alternate_rows(h16.astype(F32)), axis=0, keepdims=True)
    row = j * fc + lax.broadcasted_iota(jnp.int32, (fc, c), 0)
    is0 = row == 0
    wgt = jnp.where(is0, 1.0 / n_fft, 2.0 / n_fft)
    for o in range(2):
        f = slice(2 * o * c, (2 * o + 1) * c)
        b = slice((2 * o + 1) * c, (2 * o + 2) * c)
        re = gc[:, f] + gc[:, b]
        a_ref[o] = re * wgt
        b_ref[o] = jnp.where(is0, 0.0, gs[:, f] - gs[:, b]) * wgt
        d_ref[o] = jnp.where(is0, nyq[:, f] + nyq[:, b], re) * wgt


def _filter_spectra(dft, hcat, *, fc):
    _, nf, seq = dft.shape
    c = HYENA_W
    out_spec = pl.BlockSpec((2, fc, c), lambda j: (0, j, 0))
    shape = jax.ShapeDtypeStruct((2, nf, c), F32)
    return pl.pallas_call(
        functools.partial(_spectrum_kernel, n_fft=2 * seq),
        grid=(nf // fc,),
        in_specs=[
            pl.BlockSpec((2, fc, seq), lambda j: (0, j, 0)),
            pl.BlockSpec(hcat.shape, lambda j: (0, 0)),
        ],
        out_specs=[out_spec, out_spec, out_spec],
        out_shape=[shape, shape, shape],
        compiler_params=_params(("arbitrary",), VMEM_LIMIT),
        name="hyena_spectra",
    )(dft, hcat)


def _alternate_rows(x):
    row = lax.broadcasted_iota(jnp.int32, x.shape, 0)
    return jnp.where(row % 2 == 0, x, -x)


def _longconv_seq(u16, m_ref, a_ref, b_ref, d0_ref):
    seq = u16.shape[0]
    fcs = min(seq, LONGCONV_FREQ_CHUNK)
    row0 = lax.broadcasted_iota(jnp.int32, (fcs, u16.shape[1]), 0) == 0
    y = None
    for k in range(seq // fcs):
        fs = slice(k * fcs, (k + 1) * fcs)
        a = a_ref[fs, :]
        b = b_ref[fs, :]
        ur = _dot(m_ref[0, fs, :], u16)
        ui = _dot(m_ref[1, fs, :], u16)
        d = a
        if k == 0:
            ui = jnp.where(row0, jnp.sum(_alternate_rows(u16.astype(F32)), axis=0, keepdims=True), ui)
            d = jnp.where(row0, d0_ref[...], a)
        yr = (ur * a - ui * b).astype(BF16)
        yi = (ur * b + ui * d).astype(BF16)
        part = _dot(m_ref[0, :, fs], yr) + _dot(m_ref[1, :, fs], yi)
        if k == 0:
            y = part + _alternate_rows(jnp.broadcast_to(yi[0:1].astype(F32), u16.shape))
        else:
            y = y + part
    return y


def _hyena_conv0_kernel(hv_ref, hx_ref, sw_ref, sb_ref, bias_ref, m_ref, a_ref, b_ref, d_ref,
                        y32_ref, y16_ref, *, seq):
    c = HYENA_W
    for bi in range(hv_ref.shape[0] // seq):
        rows = slice(bi * seq, (bi + 1) * seq)
        v = _short_conv3(hv_ref[rows, :], sw_ref[:, 0:c], sb_ref[:, 0:c])
        x1 = _short_conv3(hx_ref[rows, :], sw_ref[:, c:2 * c], sb_ref[:, c:2 * c])
        conv = _longconv_seq(v.astype(BF16), m_ref, a_ref, b_ref, d_ref)
        y = x1 * (conv + bias_ref[0:1] * v)
        y32_ref[rows, :] = y
        y16_ref[rows, :] = y.astype(BF16)


def _hyena_conv1_kernel(y16_ref, y32_ref, hx_ref, sw_ref, sb_ref, bias_ref, m_ref, a_ref, b_ref, d_ref,
                        o_ref, *, seq):
    c = HYENA_W
    for bi in range(y16_ref.shape[0] // seq):
        rows = slice(bi * seq, (bi + 1) * seq)
        x2 = _short_conv3(hx_ref[rows, :], sw_ref[:, 2 * c:3 * c], sb_ref[:, 2 * c:3 * c])
        conv = _longconv_seq(y16_ref[rows, :], m_ref, a_ref, b_ref, d_ref)
        o_ref[rows, :] = (x2 * (conv + bias_ref[1:2] * y32_ref[rows, :])).astype(BF16)


def _hyena_mixer(hy, sw, sb, bias, dft, spectra, *, l, batch, seq, bg):
    c = HYENA_W
    t = batch * seq
    rows = bg * seq
    sw_p = jnp.pad(sw, ((0, 0), (0, SUBLANE - sw.shape[1]), (0, 0)))
    sb3 = sb.reshape(DEPTH, 1, 3 * c)
    chan = lambda ch: pl.BlockSpec((rows, c), lambda g: (g, ch))
    sw_spec = pl.BlockSpec((None, SUBLANE, 3 * c), lambda g: (l, 0, 0))
    sb_spec = pl.BlockSpec((None, 1, 3 * c), lambda g: (l, 0, 0))
    bias_spec = pl.BlockSpec((None, 2, c), lambda g: (l, 0, 0))
    dft_spec = _resident(dft.shape, lambda g: (0, 0, 0))
    coef = lambda order: ([_resident((None, seq, c), lambda g: (order, 0, 0))] * 2
                          + [pl.BlockSpec((None, 1, c), lambda g: (order, 0, 0))])
    spec_a, spec_b, spec_d = spectra
    spectra = (spec_a, spec_b, spec_d[:, 0:1, :])
    params = _params(("arbitrary",), VMEM_LIMIT)

    bg0 = max(1, bg // 2)
    chan0 = lambda ch: pl.BlockSpec((bg0 * seq, c), lambda g: (g, ch))
    y32, y16 = pl.pallas_call(
        functools.partial(_hyena_conv0_kernel, seq=seq),
        grid=(batch // bg0,),
        in_specs=[chan0(0), chan0(1), sw_spec, sb_spec, bias_spec, dft_spec] + coef(0),
        out_specs=[chan0(0), chan0(0)],
        out_shape=[jax.ShapeDtypeStruct((t, c), F32), jax.ShapeDtypeStruct((t, c), BF16)],
        compiler_params=params,
        name="hyena_conv0",
    )(hy, hy, sw_p, sb3, bias, dft, *spectra)

    return pl.pallas_call(
        functools.partial(_hyena_conv1_kernel, seq=seq),
        grid=(batch // bg,),
        in_specs=[chan(0), chan(0), chan(2), sw_spec, sb_spec, bias_spec, dft_spec] + coef(1),
        out_specs=chan(0),
        out_shape=jax.ShapeDtypeStruct((t, c), BF16),
        compiler_params=params,
        name="hyena_conv1",
    )(y16, y32, hy, sw_p, sb3, bias, dft, *spectra)


def _dft_kernel(c1_ref, s1_ref, c0_ref, s0_ref, o_ref):
    c0 = c0_ref[...]
    s0 = s0_ref[...]
    for i in range(c1_ref.shape[0]):
        c1 = c1_ref[i:i + 1, :]
        s1 = s1_ref[i:i + 1, :]
        rows = slice(i * DFT_MINOR, (i + 1) * DFT_MINOR)
        o_ref[0, rows, :] = (c1 * c0 - s1 * s0).astype(BF16)
        o_ref[1, rows, :] = (-(s1 * c0 + c1 * s0)).astype(BF16)


def _dft_matrix(seq):
    n = 2 * seq
    t = np.arange(seq, dtype=np.int64)
    f1n = seq // DFT_MINOR
    ang1 = 2.0 * np.pi * ((np.arange(f1n)[:, None] * DFT_MINOR * t[None, :]) % n).astype(np.float64) / n
    ang0 = 2.0 * np.pi * ((np.arange(DFT_MINOR)[:, None] * t[None, :]) % n).astype(np.float64) / n
    tables = [jnp.asarray(f(a), F32) for a in (ang1, ang0) for f in (np.cos, np.sin)]
    nb = min(f1n, SUBLANE)
    coarse = pl.BlockSpec((nb, seq), lambda j: (j, 0))
    fine = pl.BlockSpec((DFT_MINOR, seq), lambda j: (0, 0))
    return pl.pallas_call(
        _dft_kernel,
        grid=(f1n // nb,),
        in_specs=[coarse, coarse, fine, fine],
        out_specs=pl.BlockSpec((2, nb * DFT_MINOR, seq), lambda j: (0, j, 0)),
        out_shape=jax.ShapeDtypeStruct((2, seq, seq), BF16),
        compiler_params=_params(("arbitrary",)),
        name="dft_matrix",
    )(*tables)


def _mixffn_kernel(x_ref, att_ref, cvo_ref, hyo_ref, mod_ref, wo_ref, g_ref,
                   wg_ref, wu_ref, wd_ref, gf_ref, o_ref, a_ref, *, final):
    m = mod_ref[...]
    o1 = ATTN_W
    o2 = ATTN_W + CONV_W
    mix = (_dot(att_ref[...], wo_ref[0:o1, :]) + _dot(cvo_ref[...], wo_ref[o1:o2, :])
           + _dot(hyo_ref[...], wo_ref[o2:, :]))
    x = x_ref[...] + m[5:6] * mix
    h = _mod_norm(x, g_ref[...], m[7:8], m[6:7])
    y = _swiglu_into(h.astype(BF16), wg_ref, wu_ref, wd_ref, a_ref)
    x = x + (0.5 * m[8:9]) * y
    if final:
        x = x * lax.rsqrt(jnp.mean(x * x, axis=-1, keepdims=True) + EPS) * gf_ref[...]
    o_ref[...] = x


def _mixffn(x, att, cvo, hyo, mod4, w_out, gain, wg, wu, wd, g_final, *, l, row0, rows_per_cond, final):
    t, d = x.shape
    tm = TOKEN_TILE
    tok = lambda w: pl.BlockSpec((tm, w), lambda i: (i, 0))
    wspec_in = _resident((None, d, D_FF), lambda i: (l, 0, 0))
    return pl.pallas_call(
        functools.partial(_mixffn_kernel, final=final),
        grid=(t // tm,),
        in_specs=[
            tok(d), tok(ATTN_W), tok(CONV_W), tok(HYENA_W),
            _mod_spec(l, row0, rows_per_cond, tm),
            _resident((None, d, d), lambda i: (l, 0, 0)),
            pl.BlockSpec((None, 1, d), lambda i: (l, 0, 0)),
            wspec_in, wspec_in,
            _resident((None, D_FF, d), lambda i: (l, 0, 0)),
            pl.BlockSpec((1, d), lambda i: (0, 0)),
        ],
        out_specs=tok(d),
        out_shape=jax.ShapeDtypeStruct((t, d), F32),
        scratch_shapes=[pltpu.VMEM((tm, D_FF), BF16)],
        compiler_params=_params(("arbitrary",), VMEM_LIMIT),
        name="mixffn",
    )(x, att, cvo, hyo, mod4, w_out, gain.reshape(DEPTH, 1, d), wg, wu, wd, g_final.reshape(1, d))


def _pad_cols(a, width):
    return jnp.pad(a, [(0, 0)] * (a.ndim - 1) + [(0, width - a.shape[-1])])


def _trunk(x, cache, mod4, P, *, batch, seq, row0, rows_per_cond):
    fc = min(seq, FREQ_CHUNK)
    bg = max(1, min(batch, LONGCONV_ROWS // seq))
    dft = _dft_matrix(seq)
    z = _hyena_features(seq)
    rope_tables = None if cache is None else _rope_tables(seq)
    w_in = P['w_in']
    cond = dict(row0=row0, rows_per_cond=rows_per_cond)
    ks, vs = [], []
    for l in range(DEPTH):
        x, q, k, v, cv, hy = _ffn_inproj(x, mod4, P['g_ffn1'], P['g_mix'], P['w1_gate'], P['w1_up'],
                                         P['w1_down'], w_in, rope_tables, l=l, seq=seq, **cond)
        if cache is None:
            att = _ctx_attention(q, k, v, P['attn_sink'], l=l, batch=batch, seq=seq)
            ks.append(k)
            vs.append(v)
        else:
            att = _lat_attention(q, k, v, cache[0], cache[1], P['attn_sink'], l=l, batch=batch, seq=seq)
        cvo = _conformer_conv(cv, P['conv_dw'], P['conv_dw_b'], P['conv_ln_g'], P['conv_ln_b'],
                              P['conv_pw'], l=l, batch=batch, seq=seq)
        hcat = _hyena_filters(z, P['hy_w1'][l], P['hy_b1'][l], P['hy_f1'][l], P['hy_w2'][l],
                              P['hy_b2'][l], P['hy_f2'][l], P['hy_w3'][l], P['hy_log_decay'][l])
        spectra = _filter_spectra(dft, hcat, fc=fc)
        hyo = _hyena_mixer(hy, P['hy_short_w'], P['hy_short_b'], P['hy_bias'], dft, spectra,
                           l=l, batch=batch, seq=seq, bg=bg)
        x = _mixffn(x, att, cvo, hyo, mod4, P['w_out'], P['g_ffn2'], P['w2_gate'], P['w2_up'],
                    P['w2_down'], P['g_final'], l=l, final=(l == DEPTH - 1), **cond)
    return x, ks, vs


def kernel(x_prompt, x_sample, c, cache_k, cache_v, c_ctx, w_mod, b_mod, g_ffn1, g_mix, g_ffn2,
           g_final, w1_gate, w1_up, w1_down, w2_gate, w2_up, w2_down, w_in, w_out, attn_sink,
           conv_dw, conv_dw_b, conv_ln_g, conv_ln_b, conv_pw, hy_short_w, hy_short_b,
           hy_w1, hy_b1, hy_f1, hy_w2, hy_b2, hy_f2, hy_w3, hy_log_decay, hy_bias):
    batch, seq, d = x_prompt.shape
    dec_batch, dec_seq, _ = x_sample.shape
    assert 1 + dec_batch <= COND_ROWS

    hid = LANE
    P = dict(
        g_ffn1=g_ffn1, g_mix=g_mix, g_ffn2=g_ffn2, g_final=g_final, attn_sink=attn_sink,
        w1_gate=w1_gate.astype(BF16), w1_up=w1_up.astype(BF16), w1_down=w1_down.astype(BF16),
        w2_gate=w2_gate.astype(BF16), w2_up=w2_up.astype(BF16), w2_down=w2_down.astype(BF16),
        w_in=w_in.astype(BF16), w_out=w_out.astype(BF16),
        conv_dw=conv_dw, conv_dw_b=conv_dw_b, conv_ln_g=conv_ln_g, conv_ln_b=conv_ln_b, conv_pw=conv_pw,
        hy_short_w=hy_short_w, hy_short_b=hy_short_b,
        hy_w1=jnp.pad(hy_w1, ((0, 0), (0, hid - HYENA_EMB), (0, hid - HYENA_HID))),
        hy_b1=_pad_cols(hy_b1, hid)[:, None, :], hy_f1=_pad_cols(hy_f1, hid)[:, None, :],
        hy_w2=jnp.pad(hy_w2, ((0, 0), (0, hid - HYENA_HID), (0, hid - HYENA_HID))),
        hy_b2=_pad_cols(hy_b2, hid)[:, None, :], hy_f2=_pad_cols(hy_f2, hid)[:, None, :],
        hy_w3=jnp.pad(hy_w3, ((0, 0), (0, hid - HYENA_HID), (0, 0))),
        hy_log_decay=hy_log_decay[:, None, :], hy_bias=hy_bias,
    )

    conds = jnp.concatenate([c_ctx[None, :], c, jnp.zeros((COND_ROWS - 1 - dec_batch, d), F32)], axis=0)
    mod = _modulation(conds, w_mod, b_mod)
    mod4 = mod.reshape(DEPTH, COND_ROWS, N_MOD, d)

    xp, ks, vs = _trunk(x_prompt.reshape(batch * seq, d), None, mod4, P,
                        batch=batch, seq=seq, row0=0, rows_per_cond=batch * seq)
    cache = (cache_k.reshape(*cache_k.shape[:3], KV_W), cache_v.reshape(*cache_v.shape[:3], KV_W))
    xs, _, _ = _trunk(x_sample.reshape(dec_batch * dec_seq, d), cache, mod4, P,
                      batch=dec_batch, seq=dec_seq, row0=1, rows_per_cond=dec_seq)

    kv_shape = (batch, DEPTH, seq, KV_HEADS, HEAD_DIM)
    new_k = jnp.stack([k.reshape(batch, seq, KV_W) for k in ks], axis=1).reshape(kv_shape)
    new_v = jnp.stack([v.reshape(batch, seq, KV_W) for v in vs], axis=1).reshape(kv_shape)
    return (xp.reshape(batch, seq, d), xs.reshape(dec_batch, dec_seq, d), new_k, new_v)
```

```python
import functools
import math

import numpy as np
import jax
import jax.numpy as jnp
from jax import lax
from jax.experimental import pallas as pl
from jax.experimental.pallas import tpu as pltpu

F32 = jnp.float32
BF16 = jnp.bfloat16

D_MODEL = 1024
DEPTH = 2
GRID_W = 64
N_HEADS = 8
KV_HEADS = 2
HEAD_DIM = 64
Q_PER_KV = N_HEADS // KV_HEADS
ATTN_W = N_HEADS * HEAD_DIM
KV_W = KV_HEADS * HEAD_DIM
CONV_W = D_MODEL // 4
HYENA_W = D_MODEL // 4
WINDOW = 128
BLOCK = 128
CONV_K = 31
HYENA_EMB = 33
HYENA_BANDS = (HYENA_EMB - 1) // 2
HYENA_HID = 64
D_FF = 2816
N_MOD = 9
ROPE_BASE = 10000.0
EPS = 1e-6
NEG_INF = -1e30
LOG2_E = math.log2(math.e)
Q_SCALE = HEAD_DIM ** -0.5 * LOG2_E

LANE = 128
SUBLANE = 8
MXU_W = 256
COND_ROWS = 16
TOKEN_TILE = 512
CONV_CHUNK = 128
CONV_PAD = 16
DFT_MINOR = 64
FREQ_CHUNK = 512
LONGCONV_ROWS = 4096
LAT_QUERIES = 256
VMEM_LIMIT = 56 * 1024 * 1024


def _params(sem, vmem=None):
    return pltpu.CompilerParams(dimension_semantics=sem, vmem_limit_bytes=vmem)


def _silu(x):
    return x * jax.nn.sigmoid(x)


def _dot(a, b):
    return jnp.dot(a, b, preferred_element_type=F32)


def _dot_nt(a, b):
    return lax.dot_general(a, b, (((1,), (1,)), ((), ())), preferred_element_type=F32)


def _dot_hi(a, b):
    return jnp.dot(a, b, preferred_element_type=F32, precision=lax.Precision.HIGHEST)


def _mod_norm(x, gain, scale, shift):
    y = x * lax.rsqrt(jnp.mean(x * x, axis=-1, keepdims=True) + EPS)
    return (y * gain) * (1.0 + scale) + shift


def _mod_kernel(c_ref, w_ref, b_ref, o_ref):
    s = _silu(c_ref[...]).astype(BF16)
    o_ref[...] = _dot(s, w_ref[...].astype(BF16)) + b_ref[...]


def _modulation(conds, w_mod, b_mod):
    d = D_MODEL
    return pl.pallas_call(
        _mod_kernel,
        grid=(DEPTH, N_MOD),
        in_specs=[
            pl.BlockSpec((COND_ROWS, d), lambda l, j: (0, 0)),
            pl.BlockSpec((None, d, d), lambda l, j: (l, 0, j)),
            pl.BlockSpec((None, 1, d), lambda l, j: (l, 0, j)),
        ],
        out_specs=pl.BlockSpec((None, COND_ROWS, d), lambda l, j: (l, 0, j)),
        out_shape=jax.ShapeDtypeStruct((DEPTH, COND_ROWS, N_MOD * d), F32),
        compiler_params=_params(("arbitrary", "arbitrary")),
        name="modulation",
    )(conds, w_mod, b_mod.reshape(DEPTH, 1, N_MOD * d))


def _swiglu_into(h16, wg_ref, wu_ref, wd_ref, a_ref):
    for c in range(D_FF // MXU_W):
        sl = slice(c * MXU_W, (c + 1) * MXU_W)
        g = _dot(h16, wg_ref[:, sl])
        u = _dot(h16, wu_ref[:, sl])
        a_ref[:, sl] = (_silu(g) * u).astype(BF16)
    return _dot(a_ref[...], wd_ref[...])


def _ffn1_then_norm(x_ref, mod_ref, g1_ref, g2_ref, wg_ref, wu_ref, wd_ref, o_ref, a_ref):
    x = x_ref[...]
    m = mod_ref[...]
    h = _mod_norm(x, g1_ref[...], m[1:2], m[0:1])
    y = _swiglu_into(h.astype(BF16), wg_ref, wu_ref, wd_ref, a_ref)
    x = x + (0.5 * m[2:3]) * y
    o_ref[...] = x
    return _mod_norm(x, g2_ref[...], m[4:5], m[3:4]).astype(BF16)


def _resident(shape, index_map):
    return pl.BlockSpec(shape, index_map, pipeline_mode=pl.Buffered(1))


def _mod_spec(l, row0, rows_per_cond, tm):
    return pl.BlockSpec((None, None, N_MOD, D_MODEL),
                        lambda i: (l, row0 + (i * tm) // rows_per_cond, 0, 0))


_IN_SPLITS = (ATTN_W, KV_W, KV_W, 2 * CONV_W, 3 * HYENA_W)
_IN_SPLITS_LAT = (ATTN_W, 2 * KV_W, 2 * KV_W, 2 * CONV_W, 3 * HYENA_W)


def _rope(x, cos, sin_signed):
    w = x.shape[-1]
    lane = lax.broadcasted_iota(jnp.int32, x.shape, 1)
    quarter = HEAD_DIM // 4
    partner = jnp.where(lane % (2 * quarter) < quarter,
                        pltpu.roll(x, w - quarter, 1), pltpu.roll(x, quarter, 1))
    return x * cos + partner * sin_signed


def _ffn_inproj_kernel(x_ref, mod_ref, g1_ref, g2_ref, wg_ref, wu_ref, wd_ref, w_ref,
                       o_ref, q_ref, k_ref, v_ref, cv_ref, hy_ref, a_ref):
    h = _ffn1_then_norm(x_ref, mod_ref, g1_ref, g2_ref, wg_ref, wu_ref, wd_ref, o_ref, a_ref)
    off = 0
    for ref, width in zip((q_ref, k_ref, v_ref, cv_ref, hy_ref), _IN_SPLITS):
        ref[...] = _dot(h, w_ref[:, off:off + width])
        off += width


def _ffn_inproj_lat_kernel(x_ref, mod_ref, g1_ref, g2_ref, wg_ref, wu_ref, wd_ref, w_ref, cos_ref, sin_ref,
                           o_ref, q_ref, k_ref, v_ref, cv_ref, hy_ref, a_ref):
    wq, wk, wv, wc, wh = _IN_SPLITS
    tile = lambda a, width: jnp.concatenate([a] * (width // a.shape[1]), axis=1)
    h = _ffn1_then_norm(x_ref, mod_ref, g1_ref, g2_ref, wg_ref, wu_ref, wd_ref, o_ref, a_ref)
    cos = cos_ref[...]
    sin = sin_ref[...]
    q = _rope(_dot(h, w_ref[:, 0:wq]), tile(cos, wq), tile(sin, wq))
    q_ref[...] = (q * Q_SCALE).astype(BF16)
    off = wq
    k_ref[...] = _dup_kv_heads(_rope(_dot(h, w_ref[:, off:off + wk]), cos, sin))
    off += wk
    v_ref[...] = _dup_kv_heads(_dot(h, w_ref[:, off:off + wv]))
    off += wv
    cv_ref[...] = _dot(h, w_ref[:, off:off + wc])
    off += wc
    hy_ref[...] = _dot(h, w_ref[:, off:off + wh])


def _ffn_inproj(x, mod4, g_ffn, g_mix, wg, wu, wd, w_in, rope_tables, *, l, row0, rows_per_cond, seq):
    t, d = x.shape
    tm = TOKEN_TILE
    lat = rope_tables is not None
    splits = _IN_SPLITS_LAT if lat else _IN_SPLITS
    dtypes = (BF16, BF16, BF16, F32, F32) if lat else (F32,) * 5
    gain_spec = pl.BlockSpec((None, 1, d), lambda i: (l, 0, 0))
    wspec_in = _resident((None, d, D_FF), lambda i: (l, 0, 0))
    in_specs = [
        pl.BlockSpec((tm, d), lambda i: (i, 0)),
        _mod_spec(l, row0, rows_per_cond, tm),
        gain_spec, gain_spec,
        wspec_in, wspec_in,
        _resident((None, D_FF, d), lambda i: (l, 0, 0)),
        _resident((None, d, sum(_IN_SPLITS)), lambda i: (l, 0, 0)),
    ]
    args = [x, mod4, g_ffn.reshape(DEPTH, 1, d), g_mix.reshape(DEPTH, 1, d), wg, wu, wd, w_in]
    if lat:
        table_spec = pl.BlockSpec((tm, LANE), lambda i: (i % (seq // tm), 0))
        in_specs += [table_spec, table_spec]
        args += list(rope_tables)
    tok = lambda w: pl.BlockSpec((tm, w), lambda i: (i, 0))
    return pl.pallas_call(
        _ffn_inproj_lat_kernel if lat else _ffn_inproj_kernel,
        grid=(t // tm,),
        in_specs=in_specs,
        out_specs=[tok(d)] + [tok(w) for w in splits],
        out_shape=[jax.ShapeDtypeStruct((t, d), F32)]
        + [jax.ShapeDtypeStruct((t, w), dt) for w, dt in zip(splits, dtypes)],
        scratch_shapes=[pltpu.VMEM((tm, D_FF), BF16)],
        compiler_params=_params(("arbitrary",), VMEM_LIMIT),
        name="ffn_inproj",
    )(*args)


def _dup_kv_heads(x):
    swapped = pltpu.roll(x, HEAD_DIM, 1)
    low = lax.broadcasted_iota(jnp.int32, x.shape, 1) < HEAD_DIM
    return jnp.concatenate([jnp.where(low, x, swapped), jnp.where(low, swapped, x)], axis=1).astype(BF16)


def _group_attention(q, g, kd, vd, sink_ref, l, valid, o_ref):
    n = q.shape[0]
    c0 = 2 * g * LANE
    lhs = jnp.concatenate([q[:, c0:c0 + LANE], q[:, c0 + LANE:c0 + 2 * LANE]], axis=0)
    top = lax.broadcasted_iota(jnp.int32, (2 * n, 1), 0) < n
    low = lax.broadcasted_iota(jnp.int32, (2 * n, LANE), 1) < HEAD_DIM
    zero = jnp.zeros((2 * n, LANE), BF16)
    outs = []
    scores = [_dot_nt(jnp.where(low, lhs, zero), kd), _dot_nt(jnp.where(low, zero, lhs), kd)]
    for half, s in enumerate(scores):
        if valid is not None:
            nloc = valid.shape[1]
            s = jnp.concatenate([jnp.where(valid, s[:, :nloc], NEG_INF), s[:, nloc:]], axis=1)
        sk = jnp.where(top, sink_ref[l, 4 * g + half], sink_ref[l, 4 * g + 2 + half]) * LOG2_E
        mx = jnp.maximum(jnp.max(s, axis=-1, keepdims=True), sk)
        p = jnp.exp2(s - mx)
        den = jnp.sum(p, axis=-1, keepdims=True) + jnp.exp2(sk - mx)
        outs.append(_dot(p.astype(BF16), vd) / den)
    o = jnp.where(low, outs[0], outs[1]).astype(BF16)
    o_ref[:, c0:c0 + LANE] = o[:n]
    o_ref[:, c0 + LANE:c0 + 2 * LANE] = o[n:]


def _ctx_attn_kernel(sink_ref, q_ref, k_ref, v_ref, o_ref, *, l):
    q = (q_ref[...] * Q_SCALE).astype(BF16)
    kd = _dup_kv_heads(k_ref[...])
    vd = _dup_kv_heads(v_ref[...])
    for g in range(KV_HEADS):
        gl = slice(g * LANE, (g + 1) * LANE)
        _group_attention(q, g, kd[:, gl], vd[:, gl], sink_ref, l, None, o_ref)


def _ctx_attention(q, k, v, sink, *, l, batch, seq):
    return pl.pallas_call(
        functools.partial(_ctx_attn_kernel, l=l),
        grid=(batch,),
        in_specs=[
            pl.BlockSpec(memory_space=pltpu.SMEM),
            pl.BlockSpec((seq, ATTN_W), lambda b: (b, 0)),
            pl.BlockSpec((seq, KV_W), lambda b: (b, 0)),
            pl.BlockSpec((seq, KV_W), lambda b: (b, 0)),
        ],
        out_specs=pl.BlockSpec((seq, ATTN_W), lambda b: (b, 0)),
        out_shape=jax.ShapeDtypeStruct((batch * seq, ATTN_W), BF16),
        compiler_params=_params(("arbitrary",)),
        name="ctx_attention",
    )(sink, q, k, v)


def _lat_attn_kernel(sink_ref, q_ref, k_ref, v_ref, kc_ref, vc_ref, o_ref, kcd_ref, vcd_ref, *, l, seq):
    i = pl.program_id(1)

    @pl.when(i == 0)
    def _():
        kcd_ref[...] = _dup_kv_heads(kc_ref[...])
        vcd_ref[...] = _dup_kv_heads(vc_ref[...])

    nq = q_ref.shape[0]
    nloc = nq + 2 * WINDOW
    q0 = i * nq
    start = pl.multiple_of(jnp.clip(q0 - WINDOW, 0, seq - nloc), WINDOW)
    jpos = start + lax.broadcasted_iota(jnp.int32, (2 * nq, nloc), 1)
    row = lax.broadcasted_iota(jnp.int32, (2 * nq, nloc), 0)
    valid = jnp.abs(jpos - (q0 + row % nq)) <= WINDOW
    q = q_ref[...]
    for g in range(KV_HEADS):
        gl = slice(g * LANE, (g + 1) * LANE)
        kd = jnp.concatenate([k_ref[pl.ds(start, nloc), gl], kcd_ref[:, gl]], axis=0)
        vd = jnp.concatenate([v_ref[pl.ds(start, nloc), gl], vcd_ref[:, gl]], axis=0)
        _group_attention(q, g, kd, vd, sink_ref, l, valid, o_ref)


def _lat_attention(q, k, v, cache_k, cache_v, sink, *, l, batch, seq):
    nq = LAT_QUERIES
    nb = seq // nq
    past = cache_k.shape[2]
    kv_spec = pl.BlockSpec((seq, 2 * KV_W), lambda b, i: (b, 0))
    cache_spec = pl.BlockSpec((None, None, past, KV_W), lambda b, i: (b, l, 0, 0))
    return pl.pallas_call(
        functools.partial(_lat_attn_kernel, l=l, seq=seq),
        grid=(batch, nb),
        in_specs=[
            pl.BlockSpec(memory_space=pltpu.SMEM),
            pl.BlockSpec((nq, ATTN_W), lambda b, i: (b * nb + i, 0)),
            kv_spec, kv_spec, cache_spec, cache_spec,
        ],
        out_specs=pl.BlockSpec((nq, ATTN_W), lambda b, i: (b * nb + i, 0)),
        out_shape=jax.ShapeDtypeStruct((batch * seq, ATTN_W), BF16),
        scratch_shapes=[pltpu.VMEM((past, 2 * KV_W), BF16), pltpu.VMEM((past, 2 * KV_W), BF16)],
        compiler_params=_params(("arbitrary", "arbitrary")),
        name="lat_attention",
    )(sink, q, k, v, cache_k, cache_v)


def _rope_tables(seq):
    rows = seq // GRID_W
    row = jnp.repeat(jnp.arange(rows), GRID_W)
    col = jnp.arange(rows * GRID_W) % GRID_W
    nf = HEAD_DIM // 4
    inv = ROPE_BASE ** (-jnp.arange(nf, dtype=F32) / nf)
    ang_r = row.astype(F32)[:, None] * inv[None, :]
    ang_c = col.astype(F32)[:, None] * inv[None, :]
    cos_h = jnp.concatenate([jnp.cos(ang_r)] * 2 + [jnp.cos(ang_c)] * 2, axis=1)
    sin_h = jnp.concatenate([-jnp.sin(ang_r), jnp.sin(ang_r), -jnp.sin(ang_c), jnp.sin(ang_c)], axis=1)
    return jnp.tile(cos_h, (1, KV_HEADS)), jnp.tile(sin_h, (1, KV_HEADS))


def _conv_kernel(cv_ref, dw_ref, dwb_ref, lng_ref, lnb_ref, pw_ref, o_ref, pad_ref, *, seq):
    x = cv_ref[...]
    zeros = jnp.zeros((CONV_PAD, CONV_W), F32)
    pad_ref[0:CONV_PAD, :] = zeros
    pad_ref[CONV_PAD:CONV_PAD + seq, :] = x[:, :CONV_W] * jax.nn.sigmoid(x[:, CONV_W:])
    pad_ref[CONV_PAD + seq:2 * CONV_PAD + seq, :] = zeros
    dw = dw_ref[...]
    pw = pw_ref[...].astype(BF16)
    first = CONV_PAD - CONV_K // 2

    def chunk(c, carry):
        base = pl.multiple_of(c * CONV_CHUNK, CONV_CHUNK)
        win = pad_ref[pl.ds(base, CONV_CHUNK + 2 * CONV_PAD), :]
        acc = jnp.zeros((CONV_CHUNK, CONV_W), F32)
        for r in range(SUBLANE):
            offs = [first + k for k in range(CONV_K) if (first + k) % SUBLANE == r]
            if not offs:
                continue
            shifted = win if r == 0 else pltpu.roll(win, win.shape[0] - r, 0)
            for off in offs:
                acc = acc + shifted[off - r:off - r + CONV_CHUNK] * dw[off - first:off - first + 1]
        acc = acc + dwb_ref[...]
        mu = jnp.mean(acc, axis=-1, keepdims=True)
        cen = acc - mu
        var = jnp.mean(cen * cen, axis=-1, keepdims=True)
        y = cen * lax.rsqrt(var + EPS) * lng_ref[...] + lnb_ref[...]
        o_ref[pl.ds(base, CONV_CHUNK), :] = _dot(_silu(y).astype(BF16), pw).astype(BF16)
        return carry

    lax.fori_loop(0, seq // CONV_CHUNK, chunk, 0, unroll=2)


def _conformer_conv(cv, dw, dwb, lng, lnb, pw, *, l, batch, seq):
    kpad = 2 * CONV_PAD
    dw_p = jnp.pad(dw, ((0, 0), (0, kpad - CONV_K), (0, 0)))
    vec = lambda a: a.reshape(DEPTH, 1, CONV_W)
    vspec = pl.BlockSpec((None, 1, CONV_W), lambda b: (l, 0, 0))
    return pl.pallas_call(
        functools.partial(_conv_kernel, seq=seq),
        grid=(batch,),
        in_specs=[
            pl.BlockSpec((seq, 2 * CONV_W), lambda b: (b, 0)),
            pl.BlockSpec((None, kpad, CONV_W), lambda b: (l, 0, 0)),
            vspec, vspec, vspec,
            pl.BlockSpec((None, CONV_W, CONV_W), lambda b: (l, 0, 0)),
        ],
        out_specs=pl.BlockSpec((seq, CONV_W), lambda b: (b, 0)),
        out_shape=jax.ShapeDtypeStruct((batch * seq, CONV_W), BF16),
        scratch_shapes=[pltpu.VMEM((seq + 2 * CONV_PAD, CONV_W), F32)],
        compiler_params=_params(("arbitrary",)),
        name="conformer_conv",
    )(cv, dw_p, vec(dwb), vec(lng), vec(lnb), pw)


def _short_conv3(x, w, b):
    n = x.shape[0]
    row = lax.broadcasted_iota(jnp.int32, x.shape, 0)
    prev = jnp.where(row == 0, 0.0, pltpu.roll(x, 1, 0))
    nxt = jnp.where(row == n - 1, 0.0, pltpu.roll(x, n - 1, 0))
    return prev * w[0:1] + x * w[1:2] + nxt * w[2:3] + b


def _filter_kernel(z_ref, w1_ref, b1_ref, f1_ref, w2_ref, b2_ref, f2_ref, w3_ref, ld_ref, o_ref):
    z = z_ref[...]
    h = jnp.sin(f1_ref[...] * (_dot_hi(z, w1_ref[...]) + b1_ref[...]))
    h = jnp.sin(f2_ref[...] * (_dot_hi(h, w2_ref[...]) + b2_ref[...]))
    h = _dot_hi(h, w3_ref[...])
    tn = z[:, 0:1]
    h = h * jnp.exp(-tn * jnp.exp(ld_ref[...]))
    ss = jnp.sum(h * h, axis=0, keepdims=True)
    c = HYENA_W
    scale = []
    for o in range(2):
        tot = ss[:, 2 * o * c:(2 * o + 1) * c] + ss[:, (2 * o + 1) * c:(2 * o + 2) * c]
        r = lax.rsqrt(tot + EPS)
        scale += [r, r]
    h = h * jnp.concatenate(scale, axis=1)
    row = lax.broadcasted_iota(jnp.int32, h.shape, 0)
    col = lax.broadcasted_iota(jnp.int32, h.shape, 1)
    o_ref[...] = jnp.where((row == 0) & ((col // c) % 2 == 1), 0.0, h)


def _hyena_features(seq):
    t = jnp.arange(seq, dtype=F32)
    tn = t / (seq - 1)
    bands = jnp.linspace(1e-4, HYENA_BANDS - 1, HYENA_BANDS, dtype=F32)
    ang = 2.0 * math.pi * t[:, None] * bands[None, :] / seq
    z = jnp.concatenate([tn[:, None], jnp.cos(ang), -jnp.sin(ang)], axis=-1)
    return jnp.pad(z, ((0, 0), (0, LANE - HYENA_EMB)))


def _hyena_filters(z, w1, b1, f1, w2, b2, f2, w3, log_decay):
    seq = z.shape[0]
    n = w3.shape[1]
    return pl.pallas_call(
        _filter_kernel,
        out_shape=jax.ShapeDtypeStruct((seq, n), F32),
        compiler_params=_params(None, VMEM_LIMIT),
        name="hyena_filters",
    )(z, w1, b1, f1, w2, b2, f2, w3, log_decay)


def _spectrum_kernel(m_ref, h_ref, lr_ref, li_ref, hr_ref, hi_ref, mid_ref, *, n_fft):
    j = pl.program_id(0)
    fc = m_ref.shape[1]
    c = HYENA_W
    h16 = h_ref[...].astype(BF16)
    h = h16.astype(F32)
    hh = jnp.concatenate([h16, _alternate_rows(h).astype(BF16)], axis=1)
    gc = _dot(m_ref[0], hh)
    gs = _dot(m_ref[1], hh)
    row = j * fc + lax.broadcasted_iota(jnp.int32, (fc, c), 0)
    wgt = jnp.where(row == 0, 1.0 / n_fft, 2.0 / n_fft)
    t4 = lax.broadcasted_iota(jnp.int32, h.shape, 0) % 4
    mid_c = jnp.sum(jnp.where(t4 == 0, h, jnp.where(t4 == 2, -h, 0.0)), axis=0, keepdims=True)
    mid_s = jnp.sum(jnp.where(t4 == 3, h, jnp.where(t4 == 1, -h, 0.0)), axis=0, keepdims=True)
    for o in range(2):
        f = slice(2 * o * c, (2 * o + 1) * c)
        b = slice((2 * o + 1) * c, (2 * o + 2) * c)
        fa = slice(4 * c + 2 * o * c, 4 * c + (2 * o + 1) * c)
        ba = slice(4 * c + (2 * o + 1) * c, 4 * c + (2 * o + 2) * c)
        lr_ref[o] = (gc[:, f] + gc[:, b]) * wgt
        li_ref[o] = (gs[:, f] - gs[:, b]) * wgt
        hr_ref[o] = (gc[:, fa] + gc[:, ba]) * wgt
        hi_ref[o] = (gs[:, ba] - gs[:, fa]) * wgt
        mid_ref[o] = jnp.concatenate([mid_c[:, f] + mid_c[:, b], mid_s[:, f] - mid_s[:, b]],
                                     axis=0) * (2.0 / n_fft)


def _filter_spectra(dft, hcat, *, fc):
    seq = dft.shape[1]
    half = seq // 2
    c = HYENA_W
    fc = min(fc, half)
    out_spec = pl.BlockSpec((2, fc, c), lambda j: (0, j, 0))
    shape = jax.ShapeDtypeStruct((2, half, c), F32)
    return pl.pallas_call(
        functools.partial(_spectrum_kernel, n_fft=2 * seq),
        grid=(half // fc,),
        in_specs=[
            pl.BlockSpec((2, fc, seq), lambda j: (0, j, 0)),
            _resident(hcat.shape, lambda j: (0, 0)),
        ],
        out_specs=[out_spec] * 4 + [pl.BlockSpec((2, 2, c), lambda j: (0, 0, 0))],
        out_shape=[shape] * 4 + [jax.ShapeDtypeStruct((2, 2, c), F32)],
        compiler_params=_params(("arbitrary",), VMEM_LIMIT),
        name="hyena_spectra",
    )(dft, hcat)


def _alternate_rows(x):
    row = lax.broadcasted_iota(jnp.int32, x.shape, 0)
    return jnp.where(row % 2 == 0, x, -x)


def _longconv_split(ue, uo, m_ref, tw_ref, lr_ref, li_ref, hr_ref, hi_ref, mid_ref):
    shape = ue.shape
    tile = lambda a: jnp.concatenate([a] * (shape[1] // a.shape[1]), axis=1)
    wc = tile(tw_ref[0])
    ws = tile(tw_ref[1])
    ue16 = ue.astype(BF16)
    uo16 = uo.astype(BF16)
    er = _dot(m_ref[0], ue16)
    ei = _dot(m_ref[1], ue16)
    orr = _dot(m_ref[0], uo16)
    oi = _dot(m_ref[1], uo16)
    en = jnp.sum(_alternate_rows(ue16.astype(F32)), axis=0, keepdims=True)
    on = jnp.sum(_alternate_rows(uo16.astype(F32)), axis=0, keepdims=True)
    tr = wc * orr + ws * oi
    ti = wc * oi - ws * orr
    pr, pi, mr, mi = er + tr, ei + ti, er - tr, ei - ti
    lr, li, hr, hi = lr_ref[...], li_ref[...], hr_ref[...], hi_ref[...]
    ylr = pr * lr - pi * li
    yli = pr * li + pi * lr
    yhr = mr * hr + mi * hi
    yhi = mr * hi - mi * hr
    dr = ylr - yhr
    di = yli + yhi
    ger = (ylr + yhr).astype(BF16)
    gei = (yli - yhi).astype(BF16)
    gor = (dr * wc - di * ws).astype(BF16)
    goi = (dr * ws + di * wc).astype(BF16)
    ym_r = en * mid_ref[0:1] + on * mid_ref[1:2]
    ym_i = en * mid_ref[1:2] - on * mid_ref[0:1]
    ye = _dot(m_ref[0], ger) + _dot(m_ref[1], gei) + _alternate_rows(jnp.broadcast_to(ym_r, shape))
    yo = _dot(m_ref[0], gor) + _dot(m_ref[1], goi) - _alternate_rows(jnp.broadcast_to(ym_i, shape))
    return ye, yo


def _deinterleave(x, buf_ref):
    half = x.shape[0] // 2
    groups = buf_ref.shape[0]
    for g in range(groups):
        buf_ref[g] = x[:, g * LANE:(g + 1) * LANE]
    pick = lambda first: jnp.concatenate(
        [buf_ref[g, pl.ds(first, half, stride=2), :] for g in range(groups)], axis=1)
    return pick(0), pick(1)


def _interleave(even, odd, buf_ref):
    half = even.shape[0]
    groups = buf_ref.shape[0]
    for g in range(groups):
        buf_ref[g, pl.ds(0, half, stride=2), :] = even[:, g * LANE:(g + 1) * LANE]
        buf_ref[g, pl.ds(1, half, stride=2), :] = odd[:, g * LANE:(g + 1) * LANE]
    return jnp.concatenate([buf_ref[g] for g in range(groups)], axis=1)


def _hyena_conv0_kernel(hv_ref, hx_ref, sw_ref, sb_ref, bias_ref, m_ref, tw_ref, lr_ref, li_ref, hr_ref, hi_ref,
                        mid_ref, y_ref, vbuf_ref, xbuf_ref, *, seq):
    c = HYENA_W
    half = seq // 2
    for bi in range(hv_ref.shape[0] // seq):
        rows = slice(bi * seq, (bi + 1) * seq)
        ve, vo = _deinterleave(_short_conv3(hv_ref[rows, :], sw_ref[:, 0:c], sb_ref[:, 0:c]), vbuf_ref)
        xe, xo = _deinterleave(_short_conv3(hx_ref[rows, :], sw_ref[:, c:2 * c], sb_ref[:, c:2 * c]), xbuf_ref)
        ce, co = _longconv_split(ve, vo, m_ref, tw_ref, lr_ref, li_ref, hr_ref, hi_ref, mid_ref)
        y_ref[bi * seq:bi * seq + half, :] = xe * (ce + bias_ref[0:1] * ve)
        y_ref[bi * seq + half:(bi + 1) * seq, :] = xo * (co + bias_ref[0:1] * vo)


def _hyena_conv1_kernel(y_ref, hx_ref, sw_ref, sb_ref, bias_ref, m_ref, tw_ref, lr_ref, li_ref, hr_ref, hi_ref,
                        mid_ref, o_ref, xbuf_ref, obuf_ref, *, seq):
    c = HYENA_W
    half = seq // 2
    for bi in range(y_ref.shape[0] // seq):
        rows = slice(bi * seq, (bi + 1) * seq)
        ye = y_ref[bi * seq:bi * seq + half, :]
        yo = y_ref[bi * seq + half:(bi + 1) * seq, :]
        xe, xo = _deinterleave(_short_conv3(hx_ref[rows, :], sw_ref[:, 2 * c:3 * c], sb_ref[:, 2 * c:3 * c]),
                               xbuf_ref)
        ce, co = _longconv_split(ye, yo, m_ref, tw_ref, lr_ref, li_ref, hr_ref, hi_ref, mid_ref)
        out = _interleave(xe * (ce + bias_ref[1:2] * ye), xo * (co + bias_ref[1:2] * yo), obuf_ref)
        o_ref[rows, :] = out.astype(BF16)


def _twiddles(seq):
    f = np.arange(seq // 2, dtype=np.float64)[:, None] * np.ones((1, LANE))
    ang = 2.0 * np.pi * f / (2 * seq)
    return jnp.asarray(np.stack([np.cos(ang), np.sin(ang)]), F32)


def _hyena_mixer(hy, sw, sb, bias, dft, twiddle, spectra, *, l, batch, seq, bg):
    c = HYENA_W
    t = batch * seq
    rows = bg * seq
    sw_p = jnp.pad(sw, ((0, 0), (0, SUBLANE - sw.shape[1]), (0, 0)))
    sb3 = sb.reshape(DEPTH, 1, 3 * c)
    chan = lambda ch: pl.BlockSpec((rows, c), lambda g: (g, ch))
    sw_spec = pl.BlockSpec((None, SUBLANE, 3 * c), lambda g: (l, 0, 0))
    sb_spec = pl.BlockSpec((None, 1, 3 * c), lambda g: (l, 0, 0))
    bias_spec = pl.BlockSpec((None, 2, c), lambda g: (l, 0, 0))
    half = seq // 2
    dft_spec = _resident(dft.shape, lambda g: (0, 0, 0))
    tw_spec = _resident(twiddle.shape, lambda g: (0, 0, 0))
    coef = lambda order: ([_resident((None, half, c), lambda g: (order, 0, 0))] * 4
                          + [pl.BlockSpec((None, 2, c), lambda g: (order, 0, 0))])
    params = _params(("arbitrary",), VMEM_LIMIT)
    seq_buf = pltpu.VMEM((c // LANE, seq, LANE), F32)

    y = pl.pallas_call(
        functools.partial(_hyena_conv0_kernel, seq=seq),
        grid=(batch // bg,),
        in_specs=[chan(0), chan(1), sw_spec, sb_spec, bias_spec, dft_spec, tw_spec] + coef(0),
        out_specs=chan(0),
        out_shape=jax.ShapeDtypeStruct((t, c), F32),
        scratch_shapes=[seq_buf, seq_buf],
        compiler_params=params,
        name="hyena_conv0",
    )(hy, hy, sw_p, sb3, bias, dft, twiddle, *spectra)

    return pl.pallas_call(
        functools.partial(_hyena_conv1_kernel, seq=seq),
        grid=(batch // bg,),
        in_specs=[chan(0), chan(2), sw_spec, sb_spec, bias_spec, dft_spec, tw_spec] + coef(1),
        out_specs=chan(0),
        out_shape=jax.ShapeDtypeStruct((t, c), BF16),
        scratch_shapes=[seq_buf, seq_buf],
        compiler_params=params,
        name="hyena_conv1",
    )(y, hy, sw_p, sb3, bias, dft, twiddle, *spectra)


def _dft_kernel(c1_ref, s1_ref, c0_ref, s0_ref, o_ref):
    c0 = c0_ref[...]
    s0 = s0_ref[...]
    for i in range(c1_ref.shape[0]):
        c1 = c1_ref[i:i + 1, :]
        s1 = s1_ref[i:i + 1, :]
        rows = slice(i * DFT_MINOR, (i + 1) * DFT_MINOR)
        o_ref[0, rows, :] = (c1 * c0 - s1 * s0).astype(BF16)
        o_ref[1, rows, :] = (-(s1 * c0 + c1 * s0)).astype(BF16)


def _dft_matrix(seq):
    n = 2 * seq
    t = np.arange(seq, dtype=np.int64)
    f1n = seq // DFT_MINOR
    ang1 = 2.0 * np.pi * ((np.arange(f1n)[:, None] * DFT_MINOR * t[None, :]) % n).astype(np.float64) / n
    ang0 = 2.0 * np.pi * ((np.arange(DFT_MINOR)[:, None] * t[None, :]) % n).astype(np.float64) / n
    tables = [jnp.asarray(f(a), F32) for a in (ang1, ang0) for f in (np.cos, np.sin)]
    nb = min(f1n, SUBLANE)
    coarse = pl.BlockSpec((nb, seq), lambda j: (j, 0))
    fine = pl.BlockSpec((DFT_MINOR, seq), lambda j: (0, 0))
    return pl.pallas_call(
        _dft_kernel,
        grid=(f1n // nb,),
        in_specs=[coarse, coarse, fine, fine],
        out_specs=pl.BlockSpec((2, nb * DFT_MINOR, seq), lambda j: (0, j, 0)),
        out_shape=jax.ShapeDtypeStruct((2, seq, seq), BF16),
        compiler_params=_params(("arbitrary",)),
        name="dft_matrix",
    )(*tables)


def _mixffn_kernel(x_ref, att_ref, cvo_ref, hyo_ref, mod_ref, wo_ref, g_ref,
                   wg_ref, wu_ref, wd_ref, gf_ref, o_ref, a_ref, *, final):
    m = mod_ref[...]
    o1 = ATTN_W
    o2 = ATTN_W + CONV_W
    mix = (_dot(att_ref[...], wo_ref[0:o1, :]) + _dot(cvo_ref[...], wo_ref[o1:o2, :])
           + _dot(hyo_ref[...], wo_ref[o2:, :]))
    x = x_ref[...] + m[5:6] * mix
    h = _mod_norm(x, g_ref[...], m[7:8], m[6:7])
    y = _swiglu_into(h.astype(BF16), wg_ref, wu_ref, wd_ref, a_ref)
    x = x + (0.5 * m[8:9]) * y
    if final:
        x = x * lax.rsqrt(jnp.mean(x * x, axis=-1, keepdims=True) + EPS) * gf_ref[...]
    o_ref[...] = x


def _mixffn(x, att, cvo, hyo, mod4, w_out, gain, wg, wu, wd, g_final, *, l, row0, rows_per_cond, final):
    t, d = x.shape
    tm = TOKEN_TILE
    tok = lambda w: pl.BlockSpec((tm, w), lambda i: (i, 0))
    wspec_in = _resident((None, d, D_FF), lambda i: (l, 0, 0))
    return pl.pallas_call(
        functools.partial(_mixffn_kernel, final=final),
        grid=(t // tm,),
        in_specs=[
            tok(d), tok(ATTN_W), tok(CONV_W), tok(HYENA_W),
            _mod_spec(l, row0, rows_per_cond, tm),
            _resident((None, d, d), lambda i: (l, 0, 0)),
            pl.BlockSpec((None, 1, d), lambda i: (l, 0, 0)),
            wspec_in, wspec_in,
            _resident((None, D_FF, d), lambda i: (l, 0, 0)),
            pl.BlockSpec((1, d), lambda i: (0, 0)),
        ],
        out_specs=tok(d),
        out_shape=jax.ShapeDtypeStruct((t, d), F32),
        scratch_shapes=[pltpu.VMEM((tm, D_FF), BF16)],
        compiler_params=_params(("arbitrary",), VMEM_LIMIT),
        name="mixffn",
    )(x, att, cvo, hyo, mod4, w_out, gain.reshape(DEPTH, 1, d), wg, wu, wd, g_final.reshape(1, d))


def _pad_cols(a, width):
    return jnp.pad(a, [(0, 0)] * (a.ndim - 1) + [(0, width - a.shape[-1])])


def _trunk(x, cache, mod4, P, *, batch, seq, row0, rows_per_cond):
    fc = min(seq, FREQ_CHUNK)
    bg = max(1, min(batch, LONGCONV_ROWS // seq))
    dft = _dft_matrix(seq)
    dft_half = _dft_matrix(seq // 2)
    twiddle = _twiddles(seq)
    z = _hyena_features(seq)
    rope_tables = None if cache is None else _rope_tables(seq)
    w_in = P['w_in']
    cond = dict(row0=row0, rows_per_cond=rows_per_cond)
    ks, vs = [], []
    for l in range(DEPTH):
        x, q, k, v, cv, hy = _ffn_inproj(x, mod4, P['g_ffn1'], P['g_mix'], P['w1_gate'], P['w1_up'],
                                         P['w1_down'], w_in, rope_tables, l=l, seq=seq, **cond)
        if cache is None:
            att = _ctx_attention(q, k, v, P['attn_sink'], l=l, batch=batch, seq=seq)
            ks.append(k)
            vs.append(v)
        else:
            att = _lat_attention(q, k, v, cache[0], cache[1], P['attn_sink'], l=l, batch=batch, seq=seq)
        cvo = _conformer_conv(cv, P['conv_dw'], P['conv_dw_b'], P['conv_ln_g'], P['conv_ln_b'],
                              P['conv_pw'], l=l, batch=batch, seq=seq)
        hcat = _hyena_filters(z, P['hy_w1'][l], P['hy_b1'][l], P['hy_f1'][l], P['hy_w2'][l],
                              P['hy_b2'][l], P['hy_f2'][l], P['hy_w3'][l], P['hy_log_decay'][l])
        spectra = _filter_spectra(dft, hcat, fc=fc)
        hyo = _hyena_mixer(hy, P['hy_short_w'], P['hy_short_b'], P['hy_bias'], dft_half, twiddle, spectra,
                           l=l, batch=batch, seq=seq, bg=bg)
        x = _mixffn(x, att, cvo, hyo, mod4, P['w_out'], P['g_ffn2'], P['w2_gate'], P['w2_up'],
                    P['w2_down'], P['g_final'], l=l, final=(l == DEPTH - 1), **cond)
    return x, ks, vs


def kernel(x_prompt, x_sample, c, cache_k, cache_v, c_ctx, w_mod, b_mod, g_ffn1, g_mix, g_ffn2,
           g_final, w1_gate, w1_up, w1_down, w2_gate, w2_up, w2_down, w_in, w_out, attn_sink,
           conv_dw, conv_dw_b, conv_ln_g, conv_ln_b, conv_pw, hy_short_w, hy_short_b,
           hy_w1, hy_b1, hy_f1, hy_w2, hy_b2, hy_f2, hy_w3, hy_log_decay, hy_bias):
    batch, seq, d = x_prompt.shape
    dec_batch, dec_seq, _ = x_sample.shape
    assert 1 + dec_batch <= COND_ROWS

    hid = LANE
    P = dict(
        g_ffn1=g_ffn1, g_mix=g_mix, g_ffn2=g_ffn2, g_final=g_final, attn_sink=attn_sink,
        w1_gate=w1_gate.astype(BF16), w1_up=w1_up.astype(BF16), w1_down=w1_down.astype(BF16),
        w2_gate=w2_gate.astype(BF16), w2_up=w2_up.astype(BF16), w2_down=w2_down.astype(BF16),
        w_in=w_in.astype(BF16), w_out=w_out.astype(BF16),
        conv_dw=conv_dw, conv_dw_b=conv_dw_b, conv_ln_g=conv_ln_g, conv_ln_b=conv_ln_b, conv_pw=conv_pw,
        hy_short_w=hy_short_w, hy_short_b=hy_short_b,
        hy_w1=jnp.pad(hy_w1, ((0, 0), (0, hid - HYENA_EMB), (0, hid - HYENA_HID))),
        hy_b1=_pad_cols(hy_b1, hid)[:, None, :], hy_f1=_pad_cols(hy_f1, hid)[:, None, :],
        hy_w2=jnp.pad(hy_w2, ((0, 0), (0, hid - HYENA_HID), (0, hid - HYENA_HID))),
        hy_b2=_pad_cols(hy_b2, hid)[:, None, :], hy_f2=_pad_cols(hy_f2, hid)[:, None, :],
        hy_w3=jnp.pad(hy_w3, ((0, 0), (0, hid - HYENA_HID), (0, 0))),
        hy_log_decay=hy_log_decay[:, None, :], hy_bias=hy_bias,
    )

    conds = jnp.concatenate([c_ctx[None, :], c, jnp.zeros((COND_ROWS - 1 - dec_batch, d), F32)], axis=0)
    mod = _modulation(conds, w_mod, b_mod)
    mod4 = mod.reshape(DEPTH, COND_ROWS, N_MOD, d)

    xp, ks, vs = _trunk(x_prompt.reshape(batch * seq, d), None, mod4, P,
                        batch=batch, seq=seq, row0=0, rows_per_cond=batch * seq)
    cache = (cache_k.reshape(*cache_k.shape[:3], KV_W), cache_v.reshape(*cache_v.shape[:3], KV_W))
    xs, _, _ = _trunk(x_sample.reshape(dec_batch * dec_seq, d), cache, mod4, P,
                      batch=dec_batch, seq=dec_seq, row0=1, rows_per_cond=dec_seq)

    kv_shape = (batch, DEPTH, seq, KV_HEADS, HEAD_DIM)
    new_k = jnp.stack([k.reshape(batch, seq, KV_W) for k in ks], axis=1).reshape(kv_shape)
    new_v = jnp.stack([v.reshape(batch, seq, KV_W) for v in vs], axis=1).reshape(kv_shape)
    return (xp.reshape(batch, seq, d), xs.reshape(dec_batch, dec_seq, d), new_k, new_v)
```

```python
import functools
import math

import numpy as np
import jax
import jax.numpy as jnp
from jax import lax
from jax.experimental import pallas as pl
from jax.experimental.pallas import tpu as pltpu

F32 = jnp.float32
BF16 = jnp.bfloat16

D_MODEL = 1024
DEPTH = 2
GRID_W = 64
N_HEADS = 8
KV_HEADS = 2
HEAD_DIM = 64
Q_PER_KV = N_HEADS // KV_HEADS
ATTN_W = N_HEADS * HEAD_DIM
KV_W = KV_HEADS * HEAD_DIM
CONV_W = D_MODEL // 4
HYENA_W = D_MODEL // 4
WINDOW = 128
BLOCK = 128
CONV_K = 31
HYENA_EMB = 33
HYENA_BANDS = (HYENA_EMB - 1) // 2
HYENA_HID = 64
D_FF = 2816
N_MOD = 9
ROPE_BASE = 10000.0
EPS = 1e-6
NEG_INF = -1e30
LOG2_E = math.log2(math.e)
Q_SCALE = HEAD_DIM ** -0.5 * LOG2_E

LANE = 128
SUBLANE = 8
MXU_W = 256
COND_ROWS = 16
TOKEN_TILE = 512
CONV_CHUNK = 128
CONV_PAD = 16
DFT_MINOR = 64
FREQ_CHUNK = 512
LONGCONV_ROWS = 4096
LAT_QUERIES = 256
VMEM_LIMIT = 56 * 1024 * 1024


def _params(sem, vmem=None):
    return pltpu.CompilerParams(dimension_semantics=sem, vmem_limit_bytes=vmem)


def _silu(x):
    return x * jax.nn.sigmoid(x)


def _dot(a, b):
    return jnp.dot(a, b, preferred_element_type=F32)


def _dot_nt(a, b):
    return lax.dot_general(a, b, (((1,), (1,)), ((), ())), preferred_element_type=F32)


def _dot_hi(a, b):
    return jnp.dot(a, b, preferred_element_type=F32, precision=lax.Precision.HIGHEST)


def _mod_norm(x, gain, scale, shift):
    y = x * lax.rsqrt(jnp.mean(x * x, axis=-1, keepdims=True) + EPS)
    return (y * gain) * (1.0 + scale) + shift


def _mod_kernel(c_ref, w_ref, b_ref, o_ref):
    s = _silu(c_ref[...]).astype(BF16)
    o_ref[...] = _dot(s, w_ref[...].astype(BF16)) + b_ref[...]


def _modulation(conds, w_mod, b_mod):
    d = D_MODEL
    return pl.pallas_call(
        _mod_kernel,
        grid=(DEPTH, N_MOD),
        in_specs=[
            pl.BlockSpec((COND_ROWS, d), lambda l, j: (0, 0)),
            pl.BlockSpec((None, d, d), lambda l, j: (l, 0, j)),
            pl.BlockSpec((None, 1, d), lambda l, j: (l, 0, j)),
        ],
        out_specs=pl.BlockSpec((None, COND_ROWS, d), lambda l, j: (l, 0, j)),
        out_shape=jax.ShapeDtypeStruct((DEPTH, COND_ROWS, N_MOD * d), F32),
        compiler_params=_params(("arbitrary", "arbitrary")),
        name="modulation",
    )(conds, w_mod, b_mod.reshape(DEPTH, 1, N_MOD * d))


def _row_halves(n):
    return slice(0, n // 2), slice(n // 2, n)


def _swiglu_chains(hs, rows, wg_ref, wu_ref, wd_ref, a_ref):
    for c in range(D_FF // MXU_W):
        sl = slice(c * MXU_W, (c + 1) * MXU_W)
        for r, h in zip(rows, hs):
            a_ref[r, sl] = (_silu(_dot(h, wg_ref[:, sl])) * _dot(h, wu_ref[:, sl])).astype(BF16)
    return [_dot(a_ref[r, :], wd_ref[...]) for r in rows]


def _ffn1_then_norm(x_ref, mod_ref, g1_ref, g2_ref, wg_ref, wu_ref, wd_ref, o_ref, a_ref):
    m = mod_ref[...]
    rows = _row_halves(x_ref.shape[0])
    xs = [x_ref[r, :] for r in rows]
    hs = [_mod_norm(x, g1_ref[...], m[1:2], m[0:1]).astype(BF16) for x in xs]
    ys = _swiglu_chains(hs, rows, wg_ref, wu_ref, wd_ref, a_ref)
    out = []
    for r, x, y in zip(rows, xs, ys):
        x = x + (0.5 * m[2:3]) * y
        o_ref[r, :] = x
        out.append(_mod_norm(x, g2_ref[...], m[4:5], m[3:4]).astype(BF16))
    return rows, out


def _resident(shape, index_map):
    return pl.BlockSpec(shape, index_map, pipeline_mode=pl.Buffered(1))


def _mod_spec(l, row0, rows_per_cond, tm):
    return pl.BlockSpec((None, None, N_MOD, D_MODEL),
                        lambda i: (l, row0 + (i * tm) // rows_per_cond, 0, 0))


_IN_SPLITS = (ATTN_W, KV_W, KV_W, 2 * CONV_W, 3 * HYENA_W)
_IN_SPLITS_LAT = (ATTN_W, 2 * KV_W, 2 * KV_W, 2 * CONV_W, 3 * HYENA_W)


def _rope(x, cos, sin_signed):
    w = x.shape[-1]
    lane = lax.broadcasted_iota(jnp.int32, x.shape, 1)
    quarter = HEAD_DIM // 4
    partner = jnp.where(lane % (2 * quarter) < quarter,
                        pltpu.roll(x, w - quarter, 1), pltpu.roll(x, quarter, 1))
    return x * cos + partner * sin_signed


def _ffn_inproj_kernel(x_ref, mod_ref, g1_ref, g2_ref, wg_ref, wu_ref, wd_ref, w_ref,
                       o_ref, q_ref, k_ref, v_ref, cv_ref, hy_ref, a_ref):
    rows, hs = _ffn1_then_norm(x_ref, mod_ref, g1_ref, g2_ref, wg_ref, wu_ref, wd_ref, o_ref, a_ref)
    off = 0
    for ref, width in zip((q_ref, k_ref, v_ref, cv_ref, hy_ref), _IN_SPLITS):
        for r, h in zip(rows, hs):
            ref[r, :] = _dot(h, w_ref[:, off:off + width])
        off += width


def _ffn_inproj_lat_kernel(x_ref, mod_ref, g1_ref, g2_ref, wg_ref, wu_ref, wd_ref, w_ref, cos_ref, sin_ref,
                           o_ref, q_ref, k_ref, v_ref, cv_ref, hy_ref, a_ref):
    wq, wk, wv, wc, wh = _IN_SPLITS
    tile = lambda a, width: jnp.concatenate([a] * (width // a.shape[1]), axis=1)
    rows, hs = _ffn1_then_norm(x_ref, mod_ref, g1_ref, g2_ref, wg_ref, wu_ref, wd_ref, o_ref, a_ref)
    for r, h in zip(rows, hs):
        cos = cos_ref[r, :]
        sin = sin_ref[r, :]
        q = _rope(_dot(h, w_ref[:, 0:wq]), tile(cos, wq), tile(sin, wq))
        q_ref[r, :] = (q * Q_SCALE).astype(BF16)
        k_ref[r, :] = _dup_kv_heads(_rope(_dot(h, w_ref[:, wq:wq + wk]), cos, sin))
    off = wq + wk
    for r, h in zip(rows, hs):
        v_ref[r, :] = _dup_kv_heads(_dot(h, w_ref[:, off:off + wv]))
    off += wv
    for r, h in zip(rows, hs):
        cv_ref[r, :] = _dot(h, w_ref[:, off:off + wc])
    off += wc
    for r, h in zip(rows, hs):
        hy_ref[r, :] = _dot(h, w_ref[:, off:off + wh])


def _ffn_inproj(x, mod4, g_ffn, g_mix, wg, wu, wd, w_in, rope_tables, *, l, row0, rows_per_cond, seq):
    t, d = x.shape
    tm = TOKEN_TILE
    lat = rope_tables is not None
    splits = _IN_SPLITS_LAT if lat else _IN_SPLITS
    dtypes = (BF16, BF16, BF16, F32, F32) if lat else (F32,) * 5
    gain_spec = pl.BlockSpec((None, 1, d), lambda i: (l, 0, 0))
    wspec_in = _resident((None, d, D_FF), lambda i: (l, 0, 0))
    in_specs = [
        pl.BlockSpec((tm, d), lambda i: (i, 0)),
        _mod_spec(l, row0, rows_per_cond, tm),
        gain_spec, gain_spec,
        wspec_in, wspec_in,
        _resident((None, D_FF, d), lambda i: (l, 0, 0)),
        _resident((None, d, sum(_IN_SPLITS)), lambda i: (l, 0, 0)),
    ]
    args = [x, mod4, g_ffn.reshape(DEPTH, 1, d), g_mix.reshape(DEPTH, 1, d), wg, wu, wd, w_in]
    if lat:
        table_spec = pl.BlockSpec((tm, LANE), lambda i: (i % (seq // tm), 0))
        in_specs += [table_spec, table_spec]
        args += list(rope_tables)
    tok = lambda w: pl.BlockSpec((tm, w), lambda i: (i, 0))
    return pl.pallas_call(
        _ffn_inproj_lat_kernel if lat else _ffn_inproj_kernel,
        grid=(t // tm,),
        in_specs=in_specs,
        out_specs=[tok(d)] + [tok(w) for w in splits],
        out_shape=[jax.ShapeDtypeStruct((t, d), F32)]
        + [jax.ShapeDtypeStruct((t, w), dt) for w, dt in zip(splits, dtypes)],
        scratch_shapes=[pltpu.VMEM((tm, D_FF), BF16)],
        compiler_params=_params(("arbitrary",), VMEM_LIMIT),
        name="ffn_inproj",
    )(*args)


def _dup_kv_heads(x):
    swapped = pltpu.roll(x, HEAD_DIM, 1)
    low = lax.broadcasted_iota(jnp.int32, x.shape, 1) < HEAD_DIM
    return jnp.concatenate([jnp.where(low, x, swapped), jnp.where(low, swapped, x)], axis=1).astype(BF16)


def _paired_attention(q, kds, vds, sink_ref, l, valid, o_ref):
    n = q.shape[0]
    top = lax.broadcasted_iota(jnp.int32, (2 * n, 1), 0) < n
    low = lax.broadcasted_iota(jnp.int32, (2 * n, LANE), 1) < HEAD_DIM
    zero = jnp.zeros((2 * n, LANE), BF16)
    for g, (kd, vd) in enumerate(zip(kds, vds)):
        c0 = 2 * g * LANE
        lhs = jnp.concatenate([q[:, c0:c0 + LANE], q[:, c0 + LANE:c0 + 2 * LANE]], axis=0)
        scores = (_dot_nt(jnp.where(low, lhs, zero), kd), _dot_nt(jnp.where(low, zero, lhs), kd))
        outs = []
        for half, s in enumerate(scores):
            if valid is not None:
                nloc = valid.shape[1]
                s = jnp.concatenate([jnp.where(valid, s[:, :nloc], NEG_INF), s[:, nloc:]], axis=1)
            sk = jnp.where(top, sink_ref[l, 4 * g + half], sink_ref[l, 4 * g + 2 + half]) * LOG2_E
            mx = jnp.maximum(jnp.max(s, axis=-1, keepdims=True), sk)
            p = jnp.exp2(s - mx)
            den = jnp.sum(p, axis=-1, keepdims=True) + jnp.exp2(sk - mx)
            outs.append(_dot(p.astype(BF16), vd) / den)
        o = jnp.where(low, outs[0], outs[1]).astype(BF16)
        o_ref[:, c0:c0 + LANE] = o[:n]
        o_ref[:, c0 + LANE:c0 + 2 * LANE] = o[n:]


def _ctx_attn_kernel(sink_ref, q_ref, k_ref, v_ref, o_ref, *, l):
    q = (q_ref[...] * Q_SCALE).astype(BF16)
    kd = _dup_kv_heads(k_ref[...])
    vd = _dup_kv_heads(v_ref[...])
    groups = [slice(g * LANE, (g + 1) * LANE) for g in range(KV_HEADS)]
    _paired_attention(q, [kd[:, gl] for gl in groups], [vd[:, gl] for gl in groups], sink_ref, l, None, o_ref)


def _ctx_attention(q, k, v, sink, *, l, batch, seq):
    return pl.pallas_call(
        functools.partial(_ctx_attn_kernel, l=l),
        grid=(batch,),
        in_specs=[
            pl.BlockSpec(memory_space=pltpu.SMEM),
            pl.BlockSpec((seq, ATTN_W), lambda b: (b, 0)),
            pl.BlockSpec((seq, KV_W), lambda b: (b, 0)),
            pl.BlockSpec((seq, KV_W), lambda b: (b, 0)),
        ],
        out_specs=pl.BlockSpec((seq, ATTN_W), lambda b: (b, 0)),
        out_shape=jax.ShapeDtypeStruct((batch * seq, ATTN_W), BF16),
        compiler_params=_params(("arbitrary",)),
        name="ctx_attention",
    )(sink, q, k, v)


def _lat_attn_kernel(sink_ref, q_ref, k_ref, v_ref, kc_ref, vc_ref, o_ref, kcd_ref, vcd_ref, *, l, seq):
    i = pl.program_id(1)

    @pl.when(i == 0)
    def _():
        kcd_ref[...] = _dup_kv_heads(kc_ref[...])
        vcd_ref[...] = _dup_kv_heads(vc_ref[...])

    nq = q_ref.shape[0]
    nloc = nq + 2 * WINDOW
    q0 = i * nq
    start = pl.multiple_of(jnp.clip(q0 - WINDOW, 0, seq - nloc), WINDOW)
    jpos = start + lax.broadcasted_iota(jnp.int32, (2 * nq, nloc), 1)
    row = lax.broadcasted_iota(jnp.int32, (2 * nq, nloc), 0)
    valid = jnp.abs(jpos - (q0 + row % nq)) <= WINDOW
    groups = [slice(g * LANE, (g + 1) * LANE) for g in range(KV_HEADS)]
    kds = [jnp.concatenate([k_ref[pl.ds(start, nloc), gl], kcd_ref[:, gl]], axis=0) for gl in groups]
    vds = [jnp.concatenate([v_ref[pl.ds(start, nloc), gl], vcd_ref[:, gl]], axis=0) for gl in groups]
    _paired_attention(q_ref[...], kds, vds, sink_ref, l, valid, o_ref)


def _lat_attention(q, k, v, cache_k, cache_v, sink, *, l, batch, seq):
    nq = LAT_QUERIES
    nb = seq // nq
    past = cache_k.shape[2]
    kv_spec = pl.BlockSpec((seq, 2 * KV_W), lambda b, i: (b, 0))
    cache_spec = pl.BlockSpec((None, None, past, KV_W), lambda b, i: (b, l, 0, 0))
    return pl.pallas_call(
        functools.partial(_lat_attn_kernel, l=l, seq=seq),
        grid=(batch, nb),
        in_specs=[
            pl.BlockSpec(memory_space=pltpu.SMEM),
            pl.BlockSpec((nq, ATTN_W), lambda b, i: (b * nb + i, 0)),
            kv_spec, kv_spec, cache_spec, cache_spec,
        ],
        out_specs=pl.BlockSpec((nq, ATTN_W), lambda b, i: (b * nb + i, 0)),
        out_shape=jax.ShapeDtypeStruct((batch * seq, ATTN_W), BF16),
        scratch_shapes=[pltpu.VMEM((past, 2 * KV_W), BF16), pltpu.VMEM((past, 2 * KV_W), BF16)],
        compiler_params=_params(("arbitrary", "arbitrary")),
        name="lat_attention",
    )(sink, q, k, v, cache_k, cache_v)


def _rope_tables(seq):
    rows = seq // GRID_W
    row = jnp.repeat(jnp.arange(rows), GRID_W)
    col = jnp.arange(rows * GRID_W) % GRID_W
    nf = HEAD_DIM // 4
    inv = ROPE_BASE ** (-jnp.arange(nf, dtype=F32) / nf)
    ang_r = row.astype(F32)[:, None] * inv[None, :]
    ang_c = col.astype(F32)[:, None] * inv[None, :]
    cos_h = jnp.concatenate([jnp.cos(ang_r)] * 2 + [jnp.cos(ang_c)] * 2, axis=1)
    sin_h = jnp.concatenate([-jnp.sin(ang_r), jnp.sin(ang_r), -jnp.sin(ang_c), jnp.sin(ang_c)], axis=1)
    return jnp.tile(cos_h, (1, KV_HEADS)), jnp.tile(sin_h, (1, KV_HEADS))


def _conv_kernel(cv_ref, dw_ref, dwb_ref, lng_ref, lnb_ref, pw_ref, o_ref, pad_ref, *, seq):
    x = cv_ref[...]
    zeros = jnp.zeros((CONV_PAD, CONV_W), F32)
    pad_ref[0:CONV_PAD, :] = zeros
    pad_ref[CONV_PAD:CONV_PAD + seq, :] = x[:, :CONV_W] * jax.nn.sigmoid(x[:, CONV_W:])
    pad_ref[CONV_PAD + seq:2 * CONV_PAD + seq, :] = zeros
    dw = dw_ref[...]
    pw = pw_ref[...].astype(BF16)
    first = CONV_PAD - CONV_K // 2

    def chunk(c, carry):
        base = pl.multiple_of(c * CONV_CHUNK, CONV_CHUNK)
        win = pad_ref[pl.ds(base, CONV_CHUNK + 2 * CONV_PAD), :]
        acc = jnp.zeros((CONV_CHUNK, CONV_W), F32)
        for r in range(SUBLANE):
            offs = [first + k for k in range(CONV_K) if (first + k) % SUBLANE == r]
            if not offs:
                continue
            shifted = win if r == 0 else pltpu.roll(win, win.shape[0] - r, 0)
            for off in offs:
                acc = acc + shifted[off - r:off - r + CONV_CHUNK] * dw[off - first:off - first + 1]
        acc = acc + dwb_ref[...]
        mu = jnp.mean(acc, axis=-1, keepdims=True)
        cen = acc - mu
        var = jnp.mean(cen * cen, axis=-1, keepdims=True)
        y = cen * lax.rsqrt(var + EPS) * lng_ref[...] + lnb_ref[...]
        o_ref[pl.ds(base, CONV_CHUNK), :] = _dot(_silu(y).astype(BF16), pw).astype(BF16)
        return carry

    lax.fori_loop(0, seq // CONV_CHUNK, chunk, 0, unroll=2)


def _conformer_conv(cv, dw, dwb, lng, lnb, pw, *, l, batch, seq):
    kpad = 2 * CONV_PAD
    dw_p = jnp.pad(dw, ((0, 0), (0, kpad - CONV_K), (0, 0)))
    vec = lambda a: a.reshape(DEPTH, 1, CONV_W)
    vspec = pl.BlockSpec((None, 1, CONV_W), lambda b: (l, 0, 0))
    return pl.pallas_call(
        functools.partial(_conv_kernel, seq=seq),
        grid=(batch,),
        in_specs=[
            pl.BlockSpec((seq, 2 * CONV_W), lambda b: (b, 0)),
            pl.BlockSpec((None, kpad, CONV_W), lambda b: (l, 0, 0)),
            vspec, vspec, vspec,
            pl.BlockSpec((None, CONV_W, CONV_W), lambda b: (l, 0, 0)),
        ],
        out_specs=pl.BlockSpec((seq, CONV_W), lambda b: (b, 0)),
        out_shape=jax.ShapeDtypeStruct((batch * seq, CONV_W), BF16),
        scratch_shapes=[pltpu.VMEM((seq + 2 * CONV_PAD, CONV_W), F32)],
        compiler_params=_params(("arbitrary",)),
        name="conformer_conv",
    )(cv, dw_p, vec(dwb), vec(lng), vec(lnb), pw)


def _short_conv3(x, w, b):
    n = x.shape[0]
    row = lax.broadcasted_iota(jnp.int32, x.shape, 0)
    prev = jnp.where(row == 0, 0.0, pltpu.roll(x, 1, 0))
    nxt = jnp.where(row == n - 1, 0.0, pltpu.roll(x, n - 1, 0))
    return prev * w[0:1] + x * w[1:2] + nxt * w[2:3] + b


def _filter_kernel(z_ref, w1_ref, b1_ref, f1_ref, w2_ref, b2_ref, f2_ref, w3_ref, ld_ref, o_ref):
    z = z_ref[...]
    h = jnp.sin(f1_ref[...] * (_dot_hi(z, w1_ref[...]) + b1_ref[...]))
    h = jnp.sin(f2_ref[...] * (_dot_hi(h, w2_ref[...]) + b2_ref[...]))
    h = _dot_hi(h, w3_ref[...])
    tn = z[:, 0:1]
    h = h * jnp.exp(-tn * jnp.exp(ld_ref[...]))
    ss = jnp.sum(h * h, axis=0, keepdims=True)
    c = HYENA_W
    scale = []
    for o in range(2):
        tot = ss[:, 2 * o * c:(2 * o + 1) * c] + ss[:, (2 * o + 1) * c:(2 * o + 2) * c]
        r = lax.rsqrt(tot + EPS)
        scale += [r, r]
    h = h * jnp.concatenate(scale, axis=1)
    row = lax.broadcasted_iota(jnp.int32, h.shape, 0)
    col = lax.broadcasted_iota(jnp.int32, h.shape, 1)
    o_ref[...] = jnp.where((row == 0) & ((col // c) % 2 == 1), 0.0, h)


def _hyena_features(seq):
    t = jnp.arange(seq, dtype=F32)
    tn = t / (seq - 1)
    bands = jnp.linspace(1e-4, HYENA_BANDS - 1, HYENA_BANDS, dtype=F32)
    ang = 2.0 * math.pi * t[:, None] * bands[None, :] / seq
    z = jnp.concatenate([tn[:, None], jnp.cos(ang), -jnp.sin(ang)], axis=-1)
    return jnp.pad(z, ((0, 0), (0, LANE - HYENA_EMB)))


def _hyena_filters(z, w1, b1, f1, w2, b2, f2, w3, log_decay):
    seq = z.shape[0]
    n = w3.shape[1]
    return pl.pallas_call(
        _filter_kernel,
        out_shape=jax.ShapeDtypeStruct((seq, n), F32),
        compiler_params=_params(None, VMEM_LIMIT),
        name="hyena_filters",
    )(z, w1, b1, f1, w2, b2, f2, w3, log_decay)


def _spectrum_kernel(m_ref, h_ref, lr_ref, li_ref, hr_ref, hi_ref, mid_ref, *, n_fft):
    j = pl.program_id(0)
    fc = m_ref.shape[1]
    c = HYENA_W
    h16 = h_ref[...].astype(BF16)
    h = h16.astype(F32)
    hh = jnp.concatenate([h16, _alternate_rows(h).astype(BF16)], axis=1)
    gc = _dot(m_ref[0], hh)
    gs = _dot(m_ref[1], hh)
    row = j * fc + lax.broadcasted_iota(jnp.int32, (fc, c), 0)
    wgt = jnp.where(row == 0, 1.0 / n_fft, 2.0 / n_fft)
    t4 = lax.broadcasted_iota(jnp.int32, h.shape, 0) % 4
    mid_c = jnp.sum(jnp.where(t4 == 0, h, jnp.where(t4 == 2, -h, 0.0)), axis=0, keepdims=True)
    mid_s = jnp.sum(jnp.where(t4 == 3, h, jnp.where(t4 == 1, -h, 0.0)), axis=0, keepdims=True)
    for o in range(2):
        f = slice(2 * o * c, (2 * o + 1) * c)
        b = slice((2 * o + 1) * c, (2 * o + 2) * c)
        fa = slice(4 * c + 2 * o * c, 4 * c + (2 * o + 1) * c)
        ba = slice(4 * c + (2 * o + 1) * c, 4 * c + (2 * o + 2) * c)
        lr_ref[o] = (gc[:, f] + gc[:, b]) * wgt
        li_ref[o] = (gs[:, f] - gs[:, b]) * wgt
        hr_ref[o] = (gc[:, fa] + gc[:, ba]) * wgt
        hi_ref[o] = (gs[:, ba] - gs[:, fa]) * wgt
        mid_ref[o] = jnp.concatenate([mid_c[:, f] + mid_c[:, b], mid_s[:, f] - mid_s[:, b]],
                                     axis=0) * (2.0 / n_fft)


def _filter_spectra(dft, hcat, *, fc):
    seq = dft.shape[1]
    half = seq // 2
    c = HYENA_W
    fc = min(fc, half)
    out_spec = pl.BlockSpec((2, fc, c), lambda j: (0, j, 0))
    shape = jax.ShapeDtypeStruct((2, half, c), F32)
    return pl.pallas_call(
        functools.partial(_spectrum_kernel, n_fft=2 * seq),
        grid=(half // fc,),
        in_specs=[
            pl.BlockSpec((2, fc, seq), lambda j: (0, j, 0)),
            _resident(hcat.shape, lambda j: (0, 0)),
        ],
        out_specs=[out_spec] * 4 + [pl.BlockSpec((2, 2, c), lambda j: (0, 0, 0))],
        out_shape=[shape] * 4 + [jax.ShapeDtypeStruct((2, 2, c), F32)],
        compiler_params=_params(("arbitrary",), VMEM_LIMIT),
        name="hyena_spectra",
    )(dft, hcat)


def _alternate_rows(x):
    row = lax.broadcasted_iota(jnp.int32, x.shape, 0)
    return jnp.where(row % 2 == 0, x, -x)


def _longconv_split(ue, uo, m_ref, tw_ref, lr_ref, li_ref, hr_ref, hi_ref, mid_ref):
    shape = ue.shape
    tile = lambda a: jnp.concatenate([a] * (shape[1] // a.shape[1]), axis=1)
    wc = tile(tw_ref[0])
    ws = tile(tw_ref[1])
    ue16 = ue.astype(BF16)
    uo16 = uo.astype(BF16)
    er = _dot(m_ref[0], ue16)
    ei = _dot(m_ref[1], ue16)
    orr = _dot(m_ref[0], uo16)
    oi = _dot(m_ref[1], uo16)
    sign = _alternate_rows(jnp.ones(shape, F32))
    en = jnp.sum(ue16.astype(F32) * sign, axis=0, keepdims=True)
    on = jnp.sum(uo16.astype(F32) * sign, axis=0, keepdims=True)
    tr = wc * orr + ws * oi
    ti = wc * oi - ws * orr
    pr, pi, mr, mi = er + tr, ei + ti, er - tr, ei - ti
    lr, li, hr, hi = lr_ref[...], li_ref[...], hr_ref[...], hi_ref[...]
    ylr = pr * lr - pi * li
    yli = pr * li + pi * lr
    yhr = mr * hr + mi * hi
    yhi = mr * hi - mi * hr
    dr = ylr - yhr
    di = yli + yhi
    ger = (ylr + yhr).astype(BF16)
    gei = (yli - yhi).astype(BF16)
    gor = (dr * wc - di * ws).astype(BF16)
    goi = (dr * ws + di * wc).astype(BF16)
    ym_r = en * mid_ref[0:1] + on * mid_ref[1:2]
    ym_i = en * mid_ref[1:2] - on * mid_ref[0:1]
    ye = _dot(m_ref[0], ger) + _dot(m_ref[1], gei) + sign * ym_r
    yo = _dot(m_ref[0], gor) + _dot(m_ref[1], goi) - sign * ym_i
    return ye, yo


def _deinterleave(x, buf_ref):
    half = x.shape[0] // 2
    groups = buf_ref.shape[0]
    for g in range(groups):
        buf_ref[g] = x[:, g * LANE:(g + 1) * LANE]
    pick = lambda first: jnp.concatenate(
        [buf_ref[g, pl.ds(first, half, stride=2), :] for g in range(groups)], axis=1)
    return pick(0), pick(1)


def _interleave(even, odd, buf_ref):
    half = even.shape[0]
    groups = buf_ref.shape[0]
    for g in range(groups):
        buf_ref[g, pl.ds(0, half, stride=2), :] = even[:, g * LANE:(g + 1) * LANE]
        buf_ref[g, pl.ds(1, half, stride=2), :] = odd[:, g * LANE:(g + 1) * LANE]
    return jnp.concatenate([buf_ref[g] for g in range(groups)], axis=1)


def _hyena_conv0_kernel(hv_ref, hx_ref, sw_ref, sb_ref, bias_ref, m_ref, tw_ref, lr_ref, li_ref, hr_ref, hi_ref,
                        mid_ref, y_ref, vbuf_ref, xbuf_ref, *, seq):
    c = HYENA_W
    half = seq // 2
    for bi in range(hv_ref.shape[0] // seq):
        rows = slice(bi * seq, (bi + 1) * seq)
        ve, vo = _deinterleave(_short_conv3(hv_ref[rows, :], sw_ref[:, 0:c], sb_ref[:, 0:c]), vbuf_ref)
        xe, xo = _deinterleave(_short_conv3(hx_ref[rows, :], sw_ref[:, c:2 * c], sb_ref[:, c:2 * c]), xbuf_ref)
        ce, co = _longconv_split(ve, vo, m_ref, tw_ref, lr_ref, li_ref, hr_ref, hi_ref, mid_ref)
        y_ref[bi * seq:bi * seq + half, :] = xe * (ce + bias_ref[0:1] * ve)
        y_ref[bi * seq + half:(bi + 1) * seq, :] = xo * (co + bias_ref[0:1] * vo)


def _hyena_conv1_kernel(y_ref, hx_ref, sw_ref, sb_ref, bias_ref, m_ref, tw_ref, lr_ref, li_ref, hr_ref, hi_ref,
                        mid_ref, o_ref, xbuf_ref, obuf_ref, *, seq):
    c = HYENA_W
    half = seq // 2
    for bi in range(y_ref.shape[0] // seq):
        rows = slice(bi * seq, (bi + 1) * seq)
        ye = y_ref[bi * seq:bi * seq + half, :]
        yo = y_ref[bi * seq + half:(bi + 1) * seq, :]
        xe, xo = _deinterleave(_short_conv3(hx_ref[rows, :], sw_ref[:, 2 * c:3 * c], sb_ref[:, 2 * c:3 * c]),
                               xbuf_ref)
        ce, co = _longconv_split(ye, yo, m_ref, tw_ref, lr_ref, li_ref, hr_ref, hi_ref, mid_ref)
        out = _interleave(xe * (ce + bias_ref[1:2] * ye), xo * (co + bias_ref[1:2] * yo), obuf_ref)
        o_ref[rows, :] = out.astype(BF16)


def _twiddles(seq):
    f = np.arange(seq // 2, dtype=np.float64)[:, None] * np.ones((1, LANE))
    ang = 2.0 * np.pi * f / (2 * seq)
    return jnp.asarray(np.stack([np.cos(ang), np.sin(ang)]), F32)


def _hyena_mixer(hy, sw, sb, bias, dft, twiddle, spectra, *, l, batch, seq, bg):
    c = HYENA_W
    t = batch * seq
    rows = bg * seq
    sw_p = jnp.pad(sw, ((0, 0), (0, SUBLANE - sw.shape[1]), (0, 0)))
    sb3 = sb.reshape(DEPTH, 1, 3 * c)
    chan = lambda ch: pl.BlockSpec((rows, c), lambda g: (g, ch))
    sw_spec = pl.BlockSpec((None, SUBLANE, 3 * c), lambda g: (l, 0, 0))
    sb_spec = pl.BlockSpec((None, 1, 3 * c), lambda g: (l, 0, 0))
    bias_spec = pl.BlockSpec((None, 2, c), lambda g: (l, 0, 0))
    half = seq // 2
    dft_spec = _resident(dft.shape, lambda g: (0, 0, 0))
    tw_spec = _resident(twiddle.shape, lambda g: (0, 0, 0))
    coef = lambda order: ([_resident((None, half, c), lambda g: (order, 0, 0))] * 4
                          + [pl.BlockSpec((None, 2, c), lambda g: (order, 0, 0))])
    params = _params(("arbitrary",), VMEM_LIMIT)
    seq_buf = pltpu.VMEM((c // LANE, seq, LANE), F32)

    y = pl.pallas_call(
        functools.partial(_hyena_conv0_kernel, seq=seq),
        grid=(batch // bg,),
        in_specs=[chan(0), chan(1), sw_spec, sb_spec, bias_spec, dft_spec, tw_spec] + coef(0),
        out_specs=chan(0),
        out_shape=jax.ShapeDtypeStruct((t, c), F32),
        scratch_shapes=[seq_buf, seq_buf],
        compiler_params=params,
        name="hyena_conv0",
    )(hy, hy, sw_p, sb3, bias, dft, twiddle, *spectra)

    return pl.pallas_call(
        functools.partial(_hyena_conv1_kernel, seq=seq),
        grid=(batch // bg,),
        in_specs=[chan(0), chan(2), sw_spec, sb_spec, bias_spec, dft_spec, tw_spec] + coef(1),
        out_specs=chan(0),
        out_shape=jax.ShapeDtypeStruct((t, c), BF16),
        scratch_shapes=[seq_buf, seq_buf],
        compiler_params=params,
        name="hyena_conv1",
    )(y, hy, sw_p, sb3, bias, dft, twiddle, *spectra)


def _dft_kernel(c1_ref, s1_ref, c0_ref, s0_ref, o_ref):
    c0 = c0_ref[...]
    s0 = s0_ref[...]
    for i in range(c1_ref.shape[0]):
        c1 = c1_ref[i:i + 1, :]
        s1 = s1_ref[i:i + 1, :]
        rows = slice(i * DFT_MINOR, (i + 1) * DFT_MINOR)
        o_ref[0, rows, :] = (c1 * c0 - s1 * s0).astype(BF16)
        o_ref[1, rows, :] = (-(s1 * c0 + c1 * s0)).astype(BF16)


def _dft_matrix(seq):
    n = 2 * seq
    t = np.arange(seq, dtype=np.int64)
    f1n = seq // DFT_MINOR
    ang1 = 2.0 * np.pi * ((np.arange(f1n)[:, None] * DFT_MINOR * t[None, :]) % n).astype(np.float64) / n
    ang0 = 2.0 * np.pi * ((np.arange(DFT_MINOR)[:, None] * t[None, :]) % n).astype(np.float64) / n
    tables = [jnp.asarray(f(a), F32) for a in (ang1, ang0) for f in (np.cos, np.sin)]
    nb = min(f1n, SUBLANE)
    coarse = pl.BlockSpec((nb, seq), lambda j: (j, 0))
    fine = pl.BlockSpec((DFT_MINOR, seq), lambda j: (0, 0))
    return pl.pallas_call(
        _dft_kernel,
        grid=(f1n // nb,),
        in_specs=[coarse, coarse, fine, fine],
        out_specs=pl.BlockSpec((2, nb * DFT_MINOR, seq), lambda j: (0, j, 0)),
        out_shape=jax.ShapeDtypeStruct((2, seq, seq), BF16),
        compiler_params=_params(("arbitrary",)),
        name="dft_matrix",
    )(*tables)


def _mixffn_kernel(x_ref, att_ref, cvo_ref, hyo_ref, mod_ref, wo_ref, g_ref,
                   wg_ref, wu_ref, wd_ref, gf_ref, o_ref, a_ref, *, final):
    m = mod_ref[...]
    o1 = ATTN_W
    o2 = ATTN_W + CONV_W
    rows = _row_halves(x_ref.shape[0])
    xs, hs = [], []
    for r in rows:
        mix = (_dot(att_ref[r, :], wo_ref[0:o1, :]) + _dot(cvo_ref[r, :], wo_ref[o1:o2, :])
               + _dot(hyo_ref[r, :], wo_ref[o2:, :]))
        x = x_ref[r, :] + m[5:6] * mix
        xs.append(x)
        hs.append(_mod_norm(x, g_ref[...], m[7:8], m[6:7]).astype(BF16))
    ys = _swiglu_chains(hs, rows, wg_ref, wu_ref, wd_ref, a_ref)
    for r, x, y in zip(rows, xs, ys):
        x = x + (0.5 * m[8:9]) * y
        if final:
            x = x * lax.rsqrt(jnp.mean(x * x, axis=-1, keepdims=True) + EPS) * gf_ref[...]
        o_ref[r, :] = x


def _mixffn(x, att, cvo, hyo, mod4, w_out, gain, wg, wu, wd, g_final, *, l, row0, rows_per_cond, final):
    t, d = x.shape
    tm = TOKEN_TILE
    tok = lambda w: pl.BlockSpec((tm, w), lambda i: (i, 0))
    wspec_in = _resident((None, d, D_FF), lambda i: (l, 0, 0))
    return pl.pallas_call(
        functools.partial(_mixffn_kernel, final=final),
        grid=(t // tm,),
        in_specs=[
            tok(d), tok(ATTN_W), tok(CONV_W), tok(HYENA_W),
            _mod_spec(l, row0, rows_per_cond, tm),
            _resident((None, d, d), lambda i: (l, 0, 0)),
            pl.BlockSpec((None, 1, d), lambda i: (l, 0, 0)),
            wspec_in, wspec_in,
            _resident((None, D_FF, d), lambda i: (l, 0, 0)),
            pl.BlockSpec((1, d), lambda i: (0, 0)),
        ],
        out_specs=tok(d),
        out_shape=jax.ShapeDtypeStruct((t, d), F32),
        scratch_shapes=[pltpu.VMEM((tm, D_FF), BF16)],
        compiler_params=_params(("arbitrary",), VMEM_LIMIT),
        name="mixffn",
    )(x, att, cvo, hyo, mod4, w_out, gain.reshape(DEPTH, 1, d), wg, wu, wd, g_final.reshape(1, d))


def _pad_cols(a, width):
    return jnp.pad(a, [(0, 0)] * (a.ndim - 1) + [(0, width - a.shape[-1])])


def _trunk(x, cache, mod4, P, *, batch, seq, row0, rows_per_cond):
    fc = min(seq, FREQ_CHUNK)
    bg = max(1, min(batch, LONGCONV_ROWS // seq))
    dft = _dft_matrix(seq)
    dft_half = _dft_matrix(seq // 2)
    twiddle = _twiddles(seq)
    z = _hyena_features(seq)
    rope_tables = None if cache is None else _rope_tables(seq)
    w_in = P['w_in']
    cond = dict(row0=row0, rows_per_cond=rows_per_cond)
    ks, vs = [], []
    for l in range(DEPTH):
        x, q, k, v, cv, hy = _ffn_inproj(x, mod4, P['g_ffn1'], P['g_mix'], P['w1_gate'], P['w1_up'],
                                         P['w1_down'], w_in, rope_tables, l=l, seq=seq, **cond)
        if cache is None:
            att = _ctx_attention(q, k, v, P['attn_sink'], l=l, batch=batch, seq=seq)
            ks.append(k)
            vs.append(v)
        else:
            att = _lat_attention(q, k, v, cache[0], cache[1], P['attn_sink'], l=l, batch=batch, seq=seq)
        cvo = _conformer_conv(cv, P['conv_dw'], P['conv_dw_b'], P['conv_ln_g'], P['conv_ln_b'],
                              P['conv_pw'], l=l, batch=batch, seq=seq)
        hcat = _hyena_filters(z, P['hy_w1'][l], P['hy_b1'][l], P['hy_f1'][l], P['hy_w2'][l],
                              P['hy_b2'][l], P['hy_f2'][l], P['hy_w3'][l], P['hy_log_decay'][l])
        spectra = _filter_spectra(dft, hcat, fc=fc)
        hyo = _hyena_mixer(hy, P['hy_short_w'], P['hy_short_b'], P['hy_bias'], dft_half, twiddle, spectra,
                           l=l, batch=batch, seq=seq, bg=bg)
        x = _mixffn(x, att, cvo, hyo, mod4, P['w_out'], P['g_ffn2'], P['w2_gate'], P['w2_up'],
                    P['w2_down'], P['g_final'], l=l, final=(l == DEPTH - 1), **cond)
    return x, ks, vs


def kernel(x_prompt, x_sample, c, cache_k, cache_v, c_ctx, w_mod, b_mod, g_ffn1, g_mix, g_ffn2,
           g_final, w1_gate, w1_up, w1_down, w2_gate, w2_up, w2_down, w_in, w_out, attn_sink,
           conv_dw, conv_dw_b, conv_ln_g, conv_ln_b, conv_pw, hy_short_w, hy_short_b,
           hy_w1, hy_b1, hy_f1, hy_w2, hy_b2, hy_f2, hy_w3, hy_log_decay, hy_bias):
    batch, seq, d = x_prompt.shape
    dec_batch, dec_seq, _ = x_sample.shape
    assert 1 + dec_batch <= COND_ROWS

    hid = LANE
    P = dict(
        g_ffn1=g_ffn1, g_mix=g_mix, g_ffn2=g_ffn2, g_final=g_final, attn_sink=attn_sink,
        w1_gate=w1_gate.astype(BF16), w1_up=w1_up.astype(BF16), w1_down=w1_down.astype(BF16),
        w2_gate=w2_gate.astype(BF16), w2_up=w2_up.astype(BF16), w2_down=w2_down.astype(BF16),
        w_in=w_in.astype(BF16), w_out=w_out.astype(BF16),
        conv_dw=conv_dw, conv_dw_b=conv_dw_b, conv_ln_g=conv_ln_g, conv_ln_b=conv_ln_b, conv_pw=conv_pw,
        hy_short_w=hy_short_w, hy_short_b=hy_short_b,
        hy_w1=jnp.pad(hy_w1, ((0, 0), (0, hid - HYENA_EMB), (0, hid - HYENA_HID))),
        hy_b1=_pad_cols(hy_b1, hid)[:, None, :], hy_f1=_pad_cols(hy_f1, hid)[:, None, :],
        hy_w2=jnp.pad(hy_w2, ((0, 0), (0, hid - HYENA_HID), (0, hid - HYENA_HID))),
        hy_b2=_pad_cols(hy_b2, hid)[:, None, :], hy_f2=_pad_cols(hy_f2, hid)[:, None, :],
        hy_w3=jnp.pad(hy_w3, ((0, 0), (0, hid - HYENA_HID), (0, 0))),
        hy_log_decay=hy_log_decay[:, None, :], hy_bias=hy_bias,
    )

    conds = jnp.concatenate([c_ctx[None, :], c, jnp.zeros((COND_ROWS - 1 - dec_batch, d), F32)], axis=0)
    mod = _modulation(conds, w_mod, b_mod)
    mod4 = mod.reshape(DEPTH, COND_ROWS, N_MOD, d)

    xp, ks, vs = _trunk(x_prompt.reshape(batch * seq, d), None, mod4, P,
                        batch=batch, seq=seq, row0=0, rows_per_cond=batch * seq)
    cache = (cache_k.reshape(*cache_k.shape[:3], KV_W), cache_v.reshape(*cache_v.shape[:3], KV_W))
    xs, _, _ = _trunk(x_sample.reshape(dec_batch * dec_seq, d), cache, mod4, P,
                      batch=dec_batch, seq=dec_seq, row0=1, rows_per_cond=dec_seq)

    kv_shape = (batch, DEPTH, seq, KV_HEADS, HEAD_DIM)
    new_k = jnp.stack([k.reshape(batch, seq, KV_W) for k in ks], axis=1).reshape(kv_shape)
    new_v = jnp.stack([v.reshape(batch, seq, KV_W) for v in vs], axis=1).reshape(kv_shape)
    return (xp.reshape(batch, seq, d), xs.reshape(dec_batch, dec_seq, d), new_k, new_v)
```

```python
import functools
import math

import numpy as np
import jax
import jax.numpy as jnp
from jax import lax
from jax.experimental import pallas as pl
from jax.experimental.pallas import tpu as pltpu

F32 = jnp.float32
BF16 = jnp.bfloat16

D_MODEL = 1024
DEPTH = 2
GRID_W = 64
N_HEADS = 8
KV_HEADS = 2
HEAD_DIM = 64
Q_PER_KV = N_HEADS // KV_HEADS
ATTN_W = N_HEADS * HEAD_DIM
KV_W = KV_HEADS * HEAD_DIM
CONV_W = D_MODEL // 4
HYENA_W = D_MODEL // 4
WINDOW = 128
BLOCK = 128
CONV_K = 31
HYENA_EMB = 33
HYENA_BANDS = (HYENA_EMB - 1) // 2
HYENA_HID = 64
D_FF = 2816
N_MOD = 9
ROPE_BASE = 10000.0
EPS = 1e-6
NEG_INF = -1e30
LOG2_E = math.log2(math.e)
Q_SCALE = HEAD_DIM ** -0.5 * LOG2_E

LANE = 128
SUBLANE = 8
MXU_W = 256
COND_ROWS = 16
TOKEN_TILE = 512
CONV_CHUNK = 128
CONV_PAD = 16
DFT_MINOR = 64
FREQ_CHUNK = 512
LONGCONV_ROWS = 4096
LONGCONV_SEQS = 4
LAT_QUERIES = 256
VMEM_LIMIT = 56 * 1024 * 1024


def _params(sem, vmem=None):
    return pltpu.CompilerParams(dimension_semantics=sem, vmem_limit_bytes=vmem)


def _silu(x):
    return x * jax.nn.sigmoid(x)


def _dot(a, b):
    return jnp.dot(a, b, preferred_element_type=F32)


def _dot_nt(a, b):
    return lax.dot_general(a, b, (((1,), (1,)), ((), ())), preferred_element_type=F32)


def _dot_hi(a, b):
    return jnp.dot(a, b, preferred_element_type=F32, precision=lax.Precision.HIGHEST)


def _mod_norm(x, gain, scale, shift):
    y = x * lax.rsqrt(jnp.mean(x * x, axis=-1, keepdims=True) + EPS)
    return (y * gain) * (1.0 + scale) + shift


def _mod_kernel(c_ref, w_ref, b_ref, o_ref):
    s = _silu(c_ref[...]).astype(BF16)
    o_ref[...] = _dot(s, w_ref[...].astype(BF16)) + b_ref[...]


def _modulation(conds, w_mod, b_mod):
    d = D_MODEL
    return pl.pallas_call(
        _mod_kernel,
        grid=(DEPTH, N_MOD),
        in_specs=[
            pl.BlockSpec((COND_ROWS, d), lambda l, j: (0, 0)),
            pl.BlockSpec((None, d, d), lambda l, j: (l, 0, j)),
            pl.BlockSpec((None, 1, d), lambda l, j: (l, 0, j)),
        ],
        out_specs=pl.BlockSpec((None, COND_ROWS, d), lambda l, j: (l, 0, j)),
        out_shape=jax.ShapeDtypeStruct((DEPTH, COND_ROWS, N_MOD * d), F32),
        compiler_params=_params(("arbitrary", "arbitrary")),
        name="modulation",
    )(conds, w_mod, b_mod.reshape(DEPTH, 1, N_MOD * d))


def _row_halves(n):
    return slice(0, n // 2), slice(n // 2, n)


def _swiglu_chains(hs, rows, wg_ref, wu_ref, wd_ref, a_ref):
    for c in range(D_FF // MXU_W):
        sl = slice(c * MXU_W, (c + 1) * MXU_W)
        for r, h in zip(rows, hs):
            a_ref[r, sl] = (_silu(_dot(h, wg_ref[:, sl])) * _dot(h, wu_ref[:, sl])).astype(BF16)
    return [_dot(a_ref[r, :], wd_ref[...]) for r in rows]


def _ffn1_then_norm(x_ref, mod_ref, g1_ref, g2_ref, wg_ref, wu_ref, wd_ref, o_ref, a_ref):
    m = mod_ref[...]
    rows = _row_halves(x_ref.shape[0])
    xs = [x_ref[r, :] for r in rows]
    hs = [_mod_norm(x, g1_ref[...], m[1:2], m[0:1]).astype(BF16) for x in xs]
    ys = _swiglu_chains(hs, rows, wg_ref, wu_ref, wd_ref, a_ref)
    out = []
    for r, x, y in zip(rows, xs, ys):
        x = x + (0.5 * m[2:3]) * y
        o_ref[r, :] = x
        out.append(_mod_norm(x, g2_ref[...], m[4:5], m[3:4]).astype(BF16))
    return rows, out


def _resident(shape, index_map):
    return pl.BlockSpec(shape, index_map, pipeline_mode=pl.Buffered(1))


def _mod_spec(l, row0, rows_per_cond, tm):
    return pl.BlockSpec((None, None, N_MOD, D_MODEL),
                        lambda i: (l, row0 + (i * tm) // rows_per_cond, 0, 0))


_IN_SPLITS = (ATTN_W, KV_W, KV_W, 2 * CONV_W, 3 * HYENA_W)
_IN_SPLITS_LAT = (ATTN_W, 2 * KV_W, 2 * KV_W, 2 * CONV_W, 3 * HYENA_W)


def _rope(x, cos, sin_signed):
    w = x.shape[-1]
    lane = lax.broadcasted_iota(jnp.int32, x.shape, 1)
    quarter = HEAD_DIM // 4
    partner = jnp.where(lane % (2 * quarter) < quarter,
                        pltpu.roll(x, w - quarter, 1), pltpu.roll(x, quarter, 1))
    return x * cos + partner * sin_signed


def _ffn_inproj_kernel(x_ref, mod_ref, g1_ref, g2_ref, wg_ref, wu_ref, wd_ref, w_ref,
                       o_ref, q_ref, k_ref, v_ref, cv_ref, hy_ref, a_ref):
    rows, hs = _ffn1_then_norm(x_ref, mod_ref, g1_ref, g2_ref, wg_ref, wu_ref, wd_ref, o_ref, a_ref)
    off = 0
    for ref, width in zip((q_ref, k_ref, v_ref, cv_ref, hy_ref), _IN_SPLITS):
        for r, h in zip(rows, hs):
            ref[r, :] = _dot(h, w_ref[:, off:off + width])
        off += width


def _ffn_inproj_lat_kernel(x_ref, mod_ref, g1_ref, g2_ref, wg_ref, wu_ref, wd_ref, w_ref, cos_ref, sin_ref,
                           o_ref, q_ref, k_ref, v_ref, cv_ref, hy_ref, a_ref):
    wq, wk, wv, wc, wh = _IN_SPLITS
    tile = lambda a, width: jnp.concatenate([a] * (width // a.shape[1]), axis=1)
    rows, hs = _ffn1_then_norm(x_ref, mod_ref, g1_ref, g2_ref, wg_ref, wu_ref, wd_ref, o_ref, a_ref)
    for r, h in zip(rows, hs):
        cos = cos_ref[r, :]
        sin = sin_ref[r, :]
        q = _rope(_dot(h, w_ref[:, 0:wq]), tile(cos, wq), tile(sin, wq))
        q_ref[r, :] = (q * Q_SCALE).astype(BF16)
        k_ref[r, :] = _dup_kv_heads(_rope(_dot(h, w_ref[:, wq:wq + wk]), cos, sin))
    off = wq + wk
    for r, h in zip(rows, hs):
        v_ref[r, :] = _dup_kv_heads(_dot(h, w_ref[:, off:off + wv]))
    off += wv
    for r, h in zip(rows, hs):
        cv_ref[r, :] = _dot(h, w_ref[:, off:off + wc])
    off += wc
    for r, h in zip(rows, hs):
        hy_ref[r, :] = _dot(h, w_ref[:, off:off + wh])


def _ffn_inproj(x, mod4, g_ffn, g_mix, wg, wu, wd, w_in, rope_tables, *, l, row0, rows_per_cond, seq):
    t, d = x.shape
    tm = TOKEN_TILE
    lat = rope_tables is not None
    splits = _IN_SPLITS_LAT if lat else _IN_SPLITS
    dtypes = (BF16, BF16, BF16, F32, F32) if lat else (F32,) * 5
    gain_spec = pl.BlockSpec((None, 1, d), lambda i: (l, 0, 0))
    wspec_in = _resident((None, d, D_FF), lambda i: (l, 0, 0))
    in_specs = [
        pl.BlockSpec((tm, d), lambda i: (i, 0)),
        _mod_spec(l, row0, rows_per_cond, tm),
        gain_spec, gain_spec,
        wspec_in, wspec_in,
        _resident((None, D_FF, d), lambda i: (l, 0, 0)),
        _resident((None, d, sum(_IN_SPLITS)), lambda i: (l, 0, 0)),
    ]
    args = [x, mod4, g_ffn.reshape(DEPTH, 1, d), g_mix.reshape(DEPTH, 1, d), wg, wu, wd, w_in]
    if lat:
        table_spec = pl.BlockSpec((tm, LANE), lambda i: (i % (seq // tm), 0))
        in_specs += [table_spec, table_spec]
        args += list(rope_tables)
    tok = lambda w: pl.BlockSpec((tm, w), lambda i: (i, 0))
    return pl.pallas_call(
        _ffn_inproj_lat_kernel if lat else _ffn_inproj_kernel,
        grid=(t // tm,),
        in_specs=in_specs,
        out_specs=[tok(d)] + [tok(w) for w in splits],
        out_shape=[jax.ShapeDtypeStruct((t, d), F32)]
        + [jax.ShapeDtypeStruct((t, w), dt) for w, dt in zip(splits, dtypes)],
        scratch_shapes=[pltpu.VMEM((tm, D_FF), BF16)],
        compiler_params=_params(("arbitrary",), VMEM_LIMIT),
        name="ffn_inproj",
    )(*args)


def _dup_kv_heads(x):
    swapped = pltpu.roll(x, HEAD_DIM, 1)
    low = lax.broadcasted_iota(jnp.int32, x.shape, 1) < HEAD_DIM
    return jnp.concatenate([jnp.where(low, x, swapped), jnp.where(low, swapped, x)], axis=1).astype(BF16)


def _paired_attention(q, kds, vds, sink_ref, l, valid, o_ref):
    n = q.shape[0]
    top = lax.broadcasted_iota(jnp.int32, (2 * n, 1), 0) < n
    low = lax.broadcasted_iota(jnp.int32, (2 * n, LANE), 1) < HEAD_DIM
    zero = jnp.zeros((2 * n, LANE), BF16)
    for g, (kd, vd) in enumerate(zip(kds, vds)):
        c0 = 2 * g * LANE
        lhs = jnp.concatenate([q[:, c0:c0 + LANE], q[:, c0 + LANE:c0 + 2 * LANE]], axis=0)
        scores = (_dot_nt(jnp.where(low, lhs, zero), kd), _dot_nt(jnp.where(low, zero, lhs), kd))
        outs = []
        for half, s in enumerate(scores):
            if valid is not None:
                nloc = valid.shape[1]
                s = jnp.concatenate([jnp.where(valid, s[:, :nloc], NEG_INF), s[:, nloc:]], axis=1)
            sk = jnp.where(top, sink_ref[l, 4 * g + half], sink_ref[l, 4 * g + 2 + half]) * LOG2_E
            mx = jnp.maximum(jnp.max(s, axis=-1, keepdims=True), sk)
            p = jnp.exp2(s - mx)
            den = jnp.sum(p, axis=-1, keepdims=True) + jnp.exp2(sk - mx)
            outs.append(_dot(p.astype(BF16), vd) / den)
        o = jnp.where(low, outs[0], outs[1]).astype(BF16)
        o_ref[:, c0:c0 + LANE] = o[:n]
        o_ref[:, c0 + LANE:c0 + 2 * LANE] = o[n:]


def _ctx_attn_kernel(sink_ref, q_ref, k_ref, v_ref, o_ref, *, l):
    q = (q_ref[...] * Q_SCALE).astype(BF16)
    kd = _dup_kv_heads(k_ref[...])
    vd = _dup_kv_heads(v_ref[...])
    groups = [slice(g * LANE, (g + 1) * LANE) for g in range(KV_HEADS)]
    _paired_attention(q, [kd[:, gl] for gl in groups], [vd[:, gl] for gl in groups], sink_ref, l, None, o_ref)


def _ctx_attention(q, k, v, sink, *, l, batch, seq):
    return pl.pallas_call(
        functools.partial(_ctx_attn_kernel, l=l),
        grid=(batch,),
        in_specs=[
            pl.BlockSpec(memory_space=pltpu.SMEM),
            pl.BlockSpec((seq, ATTN_W), lambda b: (b, 0)),
            pl.BlockSpec((seq, KV_W), lambda b: (b, 0)),
            pl.BlockSpec((seq, KV_W), lambda b: (b, 0)),
        ],
        out_specs=pl.BlockSpec((seq, ATTN_W), lambda b: (b, 0)),
        out_shape=jax.ShapeDtypeStruct((batch * seq, ATTN_W), BF16),
        compiler_params=_params(("arbitrary",)),
        name="ctx_attention",
    )(sink, q, k, v)


def _lat_attn_kernel(sink_ref, q_ref, k_ref, v_ref, kc_ref, vc_ref, o_ref, kcd_ref, vcd_ref, *, l, seq):
    i = pl.program_id(1)

    @pl.when(i == 0)
    def _():
        kcd_ref[...] = _dup_kv_heads(kc_ref[...])
        vcd_ref[...] = _dup_kv_heads(vc_ref[...])

    nq = q_ref.shape[0]
    nloc = nq + 2 * WINDOW
    q0 = i * nq
    start = pl.multiple_of(jnp.clip(q0 - WINDOW, 0, seq - nloc), WINDOW)
    jpos = start + lax.broadcasted_iota(jnp.int32, (2 * nq, nloc), 1)
    row = lax.broadcasted_iota(jnp.int32, (2 * nq, nloc), 0)
    valid = jnp.abs(jpos - (q0 + row % nq)) <= WINDOW
    groups = [slice(g * LANE, (g + 1) * LANE) for g in range(KV_HEADS)]
    kds = [jnp.concatenate([k_ref[pl.ds(start, nloc), gl], kcd_ref[:, gl]], axis=0) for gl in groups]
    vds = [jnp.concatenate([v_ref[pl.ds(start, nloc), gl], vcd_ref[:, gl]], axis=0) for gl in groups]
    _paired_attention(q_ref[...], kds, vds, sink_ref, l, valid, o_ref)


def _lat_attention(q, k, v, cache_k, cache_v, sink, *, l, batch, seq):
    nq = LAT_QUERIES
    nb = seq // nq
    past = cache_k.shape[2]
    kv_spec = pl.BlockSpec((seq, 2 * KV_W), lambda b, i: (b, 0))
    cache_spec = pl.BlockSpec((None, None, past, KV_W), lambda b, i: (b, l, 0, 0))
    return pl.pallas_call(
        functools.partial(_lat_attn_kernel, l=l, seq=seq),
        grid=(batch, nb),
        in_specs=[
            pl.BlockSpec(memory_space=pltpu.SMEM),
            pl.BlockSpec((nq, ATTN_W), lambda b, i: (b * nb + i, 0)),
            kv_spec, kv_spec, cache_spec, cache_spec,
        ],
        out_specs=pl.BlockSpec((nq, ATTN_W), lambda b, i: (b * nb + i, 0)),
        out_shape=jax.ShapeDtypeStruct((batch * seq, ATTN_W), BF16),
        scratch_shapes=[pltpu.VMEM((past, 2 * KV_W), BF16), pltpu.VMEM((past, 2 * KV_W), BF16)],
        compiler_params=_params(("arbitrary", "arbitrary")),
        name="lat_attention",
    )(sink, q, k, v, cache_k, cache_v)


def _rope_tables(seq):
    rows = seq // GRID_W
    row = jnp.repeat(jnp.arange(rows), GRID_W)
    col = jnp.arange(rows * GRID_W) % GRID_W
    nf = HEAD_DIM // 4
    inv = ROPE_BASE ** (-jnp.arange(nf, dtype=F32) / nf)
    ang_r = row.astype(F32)[:, None] * inv[None, :]
    ang_c = col.astype(F32)[:, None] * inv[None, :]
    cos_h = jnp.concatenate([jnp.cos(ang_r)] * 2 + [jnp.cos(ang_c)] * 2, axis=1)
    sin_h = jnp.concatenate([-jnp.sin(ang_r), jnp.sin(ang_r), -jnp.sin(ang_c), jnp.sin(ang_c)], axis=1)
    return jnp.tile(cos_h, (1, KV_HEADS)), jnp.tile(sin_h, (1, KV_HEADS))


def _conv_kernel(cv_ref, dw_ref, dwb_ref, lng_ref, lnb_ref, pw_ref, o_ref, pad_ref, *, seq):
    x = cv_ref[...]
    zeros = jnp.zeros((CONV_PAD, CONV_W), F32)
    pad_ref[0:CONV_PAD, :] = zeros
    pad_ref[CONV_PAD:CONV_PAD + seq, :] = x[:, :CONV_W] * jax.nn.sigmoid(x[:, CONV_W:])
    pad_ref[CONV_PAD + seq:2 * CONV_PAD + seq, :] = zeros
    dw = dw_ref[...]
    pw = pw_ref[...].astype(BF16)
    first = CONV_PAD - CONV_K // 2

    def chunk(c, carry):
        base = pl.multiple_of(c * CONV_CHUNK, CONV_CHUNK)
        win = pad_ref[pl.ds(base, CONV_CHUNK + 2 * CONV_PAD), :]
        acc = jnp.zeros((CONV_CHUNK, CONV_W), F32)
        for r in range(SUBLANE):
            offs = [first + k for k in range(CONV_K) if (first + k) % SUBLANE == r]
            if not offs:
                continue
            shifted = win if r == 0 else pltpu.roll(win, win.shape[0] - r, 0)
            for off in offs:
                acc = acc + shifted[off - r:off - r + CONV_CHUNK] * dw[off - first:off - first + 1]
        acc = acc + dwb_ref[...]
        mu = jnp.mean(acc, axis=-1, keepdims=True)
        cen = acc - mu
        var = jnp.mean(cen * cen, axis=-1, keepdims=True)
        y = cen * lax.rsqrt(var + EPS) * lng_ref[...] + lnb_ref[...]
        o_ref[pl.ds(base, CONV_CHUNK), :] = _dot(_silu(y).astype(BF16), pw).astype(BF16)
        return carry

    lax.fori_loop(0, seq // CONV_CHUNK, chunk, 0, unroll=min(4, seq // CONV_CHUNK))


def _conformer_conv(cv, dw, dwb, lng, lnb, pw, *, l, batch, seq):
    kpad = 2 * CONV_PAD
    dw_p = jnp.pad(dw, ((0, 0), (0, kpad - CONV_K), (0, 0)))
    vec = lambda a: a.reshape(DEPTH, 1, CONV_W)
    vspec = pl.BlockSpec((None, 1, CONV_W), lambda b: (l, 0, 0))
    return pl.pallas_call(
        functools.partial(_conv_kernel, seq=seq),
        grid=(batch,),
        in_specs=[
            pl.BlockSpec((seq, 2 * CONV_W), lambda b: (b, 0)),
            pl.BlockSpec((None, kpad, CONV_W), lambda b: (l, 0, 0)),
            vspec, vspec, vspec,
            pl.BlockSpec((None, CONV_W, CONV_W), lambda b: (l, 0, 0)),
        ],
        out_specs=pl.BlockSpec((seq, CONV_W), lambda b: (b, 0)),
        out_shape=jax.ShapeDtypeStruct((batch * seq, CONV_W), BF16),
        scratch_shapes=[pltpu.VMEM((seq + 2 * CONV_PAD, CONV_W), F32)],
        compiler_params=_params(("arbitrary",)),
        name="conformer_conv",
    )(cv, dw_p, vec(dwb), vec(lng), vec(lnb), pw)


def _short_conv3(x, w, b):
    n = x.shape[0]
    row = lax.broadcasted_iota(jnp.int32, x.shape, 0)
    prev = jnp.where(row == 0, 0.0, pltpu.roll(x, 1, 0))
    nxt = jnp.where(row == n - 1, 0.0, pltpu.roll(x, n - 1, 0))
    return prev * w[0:1] + x * w[1:2] + nxt * w[2:3] + b


def _filter_kernel(z_ref, w1_ref, b1_ref, f1_ref, w2_ref, b2_ref, f2_ref, w3_ref, ld_ref, o_ref):
    z = z_ref[...]
    h = jnp.sin(f1_ref[...] * (_dot_hi(z, w1_ref[...]) + b1_ref[...]))
    h = jnp.sin(f2_ref[...] * (_dot_hi(h, w2_ref[...]) + b2_ref[...]))
    h = _dot_hi(h, w3_ref[...])
    tn = z[:, 0:1]
    h = h * jnp.exp(-tn * jnp.exp(ld_ref[...]))
    ss = jnp.sum(h * h, axis=0, keepdims=True)
    c = HYENA_W
    scale = []
    for o in range(2):
        tot = ss[:, 2 * o * c:(2 * o + 1) * c] + ss[:, (2 * o + 1) * c:(2 * o + 2) * c]
        r = lax.rsqrt(tot + EPS)
        scale += [r, r]
    h = h * jnp.concatenate(scale, axis=1)
    row = lax.broadcasted_iota(jnp.int32, h.shape, 0)
    col = lax.broadcasted_iota(jnp.int32, h.shape, 1)
    o_ref[...] = jnp.where((row == 0) & ((col // c) % 2 == 1), 0.0, h)


def _hyena_features(seq):
    t = jnp.arange(seq, dtype=F32)
    tn = t / (seq - 1)
    bands = jnp.linspace(1e-4, HYENA_BANDS - 1, HYENA_BANDS, dtype=F32)
    ang = 2.0 * math.pi * t[:, None] * bands[None, :] / seq
    z = jnp.concatenate([tn[:, None], jnp.cos(ang), -jnp.sin(ang)], axis=-1)
    return jnp.pad(z, ((0, 0), (0, LANE - HYENA_EMB)))


def _hyena_filters(z, w1, b1, f1, w2, b2, f2, w3, log_decay):
    seq = z.shape[0]
    n = w3.shape[1]
    return pl.pallas_call(
        _filter_kernel,
        out_shape=jax.ShapeDtypeStruct((seq, n), F32),
        compiler_params=_params(None, VMEM_LIMIT),
        name="hyena_filters",
    )(z, w1, b1, f1, w2, b2, f2, w3, log_decay)


def _spectrum_kernel(m_ref, h_ref, lr_ref, li_ref, hr_ref, hi_ref, mid_ref, *, n_fft):
    j = pl.program_id(0)
    fc = m_ref.shape[1]
    c = HYENA_W
    h16 = h_ref[...].astype(BF16)
    h = h16.astype(F32)
    hh = jnp.concatenate([h16, _alternate_rows(h).astype(BF16)], axis=1)
    gc = _dot(m_ref[0], hh)
    gs = _dot(m_ref[1], hh)
    row = j * fc + lax.broadcasted_iota(jnp.int32, (fc, c), 0)
    wgt = jnp.where(row == 0, 1.0 / n_fft, 2.0 / n_fft)
    t4 = lax.broadcasted_iota(jnp.int32, h.shape, 0) % 4
    mid_c = jnp.sum(jnp.where(t4 == 0, h, jnp.where(t4 == 2, -h, 0.0)), axis=0, keepdims=True)
    mid_s = jnp.sum(jnp.where(t4 == 3, h, jnp.where(t4 == 1, -h, 0.0)), axis=0, keepdims=True)
    for o in range(2):
        f = slice(2 * o * c, (2 * o + 1) * c)
        b = slice((2 * o + 1) * c, (2 * o + 2) * c)
        fa = slice(4 * c + 2 * o * c, 4 * c + (2 * o + 1) * c)
        ba = slice(4 * c + (2 * o + 1) * c, 4 * c + (2 * o + 2) * c)
        lr_ref[o] = (gc[:, f] + gc[:, b]) * wgt
        li_ref[o] = (gs[:, f] - gs[:, b]) * wgt
        hr_ref[o] = (gc[:, fa] + gc[:, ba]) * wgt
        hi_ref[o] = (gs[:, ba] - gs[:, fa]) * wgt
        mid_ref[o] = jnp.concatenate([mid_c[:, f] + mid_c[:, b], mid_s[:, f] - mid_s[:, b]],
                                     axis=0) * (2.0 / n_fft)


def _filter_spectra(dft, hcat, *, fc):
    half, seq = dft.shape[1:]
    c = HYENA_W
    fc = min(fc, half)
    out_spec = pl.BlockSpec((2, fc, c), lambda j: (0, j, 0))
    shape = jax.ShapeDtypeStruct((2, half, c), F32)
    return pl.pallas_call(
        functools.partial(_spectrum_kernel, n_fft=2 * seq),
        grid=(half // fc,),
        in_specs=[
            pl.BlockSpec((2, fc, seq), lambda j: (0, j, 0)),
            _resident(hcat.shape, lambda j: (0, 0)),
        ],
        out_specs=[out_spec] * 4 + [pl.BlockSpec((2, 2, c), lambda j: (0, 0, 0))],
        out_shape=[shape] * 4 + [jax.ShapeDtypeStruct((2, 2, c), F32)],
        compiler_params=_params(("arbitrary",), VMEM_LIMIT),
        name="hyena_spectra",
    )(dft, hcat)


def _alternate_rows(x):
    row = lax.broadcasted_iota(jnp.int32, x.shape, 0)
    return jnp.where(row % 2 == 0, x, -x)


def _longconv_split(ue, uo, m_ref, tw_ref, lr_ref, li_ref, hr_ref, hi_ref, mid_ref):
    shape = ue.shape
    tile = lambda a: jnp.concatenate([a] * (shape[1] // a.shape[1]), axis=1)
    wc = tile(tw_ref[0])
    ws = tile(tw_ref[1])
    ue16 = ue.astype(BF16)
    uo16 = uo.astype(BF16)
    er = _dot(m_ref[0], ue16)
    ei = _dot(m_ref[1], ue16)
    orr = _dot(m_ref[0], uo16)
    oi = _dot(m_ref[1], uo16)
    sign = _alternate_rows(jnp.ones(shape, F32))
    en = jnp.sum(ue16.astype(F32) * sign, axis=0, keepdims=True)
    on = jnp.sum(uo16.astype(F32) * sign, axis=0, keepdims=True)
    tr = wc * orr + ws * oi
    ti = wc * oi - ws * orr
    pr, pi, mr, mi = er + tr, ei + ti, er - tr, ei - ti
    lr, li, hr, hi = lr_ref[...], li_ref[...], hr_ref[...], hi_ref[...]
    ylr = pr * lr - pi * li
    yli = pr * li + pi * lr
    yhr = mr * hr + mi * hi
    yhi = mr * hi - mi * hr
    dr = ylr - yhr
    di = yli + yhi
    ger = (ylr + yhr).astype(BF16)
    gei = (yli - yhi).astype(BF16)
    gor = (dr * wc - di * ws).astype(BF16)
    goi = (dr * ws + di * wc).astype(BF16)
    ym_r = en * mid_ref[0:1] + on * mid_ref[1:2]
    ym_i = en * mid_ref[1:2] - on * mid_ref[0:1]
    ye = _dot(m_ref[0], ger) + _dot(m_ref[1], gei) + sign * ym_r
    yo = _dot(m_ref[0], gor) + _dot(m_ref[1], goi) - sign * ym_i
    return ye, yo


def _deinterleave(x, buf_ref):
    half = x.shape[0] // 2
    groups = buf_ref.shape[0]
    for g in range(groups):
        buf_ref[g] = x[:, g * LANE:(g + 1) * LANE]
    pick = lambda first: jnp.concatenate(
        [buf_ref[g, pl.ds(first, half, stride=2), :] for g in range(groups)], axis=1)
    return pick(0), pick(1)


def _interleave(even, odd, buf_ref):
    half = even.shape[0]
    groups = buf_ref.shape[0]
    for g in range(groups):
        buf_ref[g, pl.ds(0, half, stride=2), :] = even[:, g * LANE:(g + 1) * LANE]
        buf_ref[g, pl.ds(1, half, stride=2), :] = odd[:, g * LANE:(g + 1) * LANE]
    return jnp.concatenate([buf_ref[g] for g in range(groups)], axis=1)


def _hyena_conv0_kernel(hv_ref, hx_ref, sw_ref, sb_ref, bias_ref, m_ref, tw_ref, lr_ref, li_ref, hr_ref, hi_ref,
                        mid_ref, y_ref, vbuf_ref, xbuf_ref, *, seq):
    c = HYENA_W
    half = seq // 2
    for bi in range(hv_ref.shape[0] // seq):
        rows = slice(bi * seq, (bi + 1) * seq)
        ve, vo = _deinterleave(_short_conv3(hv_ref[rows, :], sw_ref[:, 0:c], sb_ref[:, 0:c]), vbuf_ref)
        xe, xo = _deinterleave(_short_conv3(hx_ref[rows, :], sw_ref[:, c:2 * c], sb_ref[:, c:2 * c]), xbuf_ref)
        ce, co = _longconv_split(ve, vo, m_ref, tw_ref, lr_ref, li_ref, hr_ref, hi_ref, mid_ref)
        y_ref[bi * seq:bi * seq + half, :] = xe * (ce + bias_ref[0:1] * ve)
        y_ref[bi * seq + half:(bi + 1) * seq, :] = xo * (co + bias_ref[0:1] * vo)


def _hyena_conv1_kernel(y_ref, hx_ref, sw_ref, sb_ref, bias_ref, m_ref, tw_ref, lr_ref, li_ref, hr_ref, hi_ref,
                        mid_ref, o_ref, xbuf_ref, obuf_ref, *, seq):
    c = HYENA_W
    half = seq // 2
    for bi in range(y_ref.shape[0] // seq):
        rows = slice(bi * seq, (bi + 1) * seq)
        ye = y_ref[bi * seq:bi * seq + half, :]
        yo = y_ref[bi * seq + half:(bi + 1) * seq, :]
        xe, xo = _deinterleave(_short_conv3(hx_ref[rows, :], sw_ref[:, 2 * c:3 * c], sb_ref[:, 2 * c:3 * c]),
                               xbuf_ref)
        ce, co = _longconv_split(ye, yo, m_ref, tw_ref, lr_ref, li_ref, hr_ref, hi_ref, mid_ref)
        out = _interleave(xe * (ce + bias_ref[1:2] * ye), xo * (co + bias_ref[1:2] * yo), obuf_ref)
        o_ref[rows, :] = out.astype(BF16)


def _twiddles(seq):
    f = np.arange(seq // 2, dtype=np.float64)[:, None] * np.ones((1, LANE))
    ang = 2.0 * np.pi * f / (2 * seq)
    return jnp.asarray(np.stack([np.cos(ang), np.sin(ang)]), F32)


def _hyena_mixer(hy, sw, sb, bias, dft, twiddle, spectra, *, l, batch, seq, bg):
    c = HYENA_W
    t = batch * seq
    rows = bg * seq
    sw_p = jnp.pad(sw, ((0, 0), (0, SUBLANE - sw.shape[1]), (0, 0)))
    sb3 = sb.reshape(DEPTH, 1, 3 * c)
    chan = lambda ch: pl.BlockSpec((rows, c), lambda g: (g, ch))
    sw_spec = pl.BlockSpec((None, SUBLANE, 3 * c), lambda g: (l, 0, 0))
    sb_spec = pl.BlockSpec((None, 1, 3 * c), lambda g: (l, 0, 0))
    bias_spec = pl.BlockSpec((None, 2, c), lambda g: (l, 0, 0))
    half = seq // 2
    dft_spec = _resident(dft.shape, lambda g: (0, 0, 0))
    tw_spec = _resident(twiddle.shape, lambda g: (0, 0, 0))
    coef = lambda order: ([_resident((None, half, c), lambda g: (order, 0, 0))] * 4
                          + [pl.BlockSpec((None, 2, c), lambda g: (order, 0, 0))])
    params = _params(("arbitrary",), VMEM_LIMIT)
    seq_buf = pltpu.VMEM((c // LANE, seq, LANE), F32)

    y = pl.pallas_call(
        functools.partial(_hyena_conv0_kernel, seq=seq),
        grid=(batch // bg,),
        in_specs=[chan(0), chan(1), sw_spec, sb_spec, bias_spec, dft_spec, tw_spec] + coef(0),
        out_specs=chan(0),
        out_shape=jax.ShapeDtypeStruct((t, c), F32),
        scratch_shapes=[seq_buf, seq_buf],
        compiler_params=params,
        name="hyena_conv0",
    )(hy, hy, sw_p, sb3, bias, dft, twiddle, *spectra)

    return pl.pallas_call(
        functools.partial(_hyena_conv1_kernel, seq=seq),
        grid=(batch // bg,),
        in_specs=[chan(0), chan(2), sw_spec, sb_spec, bias_spec, dft_spec, tw_spec] + coef(1),
        out_specs=chan(0),
        out_shape=jax.ShapeDtypeStruct((t, c), BF16),
        scratch_shapes=[seq_buf, seq_buf],
        compiler_params=params,
        name="hyena_conv1",
    )(y, hy, sw_p, sb3, bias, dft, twiddle, *spectra)


def _dft_kernel(c1_ref, s1_ref, c0_ref, s0_ref, o_ref):
    c0 = c0_ref[...]
    s0 = s0_ref[...]
    for i in range(c1_ref.shape[0]):
        c1 = c1_ref[i:i + 1, :]
        s1 = s1_ref[i:i + 1, :]
        rows = slice(i * DFT_MINOR, (i + 1) * DFT_MINOR)
        o_ref[0, rows, :] = (c1 * c0 - s1 * s0).astype(BF16)
        o_ref[1, rows, :] = (-(s1 * c0 + c1 * s0)).astype(BF16)


def _dft_matrix(seq, rows=None):
    rows = seq if rows is None else rows
    n = 2 * seq
    t = np.arange(seq, dtype=np.int64)
    f1n = rows // DFT_MINOR
    ang1 = 2.0 * np.pi * ((np.arange(f1n)[:, None] * DFT_MINOR * t[None, :]) % n).astype(np.float64) / n
    ang0 = 2.0 * np.pi * ((np.arange(DFT_MINOR)[:, None] * t[None, :]) % n).astype(np.float64) / n
    tables = [jnp.asarray(f(a), F32) for a in (ang1, ang0) for f in (np.cos, np.sin)]
    nb = min(f1n, SUBLANE)
    coarse = pl.BlockSpec((nb, seq), lambda j: (j, 0))
    fine = pl.BlockSpec((DFT_MINOR, seq), lambda j: (0, 0))
    return pl.pallas_call(
        _dft_kernel,
        grid=(f1n // nb,),
        in_specs=[coarse, coarse, fine, fine],
        out_specs=pl.BlockSpec((2, nb * DFT_MINOR, seq), lambda j: (0, j, 0)),
        out_shape=jax.ShapeDtypeStruct((2, rows, seq), BF16),
        compiler_params=_params(("arbitrary",)),
        name="dft_matrix",
    )(*tables)


def _mixffn_kernel(x_ref, att_ref, cvo_ref, hyo_ref, mod_ref, wo_ref, g_ref,
                   wg_ref, wu_ref, wd_ref, gf_ref, o_ref, a_ref, *, final):
    m = mod_ref[...]
    o1 = ATTN_W
    o2 = ATTN_W + CONV_W
    rows = _row_halves(x_ref.shape[0])
    xs, hs = [], []
    for r in rows:
        mix = (_dot(att_ref[r, :], wo_ref[0:o1, :]) + _dot(cvo_ref[r, :], wo_ref[o1:o2, :])
               + _dot(hyo_ref[r, :], wo_ref[o2:, :]))
        x = x_ref[r, :] + m[5:6] * mix
        xs.append(x)
        hs.append(_mod_norm(x, g_ref[...], m[7:8], m[6:7]).astype(BF16))
    ys = _swiglu_chains(hs, rows, wg_ref, wu_ref, wd_ref, a_ref)
    for r, x, y in zip(rows, xs, ys):
        x = x + (0.5 * m[8:9]) * y
        if final:
            x = x * lax.rsqrt(jnp.mean(x * x, axis=-1, keepdims=True) + EPS) * gf_ref[...]
        o_ref[r, :] = x


def _mixffn(x, att, cvo, hyo, mod4, w_out, gain, wg, wu, wd, g_final, *, l, row0, rows_per_cond, final):
    t, d = x.shape
    tm = TOKEN_TILE
    tok = lambda w: pl.BlockSpec((tm, w), lambda i: (i, 0))
    wspec_in = _resident((None, d, D_FF), lambda i: (l, 0, 0))
    return pl.pallas_call(
        functools.partial(_mixffn_kernel, final=final),
        grid=(t // tm,),
        in_specs=[
            tok(d), tok(ATTN_W), tok(CONV_W), tok(HYENA_W),
            _mod_spec(l, row0, rows_per_cond, tm),
            _resident((None, d, d), lambda i: (l, 0, 0)),
            pl.BlockSpec((None, 1, d), lambda i: (l, 0, 0)),
            wspec_in, wspec_in,
            _resident((None, D_FF, d), lambda i: (l, 0, 0)),
            pl.BlockSpec((1, d), lambda i: (0, 0)),
        ],
        out_specs=tok(d),
        out_shape=jax.ShapeDtypeStruct((t, d), F32),
        scratch_shapes=[pltpu.VMEM((tm, D_FF), BF16)],
        compiler_params=_params(("arbitrary",), VMEM_LIMIT),
        name="mixffn",
    )(x, att, cvo, hyo, mod4, w_out, gain.reshape(DEPTH, 1, d), wg, wu, wd, g_final.reshape(1, d))


def _pad_cols(a, width):
    return jnp.pad(a, [(0, 0)] * (a.ndim - 1) + [(0, width - a.shape[-1])])


def _trunk(x, cache, mod4, P, *, batch, seq, row0, rows_per_cond):
    fc = min(seq, FREQ_CHUNK)
    bg = max(1, min(batch, LONGCONV_ROWS // seq, LONGCONV_SEQS))
    dft = _dft_matrix(seq, seq // 2)
    dft_half = _dft_matrix(seq // 2)
    twiddle = _twiddles(seq)
    z = _hyena_features(seq)
    rope_tables = None if cache is None else _rope_tables(seq)
    w_in = P['w_in']
    cond = dict(row0=row0, rows_per_cond=rows_per_cond)
    ks, vs = [], []
    for l in range(DEPTH):
        x, q, k, v, cv, hy = _ffn_inproj(x, mod4, P['g_ffn1'], P['g_mix'], P['w1_gate'], P['w1_up'],
                                         P['w1_down'], w_in, rope_tables, l=l, seq=seq, **cond)
        if cache is None:
            att = _ctx_attention(q, k, v, P['attn_sink'], l=l, batch=batch, seq=seq)
            ks.append(k)
            vs.append(v)
        else:
            att = _lat_attention(q, k, v, cache[0], cache[1], P['attn_sink'], l=l, batch=batch, seq=seq)
        cvo = _conformer_conv(cv, P['conv_dw'], P['conv_dw_b'], P['conv_ln_g'], P['conv_ln_b'],
                              P['conv_pw'], l=l, batch=batch, seq=seq)
        hcat = _hyena_filters(z, P['hy_w1'][l], P['hy_b1'][l], P['hy_f1'][l], P['hy_w2'][l],
                              P['hy_b2'][l], P['hy_f2'][l], P['hy_w3'][l], P['hy_log_decay'][l])
        spectra = _filter_spectra(dft, hcat, fc=fc)
        hyo = _hyena_mixer(hy, P['hy_short_w'], P['hy_short_b'], P['hy_bias'], dft_half, twiddle, spectra,
                           l=l, batch=batch, seq=seq, bg=bg)
        x = _mixffn(x, att, cvo, hyo, mod4, P['w_out'], P['g_ffn2'], P['w2_gate'], P['w2_up'],
                    P['w2_down'], P['g_final'], l=l, final=(l == DEPTH - 1), **cond)
    return x, ks, vs


def kernel(x_prompt, x_sample, c, cache_k, cache_v, c_ctx, w_mod, b_mod, g_ffn1, g_mix, g_ffn2,
           g_final, w1_gate, w1_up, w1_down, w2_gate, w2_up, w2_down, w_in, w_out, attn_sink,
           conv_dw, conv_dw_b, conv_ln_g, conv_ln_b, conv_pw, hy_short_w, hy_short_b,
           hy_w1, hy_b1, hy_f1, hy_w2, hy_b2, hy_f2, hy_w3, hy_log_decay, hy_bias):
    batch, seq, d = x_prompt.shape
    dec_batch, dec_seq, _ = x_sample.shape
    assert 1 + dec_batch <= COND_ROWS

    hid = LANE
    P = dict(
        g_ffn1=g_ffn1, g_mix=g_mix, g_ffn2=g_ffn2, g_final=g_final, attn_sink=attn_sink,
        w1_gate=w1_gate.astype(BF16), w1_up=w1_up.astype(BF16), w1_down=w1_down.astype(BF16),
        w2_gate=w2_gate.astype(BF16), w2_up=w2_up.astype(BF16), w2_down=w2_down.astype(BF16),
        w_in=w_in.astype(BF16), w_out=w_out.astype(BF16),
        conv_dw=conv_dw, conv_dw_b=conv_dw_b, conv_ln_g=conv_ln_g, conv_ln_b=conv_ln_b, conv_pw=conv_pw,
        hy_short_w=hy_short_w, hy_short_b=hy_short_b,
        hy_w1=jnp.pad(hy_w1, ((0, 0), (0, hid - HYENA_EMB), (0, hid - HYENA_HID))),
        hy_b1=_pad_cols(hy_b1, hid)[:, None, :], hy_f1=_pad_cols(hy_f1, hid)[:, None, :],
        hy_w2=jnp.pad(hy_w2, ((0, 0), (0, hid - HYENA_HID), (0, hid - HYENA_HID))),
        hy_b2=_pad_cols(hy_b2, hid)[:, None, :], hy_f2=_pad_cols(hy_f2, hid)[:, None, :],
        hy_w3=jnp.pad(hy_w3, ((0, 0), (0, hid - HYENA_HID), (0, 0))),
        hy_log_decay=hy_log_decay[:, None, :], hy_bias=hy_bias,
    )

    conds = jnp.concatenate([c_ctx[None, :], c, jnp.zeros((COND_ROWS - 1 - dec_batch, d), F32)], axis=0)
    mod = _modulation(conds, w_mod, b_mod)
    mod4 = mod.reshape(DEPTH, COND_ROWS, N_MOD, d)

    xp, ks, vs = _trunk(x_prompt.reshape(batch * seq, d), None, mod4, P,
                        batch=batch, seq=seq, row0=0, rows_per_cond=batch * seq)
    cache = (cache_k.reshape(*cache_k.shape[:3], KV_W), cache_v.reshape(*cache_v.shape[:3], KV_W))
    xs, _, _ = _trunk(x_sample.reshape(dec_batch * dec_seq, d), cache, mod4, P,
                      batch=dec_batch, seq=dec_seq, row0=1, rows_per_cond=dec_seq)

    kv_shape = (batch, DEPTH, seq, KV_HEADS, HEAD_DIM)
    new_k = jnp.stack([k.reshape(batch, seq, KV_W) for k in ks], axis=1).reshape(kv_shape)
    new_v = jnp.stack([v.reshape(batch, seq, KV_W) for v in vs], axis=1).reshape(kv_shape)
    return (xp.reshape(batch, seq, d), xs.reshape(dec_batch, dec_seq, d), new_k, new_v)
```

```python
import functools
import math

import numpy as np
import jax
import jax.numpy as jnp
from jax import lax
from jax.experimental import pallas as pl
from jax.experimental.pallas import tpu as pltpu

F32 = jnp.float32
BF16 = jnp.bfloat16

D_MODEL = 1024
DEPTH = 2
GRID_W = 64
N_HEADS = 8
KV_HEADS = 2
HEAD_DIM = 64
Q_PER_KV = N_HEADS // KV_HEADS
ATTN_W = N_HEADS * HEAD_DIM
KV_W = KV_HEADS * HEAD_DIM
CONV_W = D_MODEL // 4
HYENA_W = D_MODEL // 4
WINDOW = 128
BLOCK = 128
CONV_K = 31
HYENA_EMB = 33
HYENA_BANDS = (HYENA_EMB - 1) // 2
HYENA_HID = 64
D_FF = 2816
N_MOD = 9
ROPE_BASE = 10000.0
EPS = 1e-6
NEG_INF = -1e30
LOG2_E = math.log2(math.e)
Q_SCALE = HEAD_DIM ** -0.5 * LOG2_E

LANE = 128
SUBLANE = 8
MXU_W = 256
COND_ROWS = 16
TOKEN_TILE = 512
CONV_CHUNK = 128
CONV_PAD = 16
DFT_MINOR = 64
FREQ_CHUNK = 512
LONGCONV_ROWS = 4096
LONGCONV_SEQS = 4
CTX_ATTN_SEQS = 2
LAT_QUERIES = 256
VMEM_LIMIT = 56 * 1024 * 1024


def _params(sem, vmem=None):
    return pltpu.CompilerParams(dimension_semantics=sem, vmem_limit_bytes=vmem)


def _silu(x):
    return x * jax.nn.sigmoid(x)


def _dot(a, b):
    return jnp.dot(a, b, preferred_element_type=F32)


def _dot_nt(a, b):
    return lax.dot_general(a, b, (((1,), (1,)), ((), ())), preferred_element_type=F32)


def _dot_hi(a, b):
    return jnp.dot(a, b, preferred_element_type=F32, precision=lax.Precision.HIGHEST)


def _mod_norm(x, gain, scale, shift):
    y = x * lax.rsqrt(jnp.mean(x * x, axis=-1, keepdims=True) + EPS)
    return (y * gain) * (1.0 + scale) + shift


def _mod_kernel(c_ref, w_ref, b_ref, o_ref):
    s = _silu(c_ref[...]).astype(BF16)
    o_ref[...] = _dot(s, w_ref[...].astype(BF16)) + b_ref[...]


def _modulation(conds, w_mod, b_mod):
    d = D_MODEL
    return pl.pallas_call(
        _mod_kernel,
        grid=(DEPTH, N_MOD),
        in_specs=[
            pl.BlockSpec((COND_ROWS, d), lambda l, j: (0, 0)),
            pl.BlockSpec((None, d, d), lambda l, j: (l, 0, j)),
            pl.BlockSpec((None, 1, d), lambda l, j: (l, 0, j)),
        ],
        out_specs=pl.BlockSpec((None, COND_ROWS, d), lambda l, j: (l, 0, j)),
        out_shape=jax.ShapeDtypeStruct((DEPTH, COND_ROWS, N_MOD * d), F32),
        compiler_params=_params(("arbitrary", "arbitrary")),
        name="modulation",
    )(conds, w_mod, b_mod.reshape(DEPTH, 1, N_MOD * d))


def _row_halves(n):
    return slice(0, n // 2), slice(n // 2, n)


def _swiglu_chains(hs, rows, wg_ref, wu_ref, wd_ref, a_ref):
    for c in range(D_FF // MXU_W):
        sl = slice(c * MXU_W, (c + 1) * MXU_W)
        for r, h in zip(rows, hs):
            a_ref[r, sl] = (_silu(_dot(h, wg_ref[:, sl])) * _dot(h, wu_ref[:, sl])).astype(BF16)
    return [_dot(a_ref[r, :], wd_ref[...]) for r in rows]


def _ffn1_then_norm(x_ref, mod_ref, g1_ref, g2_ref, wg_ref, wu_ref, wd_ref, o_ref, a_ref):
    m = mod_ref[...]
    rows = _row_halves(x_ref.shape[0])
    xs = [x_ref[r, :] for r in rows]
    hs = [_mod_norm(x, g1_ref[...], m[1:2], m[0:1]).astype(BF16) for x in xs]
    ys = _swiglu_chains(hs, rows, wg_ref, wu_ref, wd_ref, a_ref)
    out = []
    for r, x, y in zip(rows, xs, ys):
        x = x + (0.5 * m[2:3]) * y
        o_ref[r, :] = x
        out.append(_mod_norm(x, g2_ref[...], m[4:5], m[3:4]).astype(BF16))
    return rows, out


def _resident(shape, index_map):
    return pl.BlockSpec(shape, index_map, pipeline_mode=pl.Buffered(1))


def _mod_spec(l, row0, rows_per_cond, tm):
    return pl.BlockSpec((None, None, N_MOD, D_MODEL),
                        lambda i: (l, row0 + (i * tm) // rows_per_cond, 0, 0))


_IN_SPLITS = (ATTN_W, KV_W, KV_W, 2 * CONV_W, 3 * HYENA_W)
_IN_SPLITS_LAT = (ATTN_W, 2 * KV_W, 2 * KV_W, 2 * CONV_W, 3 * HYENA_W)


def _rope(x, cos, sin_signed):
    w = x.shape[-1]
    lane = lax.broadcasted_iota(jnp.int32, x.shape, 1)
    quarter = HEAD_DIM // 4
    partner = jnp.where(lane % (2 * quarter) < quarter,
                        pltpu.roll(x, w - quarter, 1), pltpu.roll(x, quarter, 1))
    return x * cos + partner * sin_signed


def _ffn_inproj_kernel(x_ref, mod_ref, g1_ref, g2_ref, wg_ref, wu_ref, wd_ref, w_ref,
                       o_ref, q_ref, k_ref, v_ref, cv_ref, hy_ref, a_ref):
    rows, hs = _ffn1_then_norm(x_ref, mod_ref, g1_ref, g2_ref, wg_ref, wu_ref, wd_ref, o_ref, a_ref)
    off = 0
    for ref, width in zip((q_ref, k_ref, v_ref, cv_ref, hy_ref), _IN_SPLITS):
        for r, h in zip(rows, hs):
            ref[r, :] = _dot(h, w_ref[:, off:off + width])
        off += width


def _ffn_inproj_lat_kernel(x_ref, mod_ref, g1_ref, g2_ref, wg_ref, wu_ref, wd_ref, w_ref, cos_ref, sin_ref,
                           o_ref, q_ref, k_ref, v_ref, cv_ref, hy_ref, a_ref):
    wq, wk, wv, wc, wh = _IN_SPLITS
    tile = lambda a, width: jnp.concatenate([a] * (width // a.shape[1]), axis=1)
    rows, hs = _ffn1_then_norm(x_ref, mod_ref, g1_ref, g2_ref, wg_ref, wu_ref, wd_ref, o_ref, a_ref)
    for r, h in zip(rows, hs):
        cos = cos_ref[r, :]
        sin = sin_ref[r, :]
        q = _rope(_dot(h, w_ref[:, 0:wq]), tile(cos, wq), tile(sin, wq))
        q_ref[r, :] = (q * Q_SCALE).astype(BF16)
        k_ref[r, :] = _dup_kv_heads(_rope(_dot(h, w_ref[:, wq:wq + wk]), cos, sin))
    off = wq + wk
    for r, h in zip(rows, hs):
        v_ref[r, :] = _dup_kv_heads(_dot(h, w_ref[:, off:off + wv]))
    off += wv
    for r, h in zip(rows, hs):
        cv_ref[r, :] = _dot(h, w_ref[:, off:off + wc])
    off += wc
    for r, h in zip(rows, hs):
        hy_ref[r, :] = _dot(h, w_ref[:, off:off + wh])


def _ffn_inproj(x, mod4, g_ffn, g_mix, wg, wu, wd, w_in, rope_tables, *, l, row0, rows_per_cond, seq):
    t, d = x.shape
    tm = TOKEN_TILE
    lat = rope_tables is not None
    splits = _IN_SPLITS_LAT if lat else _IN_SPLITS
    dtypes = (BF16, BF16, BF16, F32, F32) if lat else (F32,) * 5
    gain_spec = pl.BlockSpec((None, 1, d), lambda i: (l, 0, 0))
    wspec_in = _resident((None, d, D_FF), lambda i: (l, 0, 0))
    in_specs = [
        pl.BlockSpec((tm, d), lambda i: (i, 0)),
        _mod_spec(l, row0, rows_per_cond, tm),
        gain_spec, gain_spec,
        wspec_in, wspec_in,
        _resident((None, D_FF, d), lambda i: (l, 0, 0)),
        _resident((None, d, sum(_IN_SPLITS)), lambda i: (l, 0, 0)),
    ]
    args = [x, mod4, g_ffn.reshape(DEPTH, 1, d), g_mix.reshape(DEPTH, 1, d), wg, wu, wd, w_in]
    if lat:
        table_spec = pl.BlockSpec((tm, LANE), lambda i: (i % (seq // tm), 0))
        in_specs += [table_spec, table_spec]
        args += list(rope_tables)
    tok = lambda w: pl.BlockSpec((tm, w), lambda i: (i, 0))
    return pl.pallas_call(
        _ffn_inproj_lat_kernel if lat else _ffn_inproj_kernel,
        grid=(t // tm,),
        in_specs=in_specs,
        out_specs=[tok(d)] + [tok(w) for w in splits],
        out_shape=[jax.ShapeDtypeStruct((t, d), F32)]
        + [jax.ShapeDtypeStruct((t, w), dt) for w, dt in zip(splits, dtypes)],
        scratch_shapes=[pltpu.VMEM((tm, D_FF), BF16)],
        compiler_params=_params(("arbitrary",), VMEM_LIMIT),
        name="ffn_inproj",
    )(*args)


def _dup_kv_heads(x):
    swapped = pltpu.roll(x, HEAD_DIM, 1)
    low = lax.broadcasted_iota(jnp.int32, x.shape, 1) < HEAD_DIM
    return jnp.concatenate([jnp.where(low, x, swapped), jnp.where(low, swapped, x)], axis=1).astype(BF16)


def _paired_attention(q, kds, vds, sink_ref, l, valid, o_ref):
    n = q.shape[0]
    top = lax.broadcasted_iota(jnp.int32, (2 * n, 1), 0) < n
    low = lax.broadcasted_iota(jnp.int32, (2 * n, LANE), 1) < HEAD_DIM
    zero = jnp.zeros((2 * n, LANE), BF16)
    for g, (kd, vd) in enumerate(zip(kds, vds)):
        c0 = 2 * g * LANE
        lhs = jnp.concatenate([q[:, c0:c0 + LANE], q[:, c0 + LANE:c0 + 2 * LANE]], axis=0)
        scores = (_dot_nt(jnp.where(low, lhs, zero), kd), _dot_nt(jnp.where(low, zero, lhs), kd))
        outs = []
        for half, s in enumerate(scores):
            if valid is not None:
                nloc = valid.shape[1]
                s = jnp.concatenate([jnp.where(valid, s[:, :nloc], NEG_INF), s[:, nloc:]], axis=1)
            sk = jnp.where(top, sink_ref[l, 4 * g + half], sink_ref[l, 4 * g + 2 + half]) * LOG2_E
            mx = jnp.maximum(jnp.max(s, axis=-1, keepdims=True), sk)
            p = jnp.exp2(s - mx)
            den = jnp.sum(p, axis=-1, keepdims=True) + jnp.exp2(sk - mx)
            outs.append(_dot(p.astype(BF16), vd) / den)
        o = jnp.where(low, outs[0], outs[1]).astype(BF16)
        o_ref[:, c0:c0 + LANE] = o[:n]
        o_ref[:, c0 + LANE:c0 + 2 * LANE] = o[n:]


def _ctx_attn_kernel(sink_ref, q_ref, k_ref, v_ref, o_ref, *, l, seq):
    groups = [slice(g * LANE, (g + 1) * LANE) for g in range(KV_HEADS)]
    for bi in range(q_ref.shape[0] // seq):
        rows = slice(bi * seq, (bi + 1) * seq)
        q = (q_ref[rows, :] * Q_SCALE).astype(BF16)
        kd = _dup_kv_heads(k_ref[rows, :])
        vd = _dup_kv_heads(v_ref[rows, :])
        _paired_attention(q, [kd[:, gl] for gl in groups], [vd[:, gl] for gl in groups], sink_ref, l, None,
                          o_ref.at[rows, :])


def _ctx_attention(q, k, v, sink, *, l, batch, seq):
    rows = min(batch, CTX_ATTN_SEQS) * seq
    return pl.pallas_call(
        functools.partial(_ctx_attn_kernel, l=l, seq=seq),
        grid=(batch * seq // rows,),
        in_specs=[
            pl.BlockSpec(memory_space=pltpu.SMEM),
            pl.BlockSpec((rows, ATTN_W), lambda b: (b, 0)),
            pl.BlockSpec((rows, KV_W), lambda b: (b, 0)),
            pl.BlockSpec((rows, KV_W), lambda b: (b, 0)),
        ],
        out_specs=pl.BlockSpec((rows, ATTN_W), lambda b: (b, 0)),
        out_shape=jax.ShapeDtypeStruct((batch * seq, ATTN_W), BF16),
        compiler_params=_params(("arbitrary",)),
        name="ctx_attention",
    )(sink, q, k, v)


def _lat_attn_kernel(sink_ref, q_ref, k_ref, v_ref, kc_ref, vc_ref, o_ref, kcd_ref, vcd_ref, *, l, seq):
    i = pl.program_id(1)

    @pl.when(i == 0)
    def _():
        kcd_ref[...] = _dup_kv_heads(kc_ref[...])
        vcd_ref[...] = _dup_kv_heads(vc_ref[...])

    nq = q_ref.shape[0]
    nloc = nq + 2 * WINDOW
    q0 = i * nq
    start = pl.multiple_of(jnp.clip(q0 - WINDOW, 0, seq - nloc), WINDOW)
    jpos = start + lax.broadcasted_iota(jnp.int32, (2 * nq, nloc), 1)
    row = lax.broadcasted_iota(jnp.int32, (2 * nq, nloc), 0)
    valid = jnp.abs(jpos - (q0 + row % nq)) <= WINDOW
    groups = [slice(g * LANE, (g + 1) * LANE) for g in range(KV_HEADS)]
    kds = [jnp.concatenate([k_ref[pl.ds(start, nloc), gl], kcd_ref[:, gl]], axis=0) for gl in groups]
    vds = [jnp.concatenate([v_ref[pl.ds(start, nloc), gl], vcd_ref[:, gl]], axis=0) for gl in groups]
    _paired_attention(q_ref[...], kds, vds, sink_ref, l, valid, o_ref)


def _lat_attention(q, k, v, cache_k, cache_v, sink, *, l, batch, seq):
    nq = LAT_QUERIES
    nb = seq // nq
    past = cache_k.shape[2]
    kv_spec = pl.BlockSpec((seq, 2 * KV_W), lambda b, i: (b, 0))
    cache_spec = pl.BlockSpec((None, None, past, KV_W), lambda b, i: (b, l, 0, 0))
    return pl.pallas_call(
        functools.partial(_lat_attn_kernel, l=l, seq=seq),
        grid=(batch, nb),
        in_specs=[
            pl.BlockSpec(memory_space=pltpu.SMEM),
            pl.BlockSpec((nq, ATTN_W), lambda b, i: (b * nb + i, 0)),
            kv_spec, kv_spec, cache_spec, cache_spec,
        ],
        out_specs=pl.BlockSpec((nq, ATTN_W), lambda b, i: (b * nb + i, 0)),
        out_shape=jax.ShapeDtypeStruct((batch * seq, ATTN_W), BF16),
        scratch_shapes=[pltpu.VMEM((past, 2 * KV_W), BF16), pltpu.VMEM((past, 2 * KV_W), BF16)],
        compiler_params=_params(("arbitrary", "arbitrary")),
        name="lat_attention",
    )(sink, q, k, v, cache_k, cache_v)


def _rope_tables(seq):
    rows = seq // GRID_W
    row = jnp.repeat(jnp.arange(rows), GRID_W)
    col = jnp.arange(rows * GRID_W) % GRID_W
    nf = HEAD_DIM // 4
    inv = ROPE_BASE ** (-jnp.arange(nf, dtype=F32) / nf)
    ang_r = row.astype(F32)[:, None] * inv[None, :]
    ang_c = col.astype(F32)[:, None] * inv[None, :]
    cos_h = jnp.concatenate([jnp.cos(ang_r)] * 2 + [jnp.cos(ang_c)] * 2, axis=1)
    sin_h = jnp.concatenate([-jnp.sin(ang_r), jnp.sin(ang_r), -jnp.sin(ang_c), jnp.sin(ang_c)], axis=1)
    return jnp.tile(cos_h, (1, KV_HEADS)), jnp.tile(sin_h, (1, KV_HEADS))


def _sigmoid_tanh(x):
    return 0.5 * jnp.tanh(0.5 * x) + 0.5


def _conv_kernel(cv_ref, dw_ref, dwb_ref, lng_ref, lnb_ref, pw_ref, o_ref, pad_ref, *, seq):
    x = cv_ref[...]
    zeros = jnp.zeros((CONV_PAD, CONV_W), F32)
    pad_ref[0:CONV_PAD, :] = zeros
    pad_ref[CONV_PAD:CONV_PAD + seq, :] = x[:, :CONV_W] * _sigmoid_tanh(x[:, CONV_W:])
    pad_ref[CONV_PAD + seq:2 * CONV_PAD + seq, :] = zeros
    dw = dw_ref[...]
    pw = pw_ref[...].astype(BF16)
    first = CONV_PAD - CONV_K // 2

    def chunk(c, carry):
        base = pl.multiple_of(c * CONV_CHUNK, CONV_CHUNK)
        win = pad_ref[pl.ds(base, CONV_CHUNK + 2 * CONV_PAD), :]
        acc = jnp.zeros((CONV_CHUNK, CONV_W), F32)
        for r in range(SUBLANE):
            offs = [first + k for k in range(CONV_K) if (first + k) % SUBLANE == r]
            if not offs:
                continue
            shifted = win if r == 0 else pltpu.roll(win, win.shape[0] - r, 0)
            for off in offs:
                acc = acc + shifted[off - r:off - r + CONV_CHUNK] * dw[off - first:off - first + 1]
        acc = acc + dwb_ref[...]
        mu = jnp.mean(acc, axis=-1, keepdims=True)
        cen = acc - mu
        var = jnp.mean(cen * cen, axis=-1, keepdims=True)
        y = cen * lax.rsqrt(var + EPS) * lng_ref[...] + lnb_ref[...]
        o_ref[pl.ds(base, CONV_CHUNK), :] = _dot((y * _sigmoid_tanh(y)).astype(BF16), pw).astype(BF16)
        return carry

    lax.fori_loop(0, seq // CONV_CHUNK, chunk, 0, unroll=min(4, seq // CONV_CHUNK))


def _conformer_conv(cv, dw, dwb, lng, lnb, pw, *, l, batch, seq):
    kpad = 2 * CONV_PAD
    dw_p = jnp.pad(dw, ((0, 0), (0, kpad - CONV_K), (0, 0)))
    vec = lambda a: a.reshape(DEPTH, 1, CONV_W)
    vspec = pl.BlockSpec((None, 1, CONV_W), lambda b: (l, 0, 0))
    return pl.pallas_call(
        functools.partial(_conv_kernel, seq=seq),
        grid=(batch,),
        in_specs=[
            pl.BlockSpec((seq, 2 * CONV_W), lambda b: (b, 0)),
            pl.BlockSpec((None, kpad, CONV_W), lambda b: (l, 0, 0)),
            vspec, vspec, vspec,
            pl.BlockSpec((None, CONV_W, CONV_W), lambda b: (l, 0, 0)),
        ],
        out_specs=pl.BlockSpec((seq, CONV_W), lambda b: (b, 0)),
        out_shape=jax.ShapeDtypeStruct((batch * seq, CONV_W), BF16),
        scratch_shapes=[pltpu.VMEM((seq + 2 * CONV_PAD, CONV_W), F32)],
        compiler_params=_params(("arbitrary",)),
        name="conformer_conv",
    )(cv, dw_p, vec(dwb), vec(lng), vec(lnb), pw)


def _short_conv3(x, w, b):
    n = x.shape[0]
    row = lax.broadcasted_iota(jnp.int32, x.shape, 0)
    prev = jnp.where(row == 0, 0.0, pltpu.roll(x, 1, 0))
    nxt = jnp.where(row == n - 1, 0.0, pltpu.roll(x, n - 1, 0))
    return prev * w[0:1] + x * w[1:2] + nxt * w[2:3] + b


def _filter_kernel(z_ref, w1_ref, b1_ref, f1_ref, w2_ref, b2_ref, f2_ref, w3_ref, ld_ref, o_ref):
    z = z_ref[...]
    h = jnp.sin(f1_ref[...] * (_dot_hi(z, w1_ref[...]) + b1_ref[...]))
    h = jnp.sin(f2_ref[...] * (_dot_hi(h, w2_ref[...]) + b2_ref[...]))
    h = _dot_hi(h, w3_ref[...])
    tn = z[:, 0:1]
    h = h * jnp.exp(-tn * jnp.exp(ld_ref[...]))
    ss = jnp.sum(h * h, axis=0, keepdims=True)
    c = HYENA_W
    scale = []
    for o in range(2):
        tot = ss[:, 2 * o * c:(2 * o + 1) * c] + ss[:, (2 * o + 1) * c:(2 * o + 2) * c]
        r = lax.rsqrt(tot + EPS)
        scale += [r, r]
    h = h * jnp.concatenate(scale, axis=1)
    row = lax.broadcasted_iota(jnp.int32, h.shape, 0)
    col = lax.broadcasted_iota(jnp.int32, h.shape, 1)
    o_ref[...] = jnp.where((row == 0) & ((col // c) % 2 == 1), 0.0, h)


def _hyena_features(seq):
    t = jnp.arange(seq, dtype=F32)
    tn = t / (seq - 1)
    bands = jnp.linspace(1e-4, HYENA_BANDS - 1, HYENA_BANDS, dtype=F32)
    ang = 2.0 * math.pi * t[:, None] * bands[None, :] / seq
    z = jnp.concatenate([tn[:, None], jnp.cos(ang), -jnp.sin(ang)], axis=-1)
    return jnp.pad(z, ((0, 0), (0, LANE - HYENA_EMB)))


def _hyena_filters(z, w1, b1, f1, w2, b2, f2, w3, log_decay):
    seq = z.shape[0]
    n = w3.shape[1]
    return pl.pallas_call(
        _filter_kernel,
        out_shape=jax.ShapeDtypeStruct((seq, n), F32),
        compiler_params=_params(None, VMEM_LIMIT),
        name="hyena_filters",
    )(z, w1, b1, f1, w2, b2, f2, w3, log_decay)


def _spectrum_kernel(m_ref, h_ref, lr_ref, li_ref, hr_ref, hi_ref, mid_ref, *, n_fft):
    j = pl.program_id(0)
    fc = m_ref.shape[1]
    c = HYENA_W
    h16 = h_ref[...].astype(BF16)
    h = h16.astype(F32)
    hh = jnp.concatenate([h16, _alternate_rows(h).astype(BF16)], axis=1)
    gc = _dot(m_ref[0], hh)
    gs = _dot(m_ref[1], hh)
    row = j * fc + lax.broadcasted_iota(jnp.int32, (fc, c), 0)
    wgt = jnp.where(row == 0, 1.0 / n_fft, 2.0 / n_fft)
    t4 = lax.broadcasted_iota(jnp.int32, h.shape, 0) % 4
    mid_c = jnp.sum(jnp.where(t4 == 0, h, jnp.where(t4 == 2, -h, 0.0)), axis=0, keepdims=True)
    mid_s = jnp.sum(jnp.where(t4 == 3, h, jnp.where(t4 == 1, -h, 0.0)), axis=0, keepdims=True)
    for o in range(2):
        f = slice(2 * o * c, (2 * o + 1) * c)
        b = slice((2 * o + 1) * c, (2 * o + 2) * c)
        fa = slice(4 * c + 2 * o * c, 4 * c + (2 * o + 1) * c)
        ba = slice(4 * c + (2 * o + 1) * c, 4 * c + (2 * o + 2) * c)
        lr_ref[o] = (gc[:, f] + gc[:, b]) * wgt
        li_ref[o] = (gs[:, f] - gs[:, b]) * wgt
        hr_ref[o] = (gc[:, fa] + gc[:, ba]) * wgt
        hi_ref[o] = (gs[:, ba] - gs[:, fa]) * wgt
        mid_ref[o] = jnp.concatenate([mid_c[:, f] + mid_c[:, b], mid_s[:, f] - mid_s[:, b]],
                                     axis=0) * (2.0 / n_fft)


def _filter_spectra(dft, hcat, *, fc):
    half, seq = dft.shape[1:]
    c = HYENA_W
    fc = min(fc, half)
    out_spec = pl.BlockSpec((2, fc, c), lambda j: (0, j, 0))
    shape = jax.ShapeDtypeStruct((2, half, c), F32)
    return pl.pallas_call(
        functools.partial(_spectrum_kernel, n_fft=2 * seq),
        grid=(half // fc,),
        in_specs=[
            pl.BlockSpec((2, fc, seq), lambda j: (0, j, 0)),
            _resident(hcat.shape, lambda j: (0, 0)),
        ],
        out_specs=[out_spec] * 4 + [pl.BlockSpec((2, 2, c), lambda j: (0, 0, 0))],
        out_shape=[shape] * 4 + [jax.ShapeDtypeStruct((2, 2, c), F32)],
        compiler_params=_params(("arbitrary",), VMEM_LIMIT),
        name="hyena_spectra",
    )(dft, hcat)


def _alternate_rows(x):
    row = lax.broadcasted_iota(jnp.int32, x.shape, 0)
    return jnp.where(row % 2 == 0, x, -x)


def _longconv_split(ue, uo, m_ref, tw_ref, lr_ref, li_ref, hr_ref, hi_ref, mid_ref):
    shape = ue.shape
    tile = lambda a: jnp.concatenate([a] * (shape[1] // a.shape[1]), axis=1)
    wc = tile(tw_ref[0])
    ws = tile(tw_ref[1])
    ue16 = ue.astype(BF16)
    uo16 = uo.astype(BF16)
    er = _dot(m_ref[0], ue16)
    ei = _dot(m_ref[1], ue16)
    orr = _dot(m_ref[0], uo16)
    oi = _dot(m_ref[1], uo16)
    sign = _alternate_rows(jnp.ones(shape, F32))
    en = jnp.sum(ue16.astype(F32) * sign, axis=0, keepdims=True)
    on = jnp.sum(uo16.astype(F32) * sign, axis=0, keepdims=True)
    tr = wc * orr + ws * oi
    ti = wc * oi - ws * orr
    pr, pi, mr, mi = er + tr, ei + ti, er - tr, ei - ti
    lr, li, hr, hi = lr_ref[...], li_ref[...], hr_ref[...], hi_ref[...]
    ylr = pr * lr - pi * li
    yli = pr * li + pi * lr
    yhr = mr * hr + mi * hi
    yhi = mr * hi - mi * hr
    dr = ylr - yhr
    di = yli + yhi
    ger = (ylr + yhr).astype(BF16)
    gei = (yli - yhi).astype(BF16)
    gor = (dr * wc - di * ws).astype(BF16)
    goi = (dr * ws + di * wc).astype(BF16)
    ym_r = en * mid_ref[0:1] + on * mid_ref[1:2]
    ym_i = en * mid_ref[1:2] - on * mid_ref[0:1]
    ye = _dot(m_ref[0], ger) + _dot(m_ref[1], gei) + sign * ym_r
    yo = _dot(m_ref[0], gor) + _dot(m_ref[1], goi) - sign * ym_i
    return ye, yo


def _deinterleave(x, buf_ref):
    half = x.shape[0] // 2
    groups = buf_ref.shape[0]
    for g in range(groups):
        buf_ref[g] = x[:, g * LANE:(g + 1) * LANE]
    pick = lambda first: jnp.concatenate(
        [buf_ref[g, pl.ds(first, half, stride=2), :] for g in range(groups)], axis=1)
    return pick(0), pick(1)


def _interleave(even, odd, buf_ref):
    half = even.shape[0]
    groups = buf_ref.shape[0]
    for g in range(groups):
        buf_ref[g, pl.ds(0, half, stride=2), :] = even[:, g * LANE:(g + 1) * LANE]
        buf_ref[g, pl.ds(1, half, stride=2), :] = odd[:, g * LANE:(g + 1) * LANE]
    return jnp.concatenate([buf_ref[g] for g in range(groups)], axis=1)


def _hyena_conv0_kernel(hv_ref, hx_ref, sw_ref, sb_ref, bias_ref, m_ref, tw_ref, lr_ref, li_ref, hr_ref, hi_ref,
                        mid_ref, y_ref, vbuf_ref, xbuf_ref, *, seq):
    c = HYENA_W
    half = seq // 2
    for bi in range(hv_ref.shape[0] // seq):
        rows = slice(bi * seq, (bi + 1) * seq)
        ve, vo = _deinterleave(_short_conv3(hv_ref[rows, :], sw_ref[:, 0:c], sb_ref[:, 0:c]), vbuf_ref)
        xe, xo = _deinterleave(_short_conv3(hx_ref[rows, :], sw_ref[:, c:2 * c], sb_ref[:, c:2 * c]), xbuf_ref)
        ce, co = _longconv_split(ve, vo, m_ref, tw_ref, lr_ref, li_ref, hr_ref, hi_ref, mid_ref)
        y_ref[bi * seq:bi * seq + half, :] = xe * (ce + bias_ref[0:1] * ve)
        y_ref[bi * seq + half:(bi + 1) * seq, :] = xo * (co + bias_ref[0:1] * vo)


def _hyena_conv1_kernel(y_ref, hx_ref, sw_ref, sb_ref, bias_ref, m_ref, tw_ref, lr_ref, li_ref, hr_ref, hi_ref,
                        mid_ref, o_ref, xbuf_ref, obuf_ref, *, seq):
    c = HYENA_W
    half = seq // 2
    for bi in range(y_ref.shape[0] // seq):
        rows = slice(bi * seq, (bi + 1) * seq)
        ye = y_ref[bi * seq:bi * seq + half, :]
        yo = y_ref[bi * seq + half:(bi + 1) * seq, :]
        xe, xo = _deinterleave(_short_conv3(hx_ref[rows, :], sw_ref[:, 2 * c:3 * c], sb_ref[:, 2 * c:3 * c]),
                               xbuf_ref)
        ce, co = _longconv_split(ye, yo, m_ref, tw_ref, lr_ref, li_ref, hr_ref, hi_ref, mid_ref)
        out = _interleave(xe * (ce + bias_ref[1:2] * ye), xo * (co + bias_ref[1:2] * yo), obuf_ref)
        o_ref[rows, :] = out.astype(BF16)


def _twiddles(seq):
    f = np.arange(seq // 2, dtype=np.float64)[:, None] * np.ones((1, LANE))
    ang = 2.0 * np.pi * f / (2 * seq)
    return jnp.asarray(np.stack([np.cos(ang), np.sin(ang)]), F32)


def _hyena_mixer(hy, sw, sb, bias, dft, twiddle, spectra, *, l, batch, seq, bg):
    c = HYENA_W
    t = batch * seq
    rows = bg * seq
    sw_p = jnp.pad(sw, ((0, 0), (0, SUBLANE - sw.shape[1]), (0, 0)))
    sb3 = sb.reshape(DEPTH, 1, 3 * c)
    chan = lambda ch: pl.BlockSpec((rows, c), lambda g: (g, ch))
    sw_spec = pl.BlockSpec((None, SUBLANE, 3 * c), lambda g: (l, 0, 0))
    sb_spec = pl.BlockSpec((None, 1, 3 * c), lambda g: (l, 0, 0))
    bias_spec = pl.BlockSpec((None, 2, c), lambda g: (l, 0, 0))
    half = seq // 2
    dft_spec = _resident(dft.shape, lambda g: (0, 0, 0))
    tw_spec = _resident(twiddle.shape, lambda g: (0, 0, 0))
    coef = lambda order: ([_resident((None, half, c), lambda g: (order, 0, 0))] * 4
                          + [pl.BlockSpec((None, 2, c), lambda g: (order, 0, 0))])
    params = _params(("arbitrary",), VMEM_LIMIT)
    seq_buf = pltpu.VMEM((c // LANE, seq, LANE), F32)

    y = pl.pallas_call(
        functools.partial(_hyena_conv0_kernel, seq=seq),
        grid=(batch // bg,),
        in_specs=[chan(0), chan(1), sw_spec, sb_spec, bias_spec, dft_spec, tw_spec] + coef(0),
        out_specs=chan(0),
        out_shape=jax.ShapeDtypeStruct((t, c), F32),
        scratch_shapes=[seq_buf, seq_buf],
        compiler_params=params,
        name="hyena_conv0",
    )(hy, hy, sw_p, sb3, bias, dft, twiddle, *spectra)

    return pl.pallas_call(
        functools.partial(_hyena_conv1_kernel, seq=seq),
        grid=(batch // bg,),
        in_specs=[chan(0), chan(2), sw_spec, sb_spec, bias_spec, dft_spec, tw_spec] + coef(1),
        out_specs=chan(0),
        out_shape=jax.ShapeDtypeStruct((t, c), BF16),
        scratch_shapes=[seq_buf, seq_buf],
        compiler_params=params,
        name="hyena_conv1",
    )(y, hy, sw_p, sb3, bias, dft, twiddle, *spectra)


def _dft_kernel(c1_ref, s1_ref, c0_ref, s0_ref, o_ref):
    c0 = c0_ref[...]
    s0 = s0_ref[...]
    for i in range(c1_ref.shape[0]):
        c1 = c1_ref[i:i + 1, :]
        s1 = s1_ref[i:i + 1, :]
        rows = slice(i * DFT_MINOR, (i + 1) * DFT_MINOR)
        o_ref[0, rows, :] = (c1 * c0 - s1 * s0).astype(BF16)
        o_ref[1, rows, :] = (-(s1 * c0 + c1 * s0)).astype(BF16)


def _dft_matrix(seq, rows=None):
    rows = seq if rows is None else rows
    n = 2 * seq
    t = np.arange(seq, dtype=np.int64)
    f1n = rows // DFT_MINOR
    ang1 = 2.0 * np.pi * ((np.arange(f1n)[:, None] * DFT_MINOR * t[None, :]) % n).astype(np.float64) / n
    ang0 = 2.0 * np.pi * ((np.arange(DFT_MINOR)[:, None] * t[None, :]) % n).astype(np.float64) / n
    tables = [jnp.asarray(f(a), F32) for a in (ang1, ang0) for f in (np.cos, np.sin)]
    nb = min(f1n, SUBLANE)
    coarse = pl.BlockSpec((nb, seq), lambda j: (j, 0))
    fine = pl.BlockSpec((DFT_MINOR, seq), lambda j: (0, 0))
    return pl.pallas_call(
        _dft_kernel,
        grid=(f1n // nb,),
        in_specs=[coarse, coarse, fine, fine],
        out_specs=pl.BlockSpec((2, nb * DFT_MINOR, seq), lambda j: (0, j, 0)),
        out_shape=jax.ShapeDtypeStruct((2, rows, seq), BF16),
        compiler_params=_params(("arbitrary",)),
        name="dft_matrix",
    )(*tables)


def _mixffn_kernel(x_ref, att_ref, cvo_ref, hyo_ref, mod_ref, wo_ref, g_ref,
                   wg_ref, wu_ref, wd_ref, gf_ref, o_ref, a_ref, *, final):
    m = mod_ref[...]
    o1 = ATTN_W
    o2 = ATTN_W + CONV_W
    rows = _row_halves(x_ref.shape[0])
    xs, hs = [], []
    for r in rows:
        mix = (_dot(att_ref[r, :], wo_ref[0:o1, :]) + _dot(cvo_ref[r, :], wo_ref[o1:o2, :])
               + _dot(hyo_ref[r, :], wo_ref[o2:, :]))
        x = x_ref[r, :] + m[5:6] * mix
        xs.append(x)
        hs.append(_mod_norm(x, g_ref[...], m[7:8], m[6:7]).astype(BF16))
    ys = _swiglu_chains(hs, rows, wg_ref, wu_ref, wd_ref, a_ref)
    for r, x, y in zip(rows, xs, ys):
        x = x + (0.5 * m[8:9]) * y
        if final:
            x = x * lax.rsqrt(jnp.mean(x * x, axis=-1, keepdims=True) + EPS) * gf_ref[...]
        o_ref[r, :] = x


def _mixffn(x, att, cvo, hyo, mod4, w_out, gain, wg, wu, wd, g_final, *, l, row0, rows_per_cond, final):
    t, d = x.shape
    tm = TOKEN_TILE
    tok = lambda w: pl.BlockSpec((tm, w), lambda i: (i, 0))
    wspec_in = _resident((None, d, D_FF), lambda i: (l, 0, 0))
    return pl.pallas_call(
        functools.partial(_mixffn_kernel, final=final),
        grid=(t // tm,),
        in_specs=[
            tok(d), tok(ATTN_W), tok(CONV_W), tok(HYENA_W),
            _mod_spec(l, row0, rows_per_cond, tm),
            _resident((None, d, d), lambda i: (l, 0, 0)),
            pl.BlockSpec((None, 1, d), lambda i: (l, 0, 0)),
            wspec_in, wspec_in,
            _resident((None, D_FF, d), lambda i: (l, 0, 0)),
            pl.BlockSpec((1, d), lambda i: (0, 0)),
        ],
        out_specs=tok(d),
        out_shape=jax.ShapeDtypeStruct((t, d), F32),
        scratch_shapes=[pltpu.VMEM((tm, D_FF), BF16)],
        compiler_params=_params(("arbitrary",), VMEM_LIMIT),
        name="mixffn",
    )(x, att, cvo, hyo, mod4, w_out, gain.reshape(DEPTH, 1, d), wg, wu, wd, g_final.reshape(1, d))


def _pad_cols(a, width):
    return jnp.pad(a, [(0, 0)] * (a.ndim - 1) + [(0, width - a.shape[-1])])


def _trunk(x, cache, mod4, P, *, batch, seq, row0, rows_per_cond):
    fc = min(seq, FREQ_CHUNK)
    bg = max(1, min(batch, LONGCONV_ROWS // seq, LONGCONV_SEQS))
    dft = _dft_matrix(seq, seq // 2)
    dft_half = _dft_matrix(seq // 2)
    twiddle = _twiddles(seq)
    z = _hyena_features(seq)
    rope_tables = None if cache is None else _rope_tables(seq)
    w_in = P['w_in']
    cond = dict(row0=row0, rows_per_cond=rows_per_cond)
    ks, vs = [], []
    for l in range(DEPTH):
        x, q, k, v, cv, hy = _ffn_inproj(x, mod4, P['g_ffn1'], P['g_mix'], P['w1_gate'], P['w1_up'],
                                         P['w1_down'], w_in, rope_tables, l=l, seq=seq, **cond)
        if cache is None:
            att = _ctx_attention(q, k, v, P['attn_sink'], l=l, batch=batch, seq=seq)
            ks.append(k)
            vs.append(v)
        else:
            att = _lat_attention(q, k, v, cache[0], cache[1], P['attn_sink'], l=l, batch=batch, seq=seq)
        cvo = _conformer_conv(cv, P['conv_dw'], P['conv_dw_b'], P['conv_ln_g'], P['conv_ln_b'],
                              P['conv_pw'], l=l, batch=batch, seq=seq)
        hcat = _hyena_filters(z, P['hy_w1'][l], P['hy_b1'][l], P['hy_f1'][l], P['hy_w2'][l],
                              P['hy_b2'][l], P['hy_f2'][l], P['hy_w3'][l], P['hy_log_decay'][l])
        spectra = _filter_spectra(dft, hcat, fc=fc)
        hyo = _hyena_mixer(hy, P['hy_short_w'], P['hy_short_b'], P['hy_bias'], dft_half, twiddle, spectra,
                           l=l, batch=batch, seq=seq, bg=bg)
        x = _mixffn(x, att, cvo, hyo, mod4, P['w_out'], P['g_ffn2'], P['w2_gate'], P['w2_up'],
                    P['w2_down'], P['g_final'], l=l, final=(l == DEPTH - 1), **cond)
    return x, ks, vs


def kernel(x_prompt, x_sample, c, cache_k, cache_v, c_ctx, w_mod, b_mod, g_ffn1, g_mix, g_ffn2,
           g_final, w1_gate, w1_up, w1_down, w2_gate, w2_up, w2_down, w_in, w_out, attn_sink,
           conv_dw, conv_dw_b, conv_ln_g, conv_ln_b, conv_pw, hy_short_w, hy_short_b,
           hy_w1, hy_b1, hy_f1, hy_w2, hy_b2, hy_f2, hy_w3, hy_log_decay, hy_bias):
    batch, seq, d = x_prompt.shape
    dec_batch, dec_seq, _ = x_sample.shape
    assert 1 + dec_batch <= COND_ROWS

    hid = LANE
    P = dict(
        g_ffn1=g_ffn1, g_mix=g_mix, g_ffn2=g_ffn2, g_final=g_final, attn_sink=attn_sink,
        w1_gate=w1_gate.astype(BF16), w1_up=w1_up.astype(BF16), w1_down=w1_down.astype(BF16),
        w2_gate=w2_gate.astype(BF16), w2_up=w2_up.astype(BF16), w2_down=w2_down.astype(BF16),
        w_in=w_in.astype(BF16), w_out=w_out.astype(BF16),
        conv_dw=conv_dw, conv_dw_b=conv_dw_b, conv_ln_g=conv_ln_g, conv_ln_b=conv_ln_b, conv_pw=conv_pw,
        hy_short_w=hy_short_w, hy_short_b=hy_short_b,
        hy_w1=jnp.pad(hy_w1, ((0, 0), (0, hid - HYENA_EMB), (0, hid - HYENA_HID))),
        hy_b1=_pad_cols(hy_b1, hid)[:, None, :], hy_f1=_pad_cols(hy_f1, hid)[:, None, :],
        hy_w2=jnp.pad(hy_w2, ((0, 0), (0, hid - HYENA_HID), (0, hid - HYENA_HID))),
        hy_b2=_pad_cols(hy_b2, hid)[:, None, :], hy_f2=_pad_cols(hy_f2, hid)[:, None, :],
        hy_w3=jnp.pad(hy_w3, ((0, 0), (0, hid - HYENA_HID), (0, 0))),
        hy_log_decay=hy_log_decay[:, None, :], hy_bias=hy_bias,
    )

    conds = jnp.concatenate([c_ctx[None, :], c, jnp.zeros((COND_ROWS - 1 - dec_batch, d), F32)], axis=0)
    mod = _modulation(conds, w_mod, b_mod)
    mod4 = mod.reshape(DEPTH, COND_ROWS, N_MOD, d)

    xp, ks, vs = _trunk(x_prompt.reshape(batch * seq, d), None, mod4, P,
                        batch=batch, seq=seq, row0=0, rows_per_cond=batch * seq)
    cache = (cache_k.reshape(*cache_k.shape[:3], KV_W), cache_v.reshape(*cache_v.shape[:3], KV_W))
    xs, _, _ = _trunk(x_sample.reshape(dec_batch * dec_seq, d), cache, mod4, P,
                      batch=dec_batch, seq=dec_seq, row0=1, rows_per_cond=dec_seq)

    kv_shape = (batch, DEPTH, seq, KV_HEADS, HEAD_DIM)
    new_k = jnp.stack([k.reshape(batch, seq, KV_W) for k in ks], axis=1).reshape(kv_shape)
    new_v = jnp.stack([v.reshape(batch, seq, KV_W) for v in vs], axis=1).reshape(kv_shape)
    return (xp.reshape(batch, seq, d), xs.reshape(dec_batch, dec_seq, d), new_k, new_v)
```

```python
import functools
import math

import numpy as np
import jax
import jax.numpy as jnp
from jax import lax
from jax.experimental import pallas as pl
from jax.experimental.pallas import tpu as pltpu

F32 = jnp.float32
BF16 = jnp.bfloat16

D_MODEL = 1024
DEPTH = 2
GRID_W = 64
N_HEADS = 8
KV_HEADS = 2
HEAD_DIM = 64
Q_PER_KV = N_HEADS // KV_HEADS
ATTN_W = N_HEADS * HEAD_DIM
KV_W = KV_HEADS * HEAD_DIM
CONV_W = D_MODEL // 4
HYENA_W = D_MODEL // 4
WINDOW = 128
BLOCK = 128
CONV_K = 31
HYENA_EMB = 33
HYENA_BANDS = (HYENA_EMB - 1) // 2
HYENA_HID = 64
D_FF = 2816
N_MOD = 9
ROPE_BASE = 10000.0
EPS = 1e-6
NEG_INF = -1e30
LOG2_E = math.log2(math.e)
Q_SCALE = HEAD_DIM ** -0.5 * LOG2_E

LANE = 128
SUBLANE = 8
MXU_W = 256
COND_ROWS = 16
TOKEN_TILE = 512
CONV_CHUNK = 128
CONV_ROWS = 1024
CONV_PAD = 16
DFT_MINOR = 64
FREQ_CHUNK = 512
LONGCONV_ROWS = 4096
LONGCONV_SEQS = 4
CTX_ATTN_SEQS = 2
LAT_QUERIES = 256
VMEM_LIMIT = 56 * 1024 * 1024


def _params(sem, vmem=None):
    return pltpu.CompilerParams(dimension_semantics=sem, vmem_limit_bytes=vmem)


def _silu(x):
    return x * jax.nn.sigmoid(x)


def _dot(a, b):
    return jnp.dot(a, b, preferred_element_type=F32)


def _dot_nt(a, b):
    return lax.dot_general(a, b, (((1,), (1,)), ((), ())), preferred_element_type=F32)


def _dot_hi(a, b):
    return jnp.dot(a, b, preferred_element_type=F32, precision=lax.Precision.HIGHEST)


def _mod_norm(x, gain, scale, shift):
    y = x * lax.rsqrt(jnp.mean(x * x, axis=-1, keepdims=True) + EPS)
    return (y * gain) * (1.0 + scale) + shift


def _mod_kernel(c_ref, w_ref, b_ref, o_ref):
    s = _silu(c_ref[...]).astype(BF16)
    o_ref[...] = _dot(s, w_ref[...].astype(BF16)) + b_ref[...]


def _modulation(conds, w_mod, b_mod):
    d = D_MODEL
    return pl.pallas_call(
        _mod_kernel,
        grid=(DEPTH, N_MOD),
        in_specs=[
            pl.BlockSpec((COND_ROWS, d), lambda l, j: (0, 0)),
            pl.BlockSpec((None, d, d), lambda l, j: (l, 0, j)),
            pl.BlockSpec((None, 1, d), lambda l, j: (l, 0, j)),
        ],
        out_specs=pl.BlockSpec((None, COND_ROWS, d), lambda l, j: (l, 0, j)),
        out_shape=jax.ShapeDtypeStruct((DEPTH, COND_ROWS, N_MOD * d), F32),
        compiler_params=_params(("arbitrary", "arbitrary")),
        name="modulation",
    )(conds, w_mod, b_mod.reshape(DEPTH, 1, N_MOD * d))


def _row_halves(n):
    return slice(0, n // 2), slice(n // 2, n)


def _swiglu_chains(hs, rows, wg_ref, wu_ref, wd_ref, a_ref):
    for c in range(D_FF // MXU_W):
        sl = slice(c * MXU_W, (c + 1) * MXU_W)
        for r, h in zip(rows, hs):
            a_ref[r, sl] = (_silu(_dot(h, wg_ref[:, sl])) * _dot(h, wu_ref[:, sl])).astype(BF16)
    return [_dot(a_ref[r, :], wd_ref[...]) for r in rows]


def _ffn1_then_norm(x_ref, mod_ref, g1_ref, g2_ref, wg_ref, wu_ref, wd_ref, o_ref, a_ref):
    m = mod_ref[...]
    rows = _row_halves(x_ref.shape[0])
    xs = [x_ref[r, :] for r in rows]
    hs = [_mod_norm(x, g1_ref[...], m[1:2], m[0:1]).astype(BF16) for x in xs]
    ys = _swiglu_chains(hs, rows, wg_ref, wu_ref, wd_ref, a_ref)
    out = []
    for r, x, y in zip(rows, xs, ys):
        x = x + (0.5 * m[2:3]) * y
        o_ref[r, :] = x
        out.append(_mod_norm(x, g2_ref[...], m[4:5], m[3:4]).astype(BF16))
    return rows, out


def _resident(shape, index_map):
    return pl.BlockSpec(shape, index_map, pipeline_mode=pl.Buffered(1))


def _mod_spec(l, row0, rows_per_cond, tm):
    return pl.BlockSpec((None, None, N_MOD, D_MODEL),
                        lambda i: (l, row0 + (i * tm) // rows_per_cond, 0, 0))


_IN_SPLITS = (ATTN_W, KV_W, KV_W, 2 * CONV_W, 3 * HYENA_W)
_IN_SPLITS_LAT = (ATTN_W, 2 * KV_W, 2 * KV_W, 2 * CONV_W, 3 * HYENA_W)


def _rope(x, cos, sin_signed):
    w = x.shape[-1]
    lane = lax.broadcasted_iota(jnp.int32, x.shape, 1)
    quarter = HEAD_DIM // 4
    partner = jnp.where(lane % (2 * quarter) < quarter,
                        pltpu.roll(x, w - quarter, 1), pltpu.roll(x, quarter, 1))
    return x * cos + partner * sin_signed


def _ffn_inproj_kernel(x_ref, mod_ref, g1_ref, g2_ref, wg_ref, wu_ref, wd_ref, w_ref,
                       o_ref, q_ref, k_ref, v_ref, cv_ref, hy_ref, a_ref):
    rows, hs = _ffn1_then_norm(x_ref, mod_ref, g1_ref, g2_ref, wg_ref, wu_ref, wd_ref, o_ref, a_ref)
    off = 0
    for ref, width in zip((q_ref, k_ref, v_ref, cv_ref, hy_ref), _IN_SPLITS):
        for r, h in zip(rows, hs):
            ref[r, :] = _dot(h, w_ref[:, off:off + width])
        off += width


def _ffn_inproj_lat_kernel(x_ref, mod_ref, g1_ref, g2_ref, wg_ref, wu_ref, wd_ref, w_ref, cos_ref, sin_ref,
                           o_ref, q_ref, k_ref, v_ref, cv_ref, hy_ref, a_ref):
    wq, wk, wv, wc, wh = _IN_SPLITS
    tile = lambda a, width: jnp.concatenate([a] * (width // a.shape[1]), axis=1)
    rows, hs = _ffn1_then_norm(x_ref, mod_ref, g1_ref, g2_ref, wg_ref, wu_ref, wd_ref, o_ref, a_ref)
    for r, h in zip(rows, hs):
        cos = cos_ref[r, :]
        sin = sin_ref[r, :]
        q = _rope(_dot(h, w_ref[:, 0:wq]), tile(cos, wq), tile(sin, wq))
        q_ref[r, :] = (q * Q_SCALE).astype(BF16)
        k_ref[r, :] = _dup_kv_heads(_rope(_dot(h, w_ref[:, wq:wq + wk]), cos, sin))
    off = wq + wk
    for r, h in zip(rows, hs):
        v_ref[r, :] = _dup_kv_heads(_dot(h, w_ref[:, off:off + wv]))
    off += wv
    for r, h in zip(rows, hs):
        cv_ref[r, :] = _dot(h, w_ref[:, off:off + wc])
    off += wc
    for r, h in zip(rows, hs):
        hy_ref[r, :] = _dot(h, w_ref[:, off:off + wh])


def _ffn_inproj(x, mod4, g_ffn, g_mix, wg, wu, wd, w_in, rope_tables, *, l, row0, rows_per_cond, seq):
    t, d = x.shape
    tm = TOKEN_TILE
    lat = rope_tables is not None
    splits = _IN_SPLITS_LAT if lat else _IN_SPLITS
    dtypes = (BF16, BF16, BF16, F32, F32) if lat else (F32,) * 5
    gain_spec = pl.BlockSpec((None, 1, d), lambda i: (l, 0, 0))
    wspec_in = _resident((None, d, D_FF), lambda i: (l, 0, 0))
    in_specs = [
        pl.BlockSpec((tm, d), lambda i: (i, 0)),
        _mod_spec(l, row0, rows_per_cond, tm),
        gain_spec, gain_spec,
        wspec_in, wspec_in,
        _resident((None, D_FF, d), lambda i: (l, 0, 0)),
        _resident((None, d, sum(_IN_SPLITS)), lambda i: (l, 0, 0)),
    ]
    args = [x, mod4, g_ffn.reshape(DEPTH, 1, d), g_mix.reshape(DEPTH, 1, d), wg, wu, wd, w_in]
    if lat:
        table_spec = pl.BlockSpec((tm, LANE), lambda i: (i % (seq // tm), 0))
        in_specs += [table_spec, table_spec]
        args += list(rope_tables)
    tok = lambda w: pl.BlockSpec((tm, w), lambda i: (i, 0))
    return pl.pallas_call(
        _ffn_inproj_lat_kernel if lat else _ffn_inproj_kernel,
        grid=(t // tm,),
        in_specs=in_specs,
        out_specs=[tok(d)] + [tok(w) for w in splits],
        out_shape=[jax.ShapeDtypeStruct((t, d), F32)]
        + [jax.ShapeDtypeStruct((t, w), dt) for w, dt in zip(splits, dtypes)],
        scratch_shapes=[pltpu.VMEM((tm, D_FF), BF16)],
        compiler_params=_params(("arbitrary",), VMEM_LIMIT),
        name="ffn_inproj",
    )(*args)


def _dup_kv_heads(x):
    swapped = pltpu.roll(x, HEAD_DIM, 1)
    low = lax.broadcasted_iota(jnp.int32, x.shape, 1) < HEAD_DIM
    return jnp.concatenate([jnp.where(low, x, swapped), jnp.where(low, swapped, x)], axis=1).astype(BF16)


def _paired_attention(q, kds, vds, sink_ref, l, valid, o_ref):
    n = q.shape[0]
    top = lax.broadcasted_iota(jnp.int32, (2 * n, 1), 0) < n
    low = lax.broadcasted_iota(jnp.int32, (2 * n, LANE), 1) < HEAD_DIM
    zero = jnp.zeros((2 * n, LANE), BF16)
    for g, (kd, vd) in enumerate(zip(kds, vds)):
        c0 = 2 * g * LANE
        lhs = jnp.concatenate([q[:, c0:c0 + LANE], q[:, c0 + LANE:c0 + 2 * LANE]], axis=0)
        scores = (_dot_nt(jnp.where(low, lhs, zero), kd), _dot_nt(jnp.where(low, zero, lhs), kd))
        outs = []
        for half, s in enumerate(scores):
            if valid is not None:
                nloc = valid.shape[1]
                s = jnp.concatenate([jnp.where(valid, s[:, :nloc], NEG_INF), s[:, nloc:]], axis=1)
            sk = jnp.where(top, sink_ref[l, 4 * g + half], sink_ref[l, 4 * g + 2 + half]) * LOG2_E
            mx = jnp.maximum(jnp.max(s, axis=-1, keepdims=True), sk)
            p = jnp.exp2(s - mx)
            den = jnp.sum(p, axis=-1, keepdims=True) + jnp.exp2(sk - mx)
            outs.append(_dot(p.astype(BF16), vd) / den)
        o = jnp.where(low, outs[0], outs[1]).astype(BF16)
        o_ref[:, c0:c0 + LANE] = o[:n]
        o_ref[:, c0 + LANE:c0 + 2 * LANE] = o[n:]


def _ctx_attn_kernel(sink_ref, q_ref, k_ref, v_ref, o_ref, *, l, seq):
    groups = [slice(g * LANE, (g + 1) * LANE) for g in range(KV_HEADS)]
    for bi in range(q_ref.shape[0] // seq):
        rows = slice(bi * seq, (bi + 1) * seq)
        q = (q_ref[rows, :] * Q_SCALE).astype(BF16)
        kd = _dup_kv_heads(k_ref[rows, :])
        vd = _dup_kv_heads(v_ref[rows, :])
        _paired_attention(q, [kd[:, gl] for gl in groups], [vd[:, gl] for gl in groups], sink_ref, l, None,
                          o_ref.at[rows, :])


def _ctx_attention(q, k, v, sink, *, l, batch, seq):
    rows = min(batch, CTX_ATTN_SEQS) * seq
    return pl.pallas_call(
        functools.partial(_ctx_attn_kernel, l=l, seq=seq),
        grid=(batch * seq // rows,),
        in_specs=[
            pl.BlockSpec(memory_space=pltpu.SMEM),
            pl.BlockSpec((rows, ATTN_W), lambda b: (b, 0)),
            pl.BlockSpec((rows, KV_W), lambda b: (b, 0)),
            pl.BlockSpec((rows, KV_W), lambda b: (b, 0)),
        ],
        out_specs=pl.BlockSpec((rows, ATTN_W), lambda b: (b, 0)),
        out_shape=jax.ShapeDtypeStruct((batch * seq, ATTN_W), BF16),
        compiler_params=_params(("arbitrary",)),
        name="ctx_attention",
    )(sink, q, k, v)


def _lat_attn_kernel(sink_ref, q_ref, k_ref, v_ref, kc_ref, vc_ref, o_ref, kcd_ref, vcd_ref, *, l, seq):
    i = pl.program_id(1)

    @pl.when(i == 0)
    def _():
        kcd_ref[...] = _dup_kv_heads(kc_ref[...])
        vcd_ref[...] = _dup_kv_heads(vc_ref[...])

    nq = q_ref.shape[0]
    nloc = nq + 2 * WINDOW
    q0 = i * nq
    start = pl.multiple_of(jnp.clip(q0 - WINDOW, 0, seq - nloc), WINDOW)
    jpos = start + lax.broadcasted_iota(jnp.int32, (2 * nq, nloc), 1)
    row = lax.broadcasted_iota(jnp.int32, (2 * nq, nloc), 0)
    valid = jnp.abs(jpos - (q0 + row % nq)) <= WINDOW
    groups = [slice(g * LANE, (g + 1) * LANE) for g in range(KV_HEADS)]
    kds = [jnp.concatenate([k_ref[pl.ds(start, nloc), gl], kcd_ref[:, gl]], axis=0) for gl in groups]
    vds = [jnp.concatenate([v_ref[pl.ds(start, nloc), gl], vcd_ref[:, gl]], axis=0) for gl in groups]
    _paired_attention(q_ref[...], kds, vds, sink_ref, l, valid, o_ref)


def _lat_attention(q, k, v, cache_k, cache_v, sink, *, l, batch, seq):
    nq = LAT_QUERIES
    nb = seq // nq
    past = cache_k.shape[2]
    kv_spec = pl.BlockSpec((seq, 2 * KV_W), lambda b, i: (b, 0))
    cache_spec = pl.BlockSpec((None, None, past, KV_W), lambda b, i: (b, l, 0, 0))
    return pl.pallas_call(
        functools.partial(_lat_attn_kernel, l=l, seq=seq),
        grid=(batch, nb),
        in_specs=[
            pl.BlockSpec(memory_space=pltpu.SMEM),
            pl.BlockSpec((nq, ATTN_W), lambda b, i: (b * nb + i, 0)),
            kv_spec, kv_spec, cache_spec, cache_spec,
        ],
        out_specs=pl.BlockSpec((nq, ATTN_W), lambda b, i: (b * nb + i, 0)),
        out_shape=jax.ShapeDtypeStruct((batch * seq, ATTN_W), BF16),
        scratch_shapes=[pltpu.VMEM((past, 2 * KV_W), BF16), pltpu.VMEM((past, 2 * KV_W), BF16)],
        compiler_params=_params(("arbitrary", "arbitrary")),
        name="lat_attention",
    )(sink, q, k, v, cache_k, cache_v)


def _rope_tables(seq):
    rows = seq // GRID_W
    row = jnp.repeat(jnp.arange(rows), GRID_W)
    col = jnp.arange(rows * GRID_W) % GRID_W
    nf = HEAD_DIM // 4
    inv = ROPE_BASE ** (-jnp.arange(nf, dtype=F32) / nf)
    ang_r = row.astype(F32)[:, None] * inv[None, :]
    ang_c = col.astype(F32)[:, None] * inv[None, :]
    cos_h = jnp.concatenate([jnp.cos(ang_r)] * 2 + [jnp.cos(ang_c)] * 2, axis=1)
    sin_h = jnp.concatenate([-jnp.sin(ang_r), jnp.sin(ang_r), -jnp.sin(ang_c), jnp.sin(ang_c)], axis=1)
    return jnp.tile(cos_h, (1, KV_HEADS)), jnp.tile(sin_h, (1, KV_HEADS))


def _sigmoid_tanh(x):
    return 0.5 * jnp.tanh(0.5 * x) + 0.5


def _conv_kernel(cv_ref, dw_ref, dwb_ref, lng_ref, lnb_ref, pw_ref, o_ref, pad_ref, *, seq):
    zeros = jnp.zeros((CONV_PAD, CONV_W), F32)
    dw = dw_ref[...]
    pw = pw_ref[...].astype(BF16)
    first = CONV_PAD - CONV_K // 2
    for bi in range(cv_ref.shape[0] // seq):
        x = cv_ref[bi * seq:(bi + 1) * seq, :]
        pad_ref[bi, 0:CONV_PAD, :] = zeros
        pad_ref[bi, CONV_PAD:CONV_PAD + seq, :] = x[:, :CONV_W] * _sigmoid_tanh(x[:, CONV_W:])
        pad_ref[bi, CONV_PAD + seq:2 * CONV_PAD + seq, :] = zeros
        lax.fori_loop(0, seq // CONV_CHUNK,
                      functools.partial(_conv_chunk, pad_ref.at[bi], dw, dwb_ref, lng_ref, lnb_ref, pw, o_ref,
                                        bi * seq, first),
                      0, unroll=min(4, seq // CONV_CHUNK))


def _conv_chunk(pad_ref, dw, dwb_ref, lng_ref, lnb_ref, pw, o_ref, row0, first, c, carry):
    base = pl.multiple_of(c * CONV_CHUNK, CONV_CHUNK)
    win = pad_ref[pl.ds(base, CONV_CHUNK + 2 * CONV_PAD), :]
    acc = jnp.zeros((CONV_CHUNK, CONV_W), F32)
    for r in range(SUBLANE):
        offs = [first + k for k in range(CONV_K) if (first + k) % SUBLANE == r]
        if not offs:
            continue
        shifted = win if r == 0 else pltpu.roll(win, win.shape[0] - r, 0)
        for off in offs:
            acc = acc + shifted[off - r:off - r + CONV_CHUNK] * dw[off - first:off - first + 1]
    acc = acc + dwb_ref[...]
    mu = jnp.mean(acc, axis=-1, keepdims=True)
    cen = acc - mu
    var = jnp.mean(cen * cen, axis=-1, keepdims=True)
    y = cen * lax.rsqrt(var + EPS) * lng_ref[...] + lnb_ref[...]
    out_rows = pl.ds(pl.multiple_of(row0 + base, CONV_CHUNK), CONV_CHUNK)
    o_ref[out_rows, :] = _dot((y * _sigmoid_tanh(y)).astype(BF16), pw).astype(BF16)
    return carry


def _conformer_conv(cv, dw, dwb, lng, lnb, pw, *, l, batch, seq):
    kpad = 2 * CONV_PAD
    dw_p = jnp.pad(dw, ((0, 0), (0, kpad - CONV_K), (0, 0)))
    vec = lambda a: a.reshape(DEPTH, 1, CONV_W)
    vspec = pl.BlockSpec((None, 1, CONV_W), lambda b: (l, 0, 0))
    nseq = max(1, min(batch, CONV_ROWS // seq))
    return pl.pallas_call(
        functools.partial(_conv_kernel, seq=seq),
        grid=(batch // nseq,),
        in_specs=[
            pl.BlockSpec((nseq * seq, 2 * CONV_W), lambda b: (b, 0)),
            pl.BlockSpec((None, kpad, CONV_W), lambda b: (l, 0, 0)),
            vspec, vspec, vspec,
            pl.BlockSpec((None, CONV_W, CONV_W), lambda b: (l, 0, 0)),
        ],
        out_specs=pl.BlockSpec((nseq * seq, CONV_W), lambda b: (b, 0)),
        out_shape=jax.ShapeDtypeStruct((batch * seq, CONV_W), BF16),
        scratch_shapes=[pltpu.VMEM((nseq, seq + 2 * CONV_PAD, CONV_W), F32)],
        compiler_params=_params(("arbitrary",)),
        name="conformer_conv",
    )(cv, dw_p, vec(dwb), vec(lng), vec(lnb), pw)


def _short_conv3(x, w, b):
    n = x.shape[0]
    row = lax.broadcasted_iota(jnp.int32, x.shape, 0)
    prev = jnp.where(row == 0, 0.0, pltpu.roll(x, 1, 0))
    nxt = jnp.where(row == n - 1, 0.0, pltpu.roll(x, n - 1, 0))
    return prev * w[0:1] + x * w[1:2] + nxt * w[2:3] + b


def _filter_kernel(z_ref, w1_ref, b1_ref, f1_ref, w2_ref, b2_ref, f2_ref, w3_ref, ld_ref, o_ref):
    z = z_ref[...]
    h = jnp.sin(f1_ref[...] * (_dot_hi(z, w1_ref[...]) + b1_ref[...]))
    h = jnp.sin(f2_ref[...] * (_dot_hi(h, w2_ref[...]) + b2_ref[...]))
    h = _dot_hi(h, w3_ref[...])
    tn = z[:, 0:1]
    h = h * jnp.exp(-tn * jnp.exp(ld_ref[...]))
    ss = jnp.sum(h * h, axis=0, keepdims=True)
    c = HYENA_W
    scale = []
    for o in range(2):
        tot = ss[:, 2 * o * c:(2 * o + 1) * c] + ss[:, (2 * o + 1) * c:(2 * o + 2) * c]
        r = lax.rsqrt(tot + EPS)
        scale += [r, r]
    h = h * jnp.concatenate(scale, axis=1)
    row = lax.broadcasted_iota(jnp.int32, h.shape, 0)
    col = lax.broadcasted_iota(jnp.int32, h.shape, 1)
    o_ref[...] = jnp.where((row == 0) & ((col // c) % 2 == 1), 0.0, h)


def _hyena_features(seq):
    t = jnp.arange(seq, dtype=F32)
    tn = t / (seq - 1)
    bands = jnp.linspace(1e-4, HYENA_BANDS - 1, HYENA_BANDS, dtype=F32)
    ang = 2.0 * math.pi * t[:, None] * bands[None, :] / seq
    z = jnp.concatenate([tn[:, None], jnp.cos(ang), -jnp.sin(ang)], axis=-1)
    return jnp.pad(z, ((0, 0), (0, LANE - HYENA_EMB)))


def _hyena_filters(z, w1, b1, f1, w2, b2, f2, w3, log_decay):
    seq = z.shape[0]
    n = w3.shape[1]
    return pl.pallas_call(
        _filter_kernel,
        out_shape=jax.ShapeDtypeStruct((seq, n), F32),
        compiler_params=_params(None, VMEM_LIMIT),
        name="hyena_filters",
    )(z, w1, b1, f1, w2, b2, f2, w3, log_decay)


def _spectrum_kernel(m_ref, h_ref, lr_ref, li_ref, hr_ref, hi_ref, mid_ref, *, n_fft):
    j = pl.program_id(0)
    fc = m_ref.shape[1]
    c = HYENA_W
    h16 = h_ref[...].astype(BF16)
    h = h16.astype(F32)
    hh = jnp.concatenate([h16, _alternate_rows(h).astype(BF16)], axis=1)
    gc = _dot(m_ref[0], hh)
    gs = _dot(m_ref[1], hh)
    row = j * fc + lax.broadcasted_iota(jnp.int32, (fc, c), 0)
    wgt = jnp.where(row == 0, 1.0 / n_fft, 2.0 / n_fft)
    t4 = lax.broadcasted_iota(jnp.int32, h.shape, 0) % 4
    mid_c = jnp.sum(jnp.where(t4 == 0, h, jnp.where(t4 == 2, -h, 0.0)), axis=0, keepdims=True)
    mid_s = jnp.sum(jnp.where(t4 == 3, h, jnp.where(t4 == 1, -h, 0.0)), axis=0, keepdims=True)
    for o in range(2):
        f = slice(2 * o * c, (2 * o + 1) * c)
        b = slice((2 * o + 1) * c, (2 * o + 2) * c)
        fa = slice(4 * c + 2 * o * c, 4 * c + (2 * o + 1) * c)
        ba = slice(4 * c + (2 * o + 1) * c, 4 * c + (2 * o + 2) * c)
        lr_ref[o] = (gc[:, f] + gc[:, b]) * wgt
        li_ref[o] = (gs[:, f] - gs[:, b]) * wgt
        hr_ref[o] = (gc[:, fa] + gc[:, ba]) * wgt
        hi_ref[o] = (gs[:, ba] - gs[:, fa]) * wgt
        mid_ref[o] = jnp.concatenate([mid_c[:, f] + mid_c[:, b], mid_s[:, f] - mid_s[:, b]],
                                     axis=0) * (2.0 / n_fft)


def _filter_spectra(dft, hcat, *, fc):
    half, seq = dft.shape[1:]
    c = HYENA_W
    fc = min(fc, half)
    out_spec = pl.BlockSpec((2, fc, c), lambda j: (0, j, 0))
    shape = jax.ShapeDtypeStruct((2, half, c), F32)
    return pl.pallas_call(
        functools.partial(_spectrum_kernel, n_fft=2 * seq),
        grid=(half // fc,),
        in_specs=[
            pl.BlockSpec((2, fc, seq), lambda j: (0, j, 0)),
            _resident(hcat.shape, lambda j: (0, 0)),
        ],
        out_specs=[out_spec] * 4 + [pl.BlockSpec((2, 2, c), lambda j: (0, 0, 0))],
        out_shape=[shape] * 4 + [jax.ShapeDtypeStruct((2, 2, c), F32)],
        compiler_params=_params(("arbitrary",), VMEM_LIMIT),
        name="hyena_spectra",
    )(dft, hcat)


def _alternate_rows(x):
    row = lax.broadcasted_iota(jnp.int32, x.shape, 0)
    return jnp.where(row % 2 == 0, x, -x)


def _longconv_split(ue, uo, m_ref, tw_ref, lr_ref, li_ref, hr_ref, hi_ref, mid_ref):
    shape = ue.shape
    tile = lambda a: jnp.concatenate([a] * (shape[1] // a.shape[1]), axis=1)
    wc = tile(tw_ref[0])
    ws = tile(tw_ref[1])
    ue16 = ue.astype(BF16)
    uo16 = uo.astype(BF16)
    er = _dot(m_ref[0], ue16)
    ei = _dot(m_ref[1], ue16)
    orr = _dot(m_ref[0], uo16)
    oi = _dot(m_ref[1], uo16)
    sign = _alternate_rows(jnp.ones(shape, F32))
    en = jnp.sum(ue16.astype(F32) * sign, axis=0, keepdims=True)
    on = jnp.sum(uo16.astype(F32) * sign, axis=0, keepdims=True)
    tr = wc * orr + ws * oi
    ti = wc * oi - ws * orr
    pr, pi, mr, mi = er + tr, ei + ti, er - tr, ei - ti
    lr, li, hr, hi = lr_ref[...], li_ref[...], hr_ref[...], hi_ref[...]
    ylr = pr * lr - pi * li
    yli = pr * li + pi * lr
    yhr = mr * hr + mi * hi
    yhi = mr * hi - mi * hr
    dr = ylr - yhr
    di = yli + yhi
    ger = (ylr + yhr).astype(BF16)
    gei = (yli - yhi).astype(BF16)
    gor = (dr * wc - di * ws).astype(BF16)
    goi = (dr * ws + di * wc).astype(BF16)
    ym_r = en * mid_ref[0:1] + on * mid_ref[1:2]
    ym_i = en * mid_ref[1:2] - on * mid_ref[0:1]
    ye = _dot(m_ref[0], ger) + _dot(m_ref[1], gei) + sign * ym_r
    yo = _dot(m_ref[0], gor) + _dot(m_ref[1], goi) - sign * ym_i
    return ye, yo


def _deinterleave(x, buf_ref):
    half = x.shape[0] // 2
    groups = buf_ref.shape[0]
    for g in range(groups):
        buf_ref[g] = x[:, g * LANE:(g + 1) * LANE]
    pick = lambda first: jnp.concatenate(
        [buf_ref[g, pl.ds(first, half, stride=2), :] for g in range(groups)], axis=1)
    return pick(0), pick(1)


def _interleave(even, odd, buf_ref):
    half = even.shape[0]
    groups = buf_ref.shape[0]
    for g in range(groups):
        buf_ref[g, pl.ds(0, half, stride=2), :] = even[:, g * LANE:(g + 1) * LANE]
        buf_ref[g, pl.ds(1, half, stride=2), :] = odd[:, g * LANE:(g + 1) * LANE]
    return jnp.concatenate([buf_ref[g] for g in range(groups)], axis=1)


def _hyena_conv0_kernel(hv_ref, hx_ref, sw_ref, sb_ref, bias_ref, m_ref, tw_ref, lr_ref, li_ref, hr_ref, hi_ref,
                        mid_ref, y_ref, vbuf_ref, xbuf_ref, *, seq):
    c = HYENA_W
    half = seq // 2
    for bi in range(hv_ref.shape[0] // seq):
        rows = slice(bi * seq, (bi + 1) * seq)
        ve, vo = _deinterleave(_short_conv3(hv_ref[rows, :], sw_ref[:, 0:c], sb_ref[:, 0:c]), vbuf_ref)
        xe, xo = _deinterleave(_short_conv3(hx_ref[rows, :], sw_ref[:, c:2 * c], sb_ref[:, c:2 * c]), xbuf_ref)
        ce, co = _longconv_split(ve, vo, m_ref, tw_ref, lr_ref, li_ref, hr_ref, hi_ref, mid_ref)
        y_ref[bi * seq:bi * seq + half, :] = xe * (ce + bias_ref[0:1] * ve)
        y_ref[bi * seq + half:(bi + 1) * seq, :] = xo * (co + bias_ref[0:1] * vo)


def _hyena_conv1_kernel(y_ref, hx_ref, sw_ref, sb_ref, bias_ref, m_ref, tw_ref, lr_ref, li_ref, hr_ref, hi_ref,
                        mid_ref, o_ref, xbuf_ref, obuf_ref, *, seq):
    c = HYENA_W
    half = seq // 2
    for bi in range(y_ref.shape[0] // seq):
        rows = slice(bi * seq, (bi + 1) * seq)
        ye = y_ref[bi * seq:bi * seq + half, :]
        yo = y_ref[bi * seq + half:(bi + 1) * seq, :]
        xe, xo = _deinterleave(_short_conv3(hx_ref[rows, :], sw_ref[:, 2 * c:3 * c], sb_ref[:, 2 * c:3 * c]),
                               xbuf_ref)
        ce, co = _longconv_split(ye, yo, m_ref, tw_ref, lr_ref, li_ref, hr_ref, hi_ref, mid_ref)
        out = _interleave(xe * (ce + bias_ref[1:2] * ye), xo * (co + bias_ref[1:2] * yo), obuf_ref)
        o_ref[rows, :] = out.astype(BF16)


def _twiddles(seq):
    f = np.arange(seq // 2, dtype=np.float64)[:, None] * np.ones((1, LANE))
    ang = 2.0 * np.pi * f / (2 * seq)
    return jnp.asarray(np.stack([np.cos(ang), np.sin(ang)]), F32)


def _hyena_mixer(hy, sw, sb, bias, dft, twiddle, spectra, *, l, batch, seq, bg):
    c = HYENA_W
    t = batch * seq
    rows = bg * seq
    sw_p = jnp.pad(sw, ((0, 0), (0, SUBLANE - sw.shape[1]), (0, 0)))
    sb3 = sb.reshape(DEPTH, 1, 3 * c)
    chan = lambda ch: pl.BlockSpec((rows, c), lambda g: (g, ch))
    sw_spec = pl.BlockSpec((None, SUBLANE, 3 * c), lambda g: (l, 0, 0))
    sb_spec = pl.BlockSpec((None, 1, 3 * c), lambda g: (l, 0, 0))
    bias_spec = pl.BlockSpec((None, 2, c), lambda g: (l, 0, 0))
    half = seq // 2
    dft_spec = _resident(dft.shape, lambda g: (0, 0, 0))
    tw_spec = _resident(twiddle.shape, lambda g: (0, 0, 0))
    coef = lambda order: ([_resident((None, half, c), lambda g: (order, 0, 0))] * 4
                          + [pl.BlockSpec((None, 2, c), lambda g: (order, 0, 0))])
    params = _params(("arbitrary",), VMEM_LIMIT)
    seq_buf = pltpu.VMEM((c // LANE, seq, LANE), F32)

    y = pl.pallas_call(
        functools.partial(_hyena_conv0_kernel, seq=seq),
        grid=(batch // bg,),
        in_specs=[chan(0), chan(1), sw_spec, sb_spec, bias_spec, dft_spec, tw_spec] + coef(0),
        out_specs=chan(0),
        out_shape=jax.ShapeDtypeStruct((t, c), F32),
        scratch_shapes=[seq_buf, seq_buf],
        compiler_params=params,
        name="hyena_conv0",
    )(hy, hy, sw_p, sb3, bias, dft, twiddle, *spectra)

    return pl.pallas_call(
        functools.partial(_hyena_conv1_kernel, seq=seq),
        grid=(batch // bg,),
        in_specs=[chan(0), chan(2), sw_spec, sb_spec, bias_spec, dft_spec, tw_spec] + coef(1),
        out_specs=chan(0),
        out_shape=jax.ShapeDtypeStruct((t, c), BF16),
        scratch_shapes=[seq_buf, seq_buf],
        compiler_params=params,
        name="hyena_conv1",
    )(y, hy, sw_p, sb3, bias, dft, twiddle, *spectra)


def _dft_kernel(c1_ref, s1_ref, c0_ref, s0_ref, o_ref):
    c0 = c0_ref[...]
    s0 = s0_ref[...]
    for i in range(c1_ref.shape[0]):
        c1 = c1_ref[i:i + 1, :]
        s1 = s1_ref[i:i + 1, :]
        rows = slice(i * DFT_MINOR, (i + 1) * DFT_MINOR)
        o_ref[0, rows, :] = (c1 * c0 - s1 * s0).astype(BF16)
        o_ref[1, rows, :] = (-(s1 * c0 + c1 * s0)).astype(BF16)


def _dft_matrix(seq, rows=None):
    rows = seq if rows is None else rows
    n = 2 * seq
    t = np.arange(seq, dtype=np.int64)
    f1n = rows // DFT_MINOR
    ang1 = 2.0 * np.pi * ((np.arange(f1n)[:, None] * DFT_MINOR * t[None, :]) % n).astype(np.float64) / n
    ang0 = 2.0 * np.pi * ((np.arange(DFT_MINOR)[:, None] * t[None, :]) % n).astype(np.float64) / n
    tables = [jnp.asarray(f(a), F32) for a in (ang1, ang0) for f in (np.cos, np.sin)]
    nb = min(f1n, SUBLANE)
    coarse = pl.BlockSpec((nb, seq), lambda j: (j, 0))
    fine = pl.BlockSpec((DFT_MINOR, seq), lambda j: (0, 0))
    return pl.pallas_call(
        _dft_kernel,
        grid=(f1n // nb,),
        in_specs=[coarse, coarse, fine, fine],
        out_specs=pl.BlockSpec((2, nb * DFT_MINOR, seq), lambda j: (0, j, 0)),
        out_shape=jax.ShapeDtypeStruct((2, rows, seq), BF16),
        compiler_params=_params(("arbitrary",)),
        name="dft_matrix",
    )(*tables)


def _mixffn_kernel(x_ref, att_ref, cvo_ref, hyo_ref, mod_ref, wo_ref, g_ref,
                   wg_ref, wu_ref, wd_ref, gf_ref, o_ref, a_ref, *, final):
    m = mod_ref[...]
    o1 = ATTN_W
    o2 = ATTN_W + CONV_W
    rows = _row_halves(x_ref.shape[0])
    xs, hs = [], []
    for r in rows:
        mix = (_dot(att_ref[r, :], wo_ref[0:o1, :]) + _dot(cvo_ref[r, :], wo_ref[o1:o2, :])
               + _dot(hyo_ref[r, :], wo_ref[o2:, :]))
        x = x_ref[r, :] + m[5:6] * mix
        xs.append(x)
        hs.append(_mod_norm(x, g_ref[...], m[7:8], m[6:7]).astype(BF16))
    ys = _swiglu_chains(hs, rows, wg_ref, wu_ref, wd_ref, a_ref)
    for r, x, y in zip(rows, xs, ys):
        x = x + (0.5 * m[8:9]) * y
        if final:
            x = x * lax.rsqrt(jnp.mean(x * x, axis=-1, keepdims=True) + EPS) * gf_ref[...]
        o_ref[r, :] = x


def _mixffn(x, att, cvo, hyo, mod4, w_out, gain, wg, wu, wd, g_final, *, l, row0, rows_per_cond, final):
    t, d = x.shape
    tm = TOKEN_TILE
    tok = lambda w: pl.BlockSpec((tm, w), lambda i: (i, 0))
    wspec_in = _resident((None, d, D_FF), lambda i: (l, 0, 0))
    return pl.pallas_call(
        functools.partial(_mixffn_kernel, final=final),
        grid=(t // tm,),
        in_specs=[
            tok(d), tok(ATTN_W), tok(CONV_W), tok(HYENA_W),
            _mod_spec(l, row0, rows_per_cond, tm),
            _resident((None, d, d), lambda i: (l, 0, 0)),
            pl.BlockSpec((None, 1, d), lambda i: (l, 0, 0)),
            wspec_in, wspec_in,
            _resident((None, D_FF, d), lambda i: (l, 0, 0)),
            pl.BlockSpec((1, d), lambda i: (0, 0)),
        ],
        out_specs=tok(d),
        out_shape=jax.ShapeDtypeStruct((t, d), F32),
        scratch_shapes=[pltpu.VMEM((tm, D_FF), BF16)],
        compiler_params=_params(("arbitrary",), VMEM_LIMIT),
        name="mixffn",
    )(x, att, cvo, hyo, mod4, w_out, gain.reshape(DEPTH, 1, d), wg, wu, wd, g_final.reshape(1, d))


def _pad_cols(a, width):
    return jnp.pad(a, [(0, 0)] * (a.ndim - 1) + [(0, width - a.shape[-1])])


def _trunk(x, cache, mod4, P, *, batch, seq, row0, rows_per_cond):
    fc = min(seq, FREQ_CHUNK)
    bg = max(1, min(batch, LONGCONV_ROWS // seq, LONGCONV_SEQS))
    dft = _dft_matrix(seq, seq // 2)
    dft_half = _dft_matrix(seq // 2)
    twiddle = _twiddles(seq)
    z = _hyena_features(seq)
    rope_tables = None if cache is None else _rope_tables(seq)
    w_in = P['w_in']
    cond = dict(row0=row0, rows_per_cond=rows_per_cond)
    ks, vs = [], []
    for l in range(DEPTH):
        x, q, k, v, cv, hy = _ffn_inproj(x, mod4, P['g_ffn1'], P['g_mix'], P['w1_gate'], P['w1_up'],
                                         P['w1_down'], w_in, rope_tables, l=l, seq=seq, **cond)
        if cache is None:
            att = _ctx_attention(q, k, v, P['attn_sink'], l=l, batch=batch, seq=seq)
            ks.append(k)
            vs.append(v)
        else:
            att = _lat_attention(q, k, v, cache[0], cache[1], P['attn_sink'], l=l, batch=batch, seq=seq)
        cvo = _conformer_conv(cv, P['conv_dw'], P['conv_dw_b'], P['conv_ln_g'], P['conv_ln_b'],
                              P['conv_pw'], l=l, batch=batch, seq=seq)
        hcat = _hyena_filters(z, P['hy_w1'][l], P['hy_b1'][l], P['hy_f1'][l], P['hy_w2'][l],
                              P['hy_b2'][l], P['hy_f2'][l], P['hy_w3'][l], P['hy_log_decay'][l])
        spectra = _filter_spectra(dft, hcat, fc=fc)
        hyo = _hyena_mixer(hy, P['hy_short_w'], P['hy_short_b'], P['hy_bias'], dft_half, twiddle, spectra,
                           l=l, batch=batch, seq=seq, bg=bg)
        x = _mixffn(x, att, cvo, hyo, mod4, P['w_out'], P['g_ffn2'], P['w2_gate'], P['w2_up'],
                    P['w2_down'], P['g_final'], l=l, final=(l == DEPTH - 1), **cond)
    return x, ks, vs


def kernel(x_prompt, x_sample, c, cache_k, cache_v, c_ctx, w_mod, b_mod, g_ffn1, g_mix, g_ffn2,
           g_final, w1_gate, w1_up, w1_down, w2_gate, w2_up, w2_down, w_in, w_out, attn_sink,
           conv_dw, conv_dw_b, conv_ln_g, conv_ln_b, conv_pw, hy_short_w, hy_short_b,
           hy_w1, hy_b1, hy_f1, hy_w2, hy_b2, hy_f2, hy_w3, hy_log_decay, hy_bias):
    batch, seq, d = x_prompt.shape
    dec_batch, dec_seq, _ = x_sample.shape
    assert 1 + dec_batch <= COND_ROWS

    hid = LANE
    P = dict(
        g_ffn1=g_ffn1, g_mix=g_mix, g_ffn2=g_ffn2, g_final=g_final, attn_sink=attn_sink,
        w1_gate=w1_gate.astype(BF16), w1_up=w1_up.astype(BF16), w1_down=w1_down.astype(BF16),
        w2_gate=w2_gate.astype(BF16), w2_up=w2_up.astype(BF16), w2_down=w2_down.astype(BF16),
        w_in=w_in.astype(BF16), w_out=w_out.astype(BF16),
        conv_dw=conv_dw, conv_dw_b=conv_dw_b, conv_ln_g=conv_ln_g, conv_ln_b=conv_ln_b, conv_pw=conv_pw,
        hy_short_w=hy_short_w, hy_short_b=hy_short_b,
        hy_w1=jnp.pad(hy_w1, ((0, 0), (0, hid - HYENA_EMB), (0, hid - HYENA_HID))),
        hy_b1=_pad_cols(hy_b1, hid)[:, None, :], hy_f1=_pad_cols(hy_f1, hid)[:, None, :],
        hy_w2=jnp.pad(hy_w2, ((0, 0), (0, hid - HYENA_HID), (0, hid - HYENA_HID))),
        hy_b2=_pad_cols(hy_b2, hid)[:, None, :], hy_f2=_pad_cols(hy_f2, hid)[:, None, :],
        hy_w3=jnp.pad(hy_w3, ((0, 0), (0, hid - HYENA_HID), (0, 0))),
        hy_log_decay=hy_log_decay[:, None, :], hy_bias=hy_bias,
    )

    conds = jnp.concatenate([c_ctx[None, :], c, jnp.zeros((COND_ROWS - 1 - dec_batch, d), F32)], axis=0)
    mod = _modulation(conds, w_mod, b_mod)
    mod4 = mod.reshape(DEPTH, COND_ROWS, N_MOD, d)

    xp, ks, vs = _trunk(x_prompt.reshape(batch * seq, d), None, mod4, P,
                        batch=batch, seq=seq, row0=0, rows_per_cond=batch * seq)
    cache = (cache_k.reshape(*cache_k.shape[:3], KV_W), cache_v.reshape(*cache_v.shape[:3], KV_W))
    xs, _, _ = _trunk(x_sample.reshape(dec_batch * dec_seq, d), cache, mod4, P,
                      batch=dec_batch, seq=dec_seq, row0=1, rows_per_cond=dec_seq)

    kv_shape = (batch, DEPTH, seq, KV_HEADS, HEAD_DIM)
    new_k = jnp.stack([k.reshape(batch, seq, KV_W) for k in ks], axis=1).reshape(kv_shape)
    new_v = jnp.stack([v.reshape(batch, seq, KV_W) for v in vs], axis=1).reshape(kv_shape)
    return (xp.reshape(batch, seq, d), xs.reshape(dec_batch, dec_seq, d), new_k, new_v)
```
